```python
import math
import jax, jax.numpy as jnp
from jax import lax
import numpy as np

D_MODEL = 1024
BATCH = 4
SEQ = 8192
DEPTH = 1

MOBA_HEADS = 8
MOBA_HEAD_DIM = 64
MOBA_BLOCK = 256
MOBA_TOPK = 3
MOBA_QCHUNK = 64
SWA_Q_HEADS = 8
SWA_KV_HEADS = 2
SWA_HEAD_DIM = 64
SWA_WINDOW = 128
SWA_BLOCK = 128
MEM_HEADS = 4
MEM_HEAD_DIM = 128
MEM_LEN = 256
NUM_BUCKETS = 32
MAX_EXACT = NUM_BUCKETS // 2
REL_MAX_DISTANCE = 128
N_SELF_HEADS = MOBA_HEADS + SWA_Q_HEADS
FFN_HIDDEN = 2816
CONV_WIDTH = 3
RMS_EPS = 1e-6
MOBA_W = MOBA_HEADS * MOBA_HEAD_DIM
SWA_QW = SWA_Q_HEADS * SWA_HEAD_DIM
SWA_KVW = SWA_KV_HEADS * SWA_HEAD_DIM
MEM_W = MEM_HEADS * MEM_HEAD_DIM
IN_SPLITS = (MOBA_W, MOBA_W, MOBA_W, SWA_QW, SWA_KVW, SWA_KVW, MEM_W, D_MODEL, D_MODEL, D_MODEL)
IN_WIDTH = 3 * MOBA_W + SWA_QW + 2 * SWA_KVW + MEM_W + 3 * D_MODEL

kernel_name = "hybrid_moba_swa_mem_convffn"


def rmsnorm(x, g):
    xf = x.astype(jnp.float32)
    r = lax.rsqrt(jnp.mean(xf * xf, axis=-1, keepdims=True) + RMS_EPS)
    return (xf * r * g.astype(jnp.float32)).astype(x.dtype)


def t5_bucket(dist):
    n = jnp.maximum(dist, 0)
    nf = jnp.maximum(n, 1).astype(jnp.float32)
    large = MAX_EXACT + (jnp.log(nf / MAX_EXACT) / math.log(REL_MAX_DISTANCE / MAX_EXACT)
                         * (NUM_BUCKETS - MAX_EXACT)).astype(jnp.int32)
    large = jnp.minimum(large, NUM_BUCKETS - 1)
    return jnp.where(n < MAX_EXACT, n, large)


def moba_attention(q, k, v, bias_table):
    B_, S_, H, dh = q.shape
    Sp = -(-S_ // MOBA_BLOCK) * MOBA_BLOCK
    pad = Sp - S_
    if pad:
        padw = ((0, 0), (0, pad), (0, 0), (0, 0))
        q, k, v = jnp.pad(q, padw), jnp.pad(k, padw), jnp.pad(v, padw)
    NB = Sp // MOBA_BLOCK
    scale = 1.0 / math.sqrt(dh)
    qh = q.transpose(0, 2, 1, 3)
    kb = k.reshape(B_, NB, MOBA_BLOCK, H, dh).transpose(0, 3, 1, 2, 4)
    vb = v.reshape(B_, NB, MOBA_BLOCK, H, dh).transpose(0, 3, 1, 2, 4)
    kmean = jnp.mean(kb.astype(jnp.float32), axis=3)
    gate = jnp.einsum('bhsd,bhnd->bhsn', qh.astype(jnp.float32), kmean)
    qblk = jnp.arange(Sp) // MOBA_BLOCK
    past = jnp.arange(NB)[None, :] < qblk[:, None]
    gate = jnp.where(past, gate, -jnp.inf)
    k_sel = min(MOBA_TOPK, NB)
    _, sel = lax.top_k(gate, k_sel)
    sel_valid = jnp.arange(k_sel)[None, :] < qblk[:, None]
    bi = jnp.arange(B_)[:, None, None, None]
    hi = jnp.arange(H)[None, :, None, None]
    hb = jnp.arange(H)[None, :, None, None, None]
    kpos_in = jnp.arange(MOBA_BLOCK)

    def chunk(i):
        q0 = i * MOBA_QCHUNK
        qc = lax.dynamic_slice_in_dim(qh, q0, MOBA_QCHUNK, axis=2)
        selc = lax.dynamic_slice_in_dim(sel, q0, MOBA_QCHUNK, axis=2)
        validc = lax.dynamic_slice_in_dim(sel_valid, q0, MOBA_QCHUNK, axis=0)
        c = q0 // MOBA_BLOCK
        kg = kb[bi, hi, selc]
        vg = vb[bi, hi, selc]
        k_own = lax.dynamic_index_in_dim(kb, c, axis=2, keepdims=False)
        v_own = lax.dynamic_index_in_dim(vb, c, axis=2, keepdims=False)
        qpos = q0 + jnp.arange(MOBA_QCHUNK)
        kpos_sel = selc[..., None] * MOBA_BLOCK + kpos_in
        s_sel = (jnp.einsum('bhqd,bhqnkd->bhqnk', qc, kg).astype(jnp.float32) * scale
                 + bias_table[hb, t5_bucket(qpos[:, None, None] - kpos_sel)].astype(jnp.float32))
        s_sel = jnp.where(validc[:, :, None], s_sel, -jnp.inf)
        dist_own = qpos[:, None] - (c * MOBA_BLOCK + kpos_in)[None, :]
        s_own = (jnp.einsum('bhqd,bhkd->bhqk', qc, k_own).astype(jnp.float32) * scale
                 + bias_table[:, t5_bucket(dist_own)].astype(jnp.float32))
        s_own = jnp.where(dist_own >= 0, s_own, -jnp.inf)
        s_all = jnp.concatenate(
            [s_sel.reshape(B_, H, MOBA_QCHUNK, k_sel * MOBA_BLOCK), s_own], axis=-1)
        p = jax.nn.softmax(s_all, axis=-1).astype(v.dtype)
        p_sel = p[..., :k_sel * MOBA_BLOCK].reshape(B_, H, MOBA_QCHUNK, k_sel, MOBA_BLOCK)
        p_own = p[..., k_sel * MOBA_BLOCK:]
        return (jnp.einsum('bhqnk,bhqnkd->bhqd', p_sel, vg)
                + jnp.einsum('bhqk,bhkd->bhqd', p_own, v_own))

    outs = lax.map(chunk, jnp.arange(Sp // MOBA_QCHUNK))
    out = outs.transpose(1, 0, 3, 2, 4).reshape(B_, Sp, H, dh)
    return out[:, :S_]


def swa_attention(q, k, v, sinks, bias_table):
    B_, S_, HQ, dh = q.shape
    HKV = k.shape[2]
    G = HQ // HKV
    QB = SWA_BLOCK
    nblk = S_ // QB
    scale = 1.0 / math.sqrt(dh)
    qb = q.reshape(B_, nblk, QB, HKV, G, dh)
    kb = k.reshape(B_, nblk, QB, HKV, dh)
    vb = v.reshape(B_, nblk, QB, HKV, dh)
    padw = ((0, 0), (1, 0), (0, 0), (0, 0), (0, 0))
    kband = jnp.concatenate([jnp.pad(kb, padw)[:, :-1], kb], axis=2)
    vband = jnp.concatenate([jnp.pad(vb, padw)[:, :-1], vb], axis=2)
    s = jnp.einsum('bnqhgd,bnkhd->bhgnqk', qb, kband).astype(jnp.float32) * scale
    dist = QB + jnp.arange(QB)[:, None] - jnp.arange(2 * QB)[None, :]
    bias = bias_table[:, t5_bucket(dist)].astype(jnp.float32).reshape(HKV, G, 1, QB, 2 * QB)
    s = s + bias
    kpos = (jnp.arange(nblk)[:, None] - 1) * QB + jnp.arange(2 * QB)[None, :]
    allowed = ((dist >= 0) & (dist < SWA_WINDOW))[None] & (kpos >= 0)[:, None, :]
    s = jnp.where(allowed, s, -jnp.inf)
    sink = sinks.astype(jnp.float32).reshape(HKV, G, 1, 1, 1)
    m = jnp.maximum(jnp.max(s, axis=-1, keepdims=True), sink)
    p = jnp.exp(s - m)
    denom = jnp.sum(p, axis=-1, keepdims=True) + jnp.exp(sink - m)
    p = (p / denom).astype(v.dtype)
    o = jnp.einsum('bhgnqk,bnkhd->bnqhgd', p, vband)
    return o.reshape(B_, S_, HQ * dh)


def memory_attention(q, km, vm):
    B_, S_, Hm, dm = q.shape
    s = jnp.einsum('bshd,bmhd->bhsm', q, km).astype(jnp.float32) * (1.0 / math.sqrt(dm))
    p = jax.nn.softmax(s, axis=-1).astype(vm.dtype)
    return jnp.einsum('bhsm,bmhd->bshd', p, vm).reshape(B_, S_, Hm * dm)


def causal_depthwise_conv(u, w, b):
    S_ = u.shape[1]
    up = jnp.pad(u, ((0, 0), (CONV_WIDTH - 1, 0), (0, 0)))
    out = b.astype(u.dtype)
    for t in range(CONV_WIDTH):
        out = out + w[t].astype(u.dtype) * up[:, t:t + S_]
    return out


def _normal(k, shape, scale):
    return jax.random.normal(k, shape, jnp.float32) * scale


def setup_inputs(seed: int = 0) -> dict:
    key = jax.random.key(seed)
    ks = jax.random.split(key, 20)
    L, D, F = DEPTH, D_MODEL, FFN_HIDDEN
    return {
        "x": _normal(ks[0], (BATCH, SEQ, D), 1.0),
        "mem": _normal(ks[1], (BATCH, MEM_LEN, D), 1.0),
        "norm_mix_pre": 1.0 + _normal(ks[2], (L, D), 0.05),
        "norm_mix_post": 1.0 + _normal(ks[3], (L, D), 0.05),
        "norm_ffn_pre": 1.0 + _normal(ks[4], (L, D), 0.05),
        "norm_ffn_post": 1.0 + _normal(ks[5], (L, D), 0.05),
        "norm_mem": 1.0 + _normal(ks[6], (L, D), 0.05),
        "w_in": _normal(ks[7], (L, D, IN_WIDTH), D ** -0.5),
        "rel_bias": _normal(ks[8], (NUM_BUCKETS, N_SELF_HEADS), 0.3),
        "swa_sinks": _normal(ks[9], (L, SWA_Q_HEADS), 1.0),
        "w_mem_kv": _normal(ks[10], (L, D, 2 * MEM_W), D ** -0.5),
        "w_branch_moba": _normal(ks[11], (L, MOBA_W, D), MOBA_W ** -0.5),
        "w_branch_swa": _normal(ks[12], (L, SWA_QW, D), SWA_QW ** -0.5),
        "w_branch_mem": _normal(ks[13], (L, MEM_W, D), MEM_W ** -0.5),
        "w_out": _normal(ks[14], (L, D, D), D ** -0.5),
        "w_ffn_up": _normal(ks[15], (L, D, 2 * F), D ** -0.5),
        "ffn_conv_w": _normal(ks[16], (L, CONV_WIDTH, 2 * F), CONV_WIDTH ** -0.5),
        "ffn_conv_b": _normal(ks[17], (L, 2 * F), 0.02),
        "w_ffn_down": _normal(ks[18], (L, F, D), F ** -0.5),
    }


def reference(x, mem, norm_mix_pre, norm_mix_post, norm_ffn_pre, norm_ffn_post, norm_mem,
              w_in, rel_bias, swa_sinks, w_mem_kv, w_branch_moba, w_branch_swa, w_branch_mem,
              w_out, w_ffn_up, ffn_conv_w, ffn_conv_b, w_ffn_down):
    B_, S_, _ = x.shape
    cuts = []
    acc = 0
    for width in IN_SPLITS[:-1]:
        acc += width
        cuts.append(acc)
    bias_moba = rel_bias[:, :MOBA_HEADS].T
    bias_swa = rel_bias[:, MOBA_HEADS:].T
    for l in range(DEPTH):
        h = rmsnorm(x, norm_mix_pre[l])
        proj = h @ w_in[l]
        (q_mb, k_mb, v_mb, q_sw, k_sw, v_sw, q_me,
         g_mb, g_sw, g_me) = jnp.split(proj, cuts, axis=-1)
        o_mb = moba_attention(
            q_mb.reshape(B_, S_, MOBA_HEADS, MOBA_HEAD_DIM),
            k_mb.reshape(B_, S_, MOBA_HEADS, MOBA_HEAD_DIM),
            v_mb.reshape(B_, S_, MOBA_HEADS, MOBA_HEAD_DIM), bias_moba).reshape(B_, S_, MOBA_W)
        o_sw = swa_attention(
            q_sw.reshape(B_, S_, SWA_Q_HEADS, SWA_HEAD_DIM),
            k_sw.reshape(B_, S_, SWA_KV_HEADS, SWA_HEAD_DIM),
            v_sw.reshape(B_, S_, SWA_KV_HEADS, SWA_HEAD_DIM), swa_sinks[l], bias_swa)
        kv_me = rmsnorm(mem, norm_mem[l]) @ w_mem_kv[l]
        M_ = mem.shape[1]
        k_me = kv_me[..., :MEM_W].reshape(B_, M_, MEM_HEADS, MEM_HEAD_DIM)
        v_me = kv_me[..., MEM_W:].reshape(B_, M_, MEM_HEADS, MEM_HEAD_DIM)
        o_me = memory_attention(q_me.reshape(B_, S_, MEM_HEADS, MEM_HEAD_DIM), k_me, v_me)
        merged = (jax.nn.sigmoid(g_mb) * (o_mb @ w_branch_moba[l])
                  + jax.nn.sigmoid(g_sw) * (o_sw @ w_branch_swa[l])
                  + jax.nn.sigmoid(g_me) * (o_me @ w_branch_mem[l]))
        x = x + rmsnorm(merged @ w_out[l], norm_mix_post[l])
        h = rmsnorm(x, norm_ffn_pre[l])
        u = causal_depthwise_conv(h @ w_ffn_up[l], ffn_conv_w[l], ffn_conv_b[l])
        u_gate, u_val = jnp.split(u, 2, axis=-1)
        y = (jax.nn.gelu(u_gate, approximate=True) * u_val) @ w_ffn_down[l]
        x = x + rmsnorm(y, norm_ffn_post[l])
    return x
```

```python
import functools
import math

import jax
import jax.numpy as jnp
from jax import lax
from jax.experimental import pallas as pl
from jax.experimental.pallas import tpu as pltpu

D_MODEL = 1024
MOBA_HEADS = 8
MOBA_HEAD_DIM = 64
MOBA_BLOCK = 256
MOBA_TOPK = 3
SWA_Q_HEADS = 8
SWA_KV_HEADS = 2
SWA_HEAD_DIM = 64
SWA_WINDOW = 128
SWA_BLOCK = 128
MEM_HEADS = 4
MEM_HEAD_DIM = 128
NUM_BUCKETS = 32
MAX_EXACT = NUM_BUCKETS // 2
REL_MAX_DISTANCE = 128
FFN_HIDDEN = 2816
CONV_WIDTH = 3
RMS_EPS = 1e-6

MOBA_W = MOBA_HEADS * MOBA_HEAD_DIM
SWA_QW = SWA_Q_HEADS * SWA_HEAD_DIM
SWA_KVW = SWA_KV_HEADS * SWA_HEAD_DIM
MEM_W = MEM_HEADS * MEM_HEAD_DIM

LANES = 128
NEG = -1e30
VMEM_LIMIT = 56 * 1024 * 1024
ROW_TILE = 512
MOBA_KEYS_PER_STEP = 512
FFN_CHUNK = 256
HALO = 8

BF16 = jnp.bfloat16
F32 = jnp.float32

OFF_QM = 0
OFF_KA = OFF_QM + MOBA_HEADS * LANES
OFF_VM = OFF_KA + MOBA_HEADS * LANES
OFF_QS = OFF_VM + MOBA_W
OFF_KS = OFF_QS + SWA_Q_HEADS * LANES
OFF_VS = OFF_KS + SWA_KV_HEADS * LANES
OFF_QE = OFF_VS + SWA_KV_HEADS * LANES
PROJ_W = OFF_QE + MEM_W


def _mm(a, b):
    return jnp.dot(a, b, preferred_element_type=F32)


def _nt(a, b):
    return lax.dot_general(a, b, (((1,), (1,)), ((), ())), preferred_element_type=F32)


def _rms(xf, g):
    r = lax.rsqrt(jnp.mean(xf * xf, axis=-1, keepdims=True) + RMS_EPS)
    return xf * r * g


def _cparams(n_axes):
    return pltpu.CompilerParams(dimension_semantics=("arbitrary",) * n_axes, vmem_limit_bytes=VMEM_LIMIT)


def _resident(shape):
    nd = len(shape)
    return pl.BlockSpec(shape, lambda *_: (0,) * nd, pipeline_mode=pl.Buffered(1))


def _proj_kernel(x_ref, g_ref, w_ref, qm_ref, ka_ref, vm_ref, qs_ref, ks_ref, vs_ref, qe_ref, kmean_ref,
                 *, blocks_per_seq):
    i = pl.program_id(0)
    h = _rms(x_ref[...], g_ref[...]).astype(BF16)

    def seg(c0, n):
        return _mm(h, w_ref[:, c0:c0 + n])

    half = 4 * LANES
    lane = lax.broadcasted_iota(jnp.int32, (1, half), 1) & (LANES - 1)
    blocks_per_tile = ROW_TILE // MOBA_BLOCK
    for j in range(2):
        cs = slice(j * half, (j + 1) * half)
        qm_ref[:, cs] = seg(OFF_QM + j * half, half).astype(BF16)
        qs_ref[:, cs] = seg(OFF_QS + j * half, half).astype(BF16)
        k = seg(OFF_KA + j * half, half)
        for r in range(blocks_per_tile):
            rows = slice(r * MOBA_BLOCK, (r + 1) * MOBA_BLOCK)
            kb = k[rows]
            kmean_ref[0, r:r + 1, cs] = jnp.sum(kb, axis=0, keepdims=True) * (1.0 / MOBA_BLOCK)
            cb = (i * blocks_per_tile + r) % blocks_per_seq
            onehot = jnp.where((lane - MOBA_HEAD_DIM) == cb, 1.0, 0.0)
            ka_ref[rows, cs] = (kb + onehot).astype(BF16)
    vm_ref[...] = seg(OFF_VM, MOBA_W).astype(BF16)
    ks_ref[...] = seg(OFF_KS, SWA_KV_HEADS * LANES).astype(BF16)
    vs_ref[...] = seg(OFF_VS, SWA_KV_HEADS * LANES).astype(BF16)
    qe_ref[...] = seg(OFF_QE, MEM_W).astype(BF16)


def _proj(x2, g, w, seq_len):
    t = x2.shape[0]
    n_tiles = t // ROW_TILE
    bpt = ROW_TILE // MOBA_BLOCK
    row = lambda n: pl.BlockSpec((ROW_TILE, n), lambda i: (i, 0))
    out_shapes = (
        jax.ShapeDtypeStruct((t, MOBA_HEADS * LANES), BF16),
        jax.ShapeDtypeStruct((t, MOBA_HEADS * LANES), BF16),
        jax.ShapeDtypeStruct((t, MOBA_W), BF16),
        jax.ShapeDtypeStruct((t, SWA_Q_HEADS * LANES), BF16),
        jax.ShapeDtypeStruct((t, SWA_KV_HEADS * LANES), BF16),
        jax.ShapeDtypeStruct((t, SWA_KV_HEADS * LANES), BF16),
        jax.ShapeDtypeStruct((t, MEM_W), BF16),
        jax.ShapeDtypeStruct((n_tiles, bpt, MOBA_HEADS * LANES), F32),
    )
    out_specs = (row(MOBA_HEADS * LANES), row(MOBA_HEADS * LANES), row(MOBA_W), row(SWA_Q_HEADS * LANES),
                 row(SWA_KV_HEADS * LANES), row(SWA_KV_HEADS * LANES), row(MEM_W),
                 pl.BlockSpec((1, bpt, MOBA_HEADS * LANES), lambda i: (i, 0, 0)))
    return pl.pallas_call(
        functools.partial(_proj_kernel, blocks_per_seq=seq_len // MOBA_BLOCK),
        grid=(n_tiles,),
        in_specs=[row(D_MODEL), _resident((1, D_MODEL)), _resident((D_MODEL, PROJ_W))],
        out_specs=out_specs,
        out_shape=out_shapes,
        compiler_params=_cparams(1),
        name="proj",
    )(x2, g, w)


def _memkv_kernel(m_ref, g_ref, w_ref, o_ref):
    h = _rms(m_ref[...], g_ref[...]).astype(BF16)
    o_ref[...] = _mm(h, w_ref[...]).astype(BF16)


def _memkv(mem2, g, w):
    rows = mem2.shape[0]
    tile = min(rows, ROW_TILE)
    return pl.pallas_call(
        _memkv_kernel,
        grid=(rows // tile,),
        in_specs=[pl.BlockSpec((tile, D_MODEL), lambda i: (i, 0)), _resident((1, D_MODEL)),
                  _resident((D_MODEL, 2 * MEM_W))],
        out_specs=pl.BlockSpec((tile, 2 * MEM_W), lambda i: (i, 0)),
        out_shape=jax.ShapeDtypeStruct((rows, 2 * MEM_W), BF16),
        compiler_params=_cparams(1),
        name="memkv",
    )(mem2, g, w)


def _moba_kernel(qm_ref, ka_ref, vm_ref, kmh_ref, kml_ref, tab_ref, o_ref):
    c = pl.program_id(2)
    mb = MOBA_BLOCK
    kb = MOBA_KEYS_PER_STEP
    lane = lax.broadcasted_iota(jnp.int32, (mb, LANES), 1)
    lane_f = lane.astype(F32)
    blk = lane - MOBA_HEAD_DIM
    is_q = lane < MOBA_HEAD_DIM
    in_gate = (lane >= MOBA_HEAD_DIM) & (lane < MOBA_HEAD_DIM + 32)
    past = in_gate & (blk < c)
    kt = pl.multiple_of(jnp.maximum(c - 1, 0) * mb, mb)
    left = jnp.where(c == 0, 1, 0)
    right = jnp.where(c == 0, 2, 1)

    q_far = []
    state = []
    vt = vm_ref[0, pl.ds(kt, 2 * mb), :]
    for a in range(2):
        cols = slice(a * LANES, (a + 1) * LANES)
        qp = qm_ref[0, :, cols]
        gate = _mm(qp, kmh_ref[0, a]) + _mm(qp, kml_ref[0, a])
        g = jnp.where(past, gate, -jnp.inf)
        sel = lane < 0
        for _ in range(MOBA_TOPK):
            top = jnp.max(g, axis=-1, keepdims=True)
            first = jnp.min(jnp.where(g == top, lane_f, 1e9), axis=-1, keepdims=True)
            hit = lane_f == first
            sel = sel | hit
            g = jnp.where(hit, -jnp.inf, g)
        sel = sel & past
        qf = qp.astype(F32)
        mask_far = jnp.where(sel & (blk <= c - 2), 0.0, NEG)
        mask_tail = jnp.where((blk == c) | (sel & (blk == c - 1)), 0.0, NEG)
        q_far.append(jnp.where(is_q, qf, jnp.where(in_gate, mask_far, 0.0)).astype(BF16))
        q_tail = jnp.where(is_q, qf, jnp.where(in_gate, mask_tail, 0.0)).astype(BF16)

        k_tail = ka_ref[0, pl.ds(kt, 2 * mb), cols]
        s = _nt(q_tail, k_tail)
        s = jnp.concatenate([s[:, :mb] + tab_ref[a, left], s[:, mb:] + tab_ref[a, right]], axis=1)
        m0 = jnp.max(s, axis=-1, keepdims=True)
        p = jnp.exp(s - m0)
        l0 = jnp.sum(p, axis=-1, keepdims=True)
        state.append((m0, l0, _mm(p.astype(BF16), vt)))

    n_far = jnp.where(c >= 2, ((c - 1) * mb + kb - 1) // kb, 0)

    def body(j, carry):
        ks = pl.multiple_of(j * kb, kb)
        vj = vm_ref[0, pl.ds(ks, kb), :]
        out = []
        for a in range(2):
            m, l, acc = carry[a]
            kj = ka_ref[0, pl.ds(ks, kb), a * LANES:(a + 1) * LANES]
            s = _nt(q_far[a], kj)
            m_new = jnp.maximum(m, jnp.max(s, axis=-1, keepdims=True))
            alpha = jnp.exp(m - m_new)
            p = jnp.exp(s - m_new)
            l = alpha * l + jnp.sum(p, axis=-1, keepdims=True)
            acc = alpha * acc + _mm(p.astype(BF16), vj)
            out.append((m_new, l, acc))
        return tuple(out)

    (_, l_a, acc_a), (_, l_b, acc_b) = lax.fori_loop(0, n_far, body, tuple(state))
    o_ref[0] = jnp.where(is_q, acc_a / l_a, acc_b / l_b).astype(BF16)


def _moba(qm, ka, vm, km_hi, km_lo, tab):
    b, s, _ = qm.shape
    pairs = MOBA_HEADS // 2
    grid = (b, pairs, s // MOBA_BLOCK)
    return pl.pallas_call(
        _moba_kernel,
        grid=grid,
        in_specs=[
            pl.BlockSpec((1, MOBA_BLOCK, 2 * LANES), lambda bi, hp, c: (bi, c, hp)),
            pl.BlockSpec((1, s, 2 * LANES), lambda bi, hp, c: (bi, 0, hp)),
            pl.BlockSpec((1, s, LANES), lambda bi, hp, c: (bi, 0, hp)),
            pl.BlockSpec((1, 2, LANES, LANES), lambda bi, hp, c: (bi, hp, 0, 0)),
            pl.BlockSpec((1, 2, LANES, LANES), lambda bi, hp, c: (bi, hp, 0, 0)),
            pl.BlockSpec((2, 3, MOBA_BLOCK, MOBA_BLOCK), lambda bi, hp, c: (hp, 0, 0, 0)),
        ],
        out_specs=pl.BlockSpec((1, MOBA_BLOCK, LANES), lambda bi, hp, c: (bi, c, hp)),
        out_shape=jax.ShapeDtypeStruct((b, s, MOBA_W), BF16),
        compiler_params=_cparams(3),
        name="moba",
    )(qm, ka, vm, km_hi, km_lo, tab)


def _local_kernel(qs_ref, ks_ref, kh_ref, vs_ref, vh_ref, tab_ref, sink_ref, qe_ref, kv_ref, osw_ref, ome_ref):
    i = pl.program_id(1)
    qb = SWA_BLOCK
    group = SWA_Q_HEADS // SWA_KV_HEADS
    no_prev = jnp.where(i == 0, NEG, 0.0)
    lane = lax.broadcasted_iota(jnp.int32, (qb, LANES), 1)
    for blk in range(ROW_TILE // qb):
        r0 = blk * qb
        rows = slice(r0, r0 + qb)
        for hk in range(SWA_KV_HEADS):
            cols = slice(hk * LANES, (hk + 1) * LANES)
            if blk == 0:
                kk = jnp.concatenate([kh_ref[0, :, cols], ks_ref[0, 0:qb, cols]], axis=0)
                vv = jnp.concatenate([vh_ref[0, :, cols], vs_ref[0, 0:qb, cols]], axis=0)
            else:
                kk = ks_ref[0, r0 - qb:r0 + qb, cols]
                vv = vs_ref[0, r0 - qb:r0 + qb, cols]
            outs = []
            for gi in range(group):
                h = hk * group + gi
                q = qs_ref[0, rows, h * LANES:(h + 1) * LANES]
                s = _nt(q, kk) + tab_ref[h]
                if blk == 0:
                    s = jnp.concatenate([s[:, :qb] + no_prev, s[:, qb:]], axis=1)
                sink = sink_ref[h:h + 1, 0:1]
                m = jnp.maximum(jnp.max(s, axis=-1, keepdims=True), sink)
                p = jnp.exp(s - m)
                den = jnp.sum(p, axis=-1, keepdims=True) + jnp.exp(sink - m)
                outs.append(_mm(p.astype(BF16), vv) / den)
            for jj in range(group // 2):
                j = hk * (group // 2) + jj
                osw_ref[0, rows, j * LANES:(j + 1) * LANES] = jnp.where(
                    lane < SWA_HEAD_DIM, outs[2 * jj], outs[2 * jj + 1]).astype(BF16)

    scale = 1.0 / math.sqrt(MEM_HEAD_DIM)
    for hd in range(MEM_HEADS):
        cols = slice(hd * MEM_HEAD_DIM, (hd + 1) * MEM_HEAD_DIM)
        q = qe_ref[0, :, cols]
        kk = kv_ref[0, :, cols]
        vv = kv_ref[0, :, MEM_W + hd * MEM_HEAD_DIM:MEM_W + (hd + 1) * MEM_HEAD_DIM]
        s = _nt(q, kk) * scale
        m = jnp.max(s, axis=-1, keepdims=True)
        p = jnp.exp(s - m)
        den = jnp.sum(p, axis=-1, keepdims=True)
        ome_ref[0, :, cols] = (_mm(p.astype(BF16), vv) / den).astype(BF16)


def _local(qs, ks, vs, tab, sinks, qe, kv):
    b, s, _ = qs.shape
    per_tile = ROW_TILE // SWA_BLOCK
    tile = lambda n: pl.BlockSpec((1, ROW_TILE, n), lambda bi, i: (bi, i, 0))
    halo = pl.BlockSpec((1, SWA_BLOCK, SWA_KV_HEADS * LANES),
                        lambda bi, i: (bi, jnp.maximum(i * per_tile - 1, 0), 0))
    m_len = kv.shape[1]
    return pl.pallas_call(
        _local_kernel,
        grid=(b, s // ROW_TILE),
        in_specs=[
            tile(SWA_Q_HEADS * LANES),
            tile(SWA_KV_HEADS * LANES), halo,
            tile(SWA_KV_HEADS * LANES), halo,
            _resident((SWA_Q_HEADS, SWA_BLOCK, 2 * SWA_BLOCK)),
            _resident((SWA_Q_HEADS, LANES)),
            tile(MEM_W),
            pl.BlockSpec((1, m_len, 2 * MEM_W), lambda bi, i: (bi, 0, 0)),
        ],
        out_specs=(tile(SWA_QW), tile(MEM_W)),
        out_shape=(jax.ShapeDtypeStruct((b, s, SWA_QW), BF16), jax.ShapeDtypeStruct((b, s, MEM_W), BF16)),
        compiler_params=_cparams(2),
        name="local",
    )(qs, ks, ks, vs, vs, tab, sinks, qe, kv)


def _merge_kernel(x_ref, omb_ref, osw_ref, ome_ref, gpre_ref, gpost_ref, wg_ref, wb_ref, wo_ref, out_ref):
    x = x_ref[...]
    h = _rms(x, gpre_ref[...]).astype(BF16)
    branch = (omb_ref[...], osw_ref[...], ome_ref[...])
    half = D_MODEL // 2
    parts = []
    for nc in range(2):
        acc = None
        for br in range(3):
            c0 = br * D_MODEL + nc * half
            gate = jax.nn.sigmoid(_mm(h, wg_ref[:, c0:c0 + half]))
            term = gate * _mm(branch[br], wb_ref[br, :, nc * half:(nc + 1) * half])
            acc = term if acc is None else acc + term
        parts.append(acc.astype(BF16))
    y = _mm(jnp.concatenate(parts, axis=1), wo_ref[...])
    out_ref[...] = x + _rms(y, gpost_ref[...])


def _merge(x2, omb, osw, ome, gpre, gpost, wg, wb, wo):
    t = x2.shape[0]
    row = lambda n: pl.BlockSpec((ROW_TILE, n), lambda i: (i, 0))
    return pl.pallas_call(
        _merge_kernel,
        grid=(t // ROW_TILE,),
        in_specs=[row(D_MODEL), row(MOBA_W), row(SWA_QW), row(MEM_W),
                  _resident((1, D_MODEL)), _resident((1, D_MODEL)),
                  _resident((D_MODEL, 3 * D_MODEL)), _resident((3, MOBA_W, D_MODEL)),
                  _resident((D_MODEL, D_MODEL))],
        out_specs=row(D_MODEL),
        out_shape=jax.ShapeDtypeStruct((t, D_MODEL), F32),
        compiler_params=_cparams(1),
        name="merge",
    )(x2, omb, osw, ome, gpre, gpost, wg, wb, wo)


def _ffn_kernel(x_ref, xh_ref, gpre_ref, gpost_ref, wup_ref, cw_ref, cb_ref, wd_ref, out_ref, acc_ref,
                *, tiles_per_seq):
    i = pl.program_id(0)
    x = x_ref[...]
    xe = jnp.concatenate([xh_ref[...], x], axis=0)
    he = _rms(xe, gpre_ref[...])
    row = lax.broadcasted_iota(jnp.int32, (ROW_TILE + HALO, 1), 0)
    seq_start = (i % tiles_per_seq) == 0
    he = jnp.where((row < HALO) & seq_start, 0.0, he).astype(BF16)

    def conv(c0):
        u = _mm(he, wup_ref[:, c0:c0 + FFN_CHUNK])
        w = cw_ref[:, c0:c0 + FFN_CHUNK]
        out = cb_ref[:, c0:c0 + FFN_CHUNK] + w[0:1] * pltpu.roll(u, 2, 0)
        out = out + w[1:2] * pltpu.roll(u, 1, 0)
        out = out + w[2:3] * u
        return out[HALO:]

    for fc in range(FFN_HIDDEN // FFN_CHUNK):
        c0 = fc * FFN_CHUNK
        act = (jax.nn.gelu(conv(c0), approximate=True) * conv(FFN_HIDDEN + c0)).astype(BF16)
        part = _mm(act, wd_ref[c0:c0 + FFN_CHUNK, :])
        if fc == 0:
            acc_ref[...] = part
        else:
            acc_ref[...] += part
    out_ref[...] = x + _rms(acc_ref[...], gpost_ref[...])


def _ffn(x2, gpre, gpost, wup, cw, cb, wd, seq_len):
    t = x2.shape[0]
    row = pl.BlockSpec((ROW_TILE, D_MODEL), lambda i: (i, 0))
    halo = pl.BlockSpec((HALO, D_MODEL), lambda i: (jnp.maximum(i * (ROW_TILE // HALO) - 1, 0), 0))
    return pl.pallas_call(
        functools.partial(_ffn_kernel, tiles_per_seq=seq_len // ROW_TILE),
        grid=(t // ROW_TILE,),
        in_specs=[row, halo, _resident((1, D_MODEL)), _resident((1, D_MODEL)),
                  _resident((D_MODEL, 2 * FFN_HIDDEN)), _resident((CONV_WIDTH, 2 * FFN_HIDDEN)),
                  _resident((1, 2 * FFN_HIDDEN)), _resident((FFN_HIDDEN, D_MODEL))],
        out_specs=row,
        out_shape=jax.ShapeDtypeStruct((t, D_MODEL), F32),
        scratch_shapes=[pltpu.VMEM((ROW_TILE, D_MODEL), F32)],
        compiler_params=_cparams(1),
        name="ffn",
    )(x2, x2, gpre, gpost, wup, cw, cb, wd)


def _t5_bucket(dist):
    n = jnp.maximum(dist, 0)
    nf = jnp.maximum(n, 1).astype(F32)
    large = MAX_EXACT + (jnp.log(nf / MAX_EXACT) / math.log(REL_MAX_DISTANCE / MAX_EXACT)
                         * (NUM_BUCKETS - MAX_EXACT)).astype(jnp.int32)
    large = jnp.minimum(large, NUM_BUCKETS - 1)
    return jnp.where(n < MAX_EXACT, n, large)


def _bias_tables(rel_bias):
    bm = rel_bias[:, :MOBA_HEADS].T.astype(F32)
    bs = rel_bias[:, MOBA_HEADS:].T.astype(F32)
    i = jnp.arange(MOBA_BLOCK)[:, None]
    j = jnp.arange(MOBA_BLOCK)[None, :]
    far = bm[:, NUM_BUCKETS - 1][:, None, None]
    own = jnp.where((i - j) >= 0, bm[:, _t5_bucket(i - j)] - far, NEG)
    prev = bm[:, _t5_bucket(MOBA_BLOCK + i - j)] - far
    tab_moba = jnp.stack([prev, own, jnp.full_like(own, NEG)], axis=1)
    qi = jnp.arange(SWA_BLOCK)[:, None]
    kj = jnp.arange(2 * SWA_BLOCK)[None, :]
    dist = SWA_BLOCK + qi - kj
    tab_swa = jnp.where((dist >= 0) & (dist < SWA_WINDOW), bs[:, _t5_bucket(dist)], NEG)
    return tab_moba, tab_swa


def _pad_heads(w, heads, dup=False):
    w3 = w.reshape(w.shape[0], heads, -1)
    other = w3 if dup else jnp.zeros_like(w3)
    return jnp.concatenate([w3, other], axis=2).reshape(w.shape[0], heads * LANES)


def _proj_weight(w_in):
    o = 0
    parts = {}
    for name, width in (("qm", MOBA_W), ("km", MOBA_W), ("vm", MOBA_W), ("qs", SWA_QW), ("ks", SWA_KVW),
                        ("vs", SWA_KVW), ("qe", MEM_W)):
        parts[name] = w_in[:, o:o + width]
        o += width
    q_scale = 1.0 / math.sqrt(MOBA_HEAD_DIM)
    w = jnp.concatenate([
        _pad_heads(parts["qm"] * q_scale, MOBA_HEADS), _pad_heads(parts["km"], MOBA_HEADS), parts["vm"],
        _pad_heads(parts["qs"] * q_scale, SWA_Q_HEADS), _pad_heads(parts["ks"], SWA_KV_HEADS),
        _pad_heads(parts["vs"], SWA_KV_HEADS, dup=True), parts["qe"]], axis=1)
    return w.astype(BF16), w_in[:, o:].astype(BF16)


def _gate_matrix(kmean, batch, blocks):
    km = kmean.reshape(batch, blocks, MOBA_HEADS, LANES)[..., :MOBA_HEAD_DIM]
    km_t = km.transpose(0, 2, 3, 1)
    full = jnp.zeros((batch, MOBA_HEADS, LANES, LANES), F32)
    full = full.at[:, :, :MOBA_HEAD_DIM, MOBA_HEAD_DIM:MOBA_HEAD_DIM + blocks].set(km_t)
    hi = full.astype(BF16)
    lo = (full - hi.astype(F32)).astype(BF16)
    return hi, lo


def kernel(x, mem, norm_mix_pre, norm_mix_post, norm_ffn_pre, norm_ffn_post, norm_mem, w_in, rel_bias, swa_sinks,
           w_mem_kv, w_branch_moba, w_branch_swa, w_branch_mem, w_out, w_ffn_up, ffn_conv_w, ffn_conv_b,
           w_ffn_down):
    b, s, d = x.shape
    m_len = mem.shape[1]
    assert d == D_MODEL and s % ROW_TILE == 0 and s // MOBA_BLOCK <= 32 and m_len % 8 == 0
    depth = w_in.shape[0]
    tab_moba, tab_swa = _bias_tables(rel_bias)
    x2 = x.reshape(b * s, d).astype(F32)
    mem2 = mem.reshape(b * m_len, d).astype(F32)
    vec = lambda v: v.reshape(1, -1).astype(F32)
    for l in range(depth):
        w_qkv, w_gate = _proj_weight(w_in[l])
        qm, ka, vm, qs, ks, vs, qe, kmean = _proj(x2, vec(norm_mix_pre[l]), w_qkv, s)
        kv = _memkv(mem2, vec(norm_mem[l]), w_mem_kv[l].astype(BF16)).reshape(b, m_len, 2 * MEM_W)
        km_hi, km_lo = _gate_matrix(kmean, b, s // MOBA_BLOCK)
        r3 = lambda a: a.reshape(b, s, a.shape[-1])
        o_mb = _moba(r3(qm), r3(ka), r3(vm), km_hi, km_lo, tab_moba)
        sinks = jnp.broadcast_to(swa_sinks[l].astype(F32)[:, None], (SWA_Q_HEADS, LANES))
        o_sw, o_me = _local(r3(qs), r3(ks), r3(vs), tab_swa, sinks, r3(qe), kv)
        w_branch = jnp.stack([w_branch_moba[l], w_branch_swa[l], w_branch_mem[l]]).astype(BF16)
        x2 = _merge(x2, o_mb.reshape(b * s, -1), o_sw.reshape(b * s, -1), o_me.reshape(b * s, -1),
                    vec(norm_mix_pre[l]), vec(norm_mix_post[l]), w_gate, w_branch, w_out[l].astype(BF16))
        x2 = _ffn(x2, vec(norm_ffn_pre[l]), vec(norm_ffn_post[l]), w_ffn_up[l].astype(BF16),
                  ffn_conv_w[l].astype(F32), vec(ffn_conv_b[l]), w_ffn_down[l].astype(BF16), s)
    return x2.reshape(b, s, d).astype(x.dtype)
```

```python
import functools
import math

import jax
import jax.numpy as jnp
from jax import lax
from jax.experimental import pallas as pl
from jax.experimental.pallas import tpu as pltpu

D_MODEL = 1024
MOBA_HEADS = 8
MOBA_HEAD_DIM = 64
MOBA_BLOCK = 256
MOBA_TOPK = 3
SWA_Q_HEADS = 8
SWA_KV_HEADS = 2
SWA_HEAD_DIM = 64
SWA_WINDOW = 128
SWA_BLOCK = 128
MEM_HEADS = 4
MEM_HEAD_DIM = 128
NUM_BUCKETS = 32
MAX_EXACT = NUM_BUCKETS // 2
REL_MAX_DISTANCE = 128
FFN_HIDDEN = 2816
CONV_WIDTH = 3
RMS_EPS = 1e-6

MOBA_W = MOBA_HEADS * MOBA_HEAD_DIM
SWA_QW = SWA_Q_HEADS * SWA_HEAD_DIM
SWA_KVW = SWA_KV_HEADS * SWA_HEAD_DIM
MEM_W = MEM_HEADS * MEM_HEAD_DIM

LANES = 128
NEG = -1e30
VMEM_LIMIT = 56 * 1024 * 1024
ROW_TILE = 512
MOBA_KEYS_PER_STEP = 512
FFN_CHUNK = 256
HALO = 8

BF16 = jnp.bfloat16
F32 = jnp.float32

OFF_QM = 0
OFF_KM = OFF_QM + MOBA_W
OFF_QS = OFF_KM + MOBA_W
OFF_KS = OFF_QS + SWA_Q_HEADS * LANES
OFF_VS = OFF_KS + SWA_KV_HEADS * LANES
OFF_QE = OFF_VS + SWA_KV_HEADS * LANES
PROJ_W = OFF_QE + MEM_W


def _mm(a, b):
    return jnp.dot(a, b, preferred_element_type=F32)


def _nt(a, b):
    return lax.dot_general(a, b, (((1,), (1,)), ((), ())), preferred_element_type=F32)


def _rms(xf, g):
    r = lax.rsqrt(jnp.mean(xf * xf, axis=-1, keepdims=True) + RMS_EPS)
    return xf * r * g


def _cparams(n_axes):
    return pltpu.CompilerParams(dimension_semantics=("arbitrary",) * n_axes, vmem_limit_bytes=VMEM_LIMIT)


def _resident(shape):
    nd = len(shape)
    return pl.BlockSpec(shape, lambda *_: (0,) * nd, pipeline_mode=pl.Buffered(1))


def _proj_kernel(x_ref, g_ref, w_ref, wvt_ref, qm_ref, km_ref, vt_ref, qs_ref, ks_ref, vs_ref, qe_ref, kmean_ref):
    h = _rms(x_ref[...], g_ref[...]).astype(BF16)

    def seg(c0, n):
        return _mm(h, w_ref[:, c0:c0 + n])

    qm_ref[...] = seg(OFF_QM, MOBA_W).astype(BF16)
    k = seg(OFF_KM, MOBA_W)
    km_ref[...] = k.astype(BF16)
    vt = _nt(wvt_ref[...], h).astype(BF16)
    for r in range(ROW_TILE // MOBA_BLOCK):
        rows = slice(r * MOBA_BLOCK, (r + 1) * MOBA_BLOCK)
        kmean_ref[0, r:r + 1, :] = jnp.sum(k[rows], axis=0, keepdims=True) * (1.0 / MOBA_BLOCK)
        vt_ref[r] = vt[:, rows]
    half = 4 * LANES
    for j in range(2):
        cs = slice(j * half, (j + 1) * half)
        qs_ref[:, cs] = seg(OFF_QS + j * half, half).astype(BF16)
    ks_ref[...] = seg(OFF_KS, SWA_KV_HEADS * LANES).astype(BF16)
    vs_ref[...] = seg(OFF_VS, SWA_KV_HEADS * LANES).astype(BF16)
    qe_ref[...] = seg(OFF_QE, MEM_W).astype(BF16)


def _proj(x2, g, w, wvt):
    t = x2.shape[0]
    n_tiles = t // ROW_TILE
    bpt = ROW_TILE // MOBA_BLOCK
    row = lambda n: pl.BlockSpec((ROW_TILE, n), lambda i: (i, 0))
    out_shapes = (
        jax.ShapeDtypeStruct((t, MOBA_W), BF16),
        jax.ShapeDtypeStruct((t, MOBA_W), BF16),
        jax.ShapeDtypeStruct((t // MOBA_BLOCK, MOBA_W, MOBA_BLOCK), BF16),
        jax.ShapeDtypeStruct((t, SWA_Q_HEADS * LANES), BF16),
        jax.ShapeDtypeStruct((t, SWA_KV_HEADS * LANES), BF16),
        jax.ShapeDtypeStruct((t, SWA_KV_HEADS * LANES), BF16),
        jax.ShapeDtypeStruct((t, MEM_W), BF16),
        jax.ShapeDtypeStruct((n_tiles, bpt, MOBA_W), F32),
    )
    out_specs = (row(MOBA_W), row(MOBA_W), pl.BlockSpec((bpt, MOBA_W, MOBA_BLOCK), lambda i: (i, 0, 0)),
                 row(SWA_Q_HEADS * LANES), row(SWA_KV_HEADS * LANES), row(SWA_KV_HEADS * LANES), row(MEM_W),
                 pl.BlockSpec((1, bpt, MOBA_W), lambda i: (i, 0, 0)))
    return pl.pallas_call(
        _proj_kernel,
        grid=(n_tiles,),
        in_specs=[row(D_MODEL), _resident((1, D_MODEL)), _resident((D_MODEL, PROJ_W)),
                  _resident((MOBA_W, D_MODEL))],
        out_specs=out_specs,
        out_shape=out_shapes,
        compiler_params=_cparams(1),
        name="proj",
    )(x2, g, w, wvt)


def _memkv_kernel(m_ref, g_ref, w_ref, o_ref):
    h = _rms(m_ref[...], g_ref[...]).astype(BF16)
    o_ref[...] = _mm(h, w_ref[...]).astype(BF16)


def _memkv(mem2, g, w):
    rows = mem2.shape[0]
    tile = min(rows, ROW_TILE)
    return pl.pallas_call(
        _memkv_kernel,
        grid=(rows // tile,),
        in_specs=[pl.BlockSpec((tile, D_MODEL), lambda i: (i, 0)), _resident((1, D_MODEL)),
                  _resident((D_MODEL, 2 * MEM_W))],
        out_specs=pl.BlockSpec((tile, 2 * MEM_W), lambda i: (i, 0)),
        out_shape=jax.ShapeDtypeStruct((rows, 2 * MEM_W), BF16),
        compiler_params=_cparams(1),
        name="memkv",
    )(mem2, g, w)


def _moba_kernel(q_ref, k_ref, vt_ref, kmean_ref, tab_ref, o_ref, mfar_ref, mtail_ref):
    c = pl.program_id(2)
    mb = MOBA_BLOCK
    kb = MOBA_KEYS_PER_STEP
    per = kb // mb
    nblk = kmean_ref.shape[1]
    hd = MOBA_HEAD_DIM
    lane_q = lax.broadcasted_iota(jnp.int32, (mb, LANES), 1)
    lane_k = lax.broadcasted_iota(jnp.int32, (nblk, LANES), 1)
    blk = lax.broadcasted_iota(jnp.int32, (nblk, mb), 0)
    blk_f = blk.astype(F32)
    past = blk < c
    tb = jnp.maximum(c - 1, 0)
    kt = pl.multiple_of(tb * mb, mb)
    left = jnp.where(c == 0, 1, 0)
    right = jnp.where(c == 0, 2, 1)

    qf = q_ref[0].astype(F32)
    km = kmean_ref[0]
    k_tail = k_ref[0, pl.ds(kt, 2 * mb), :]
    q_pad = []
    state = []
    for a in range(2):
        qa = jnp.where((lane_q >= a * hd) & (lane_q < (a + 1) * hd), qf, 0.0).astype(BF16)
        q_pad.append(qa)
        km_a = jnp.where((lane_k >= a * hd) & (lane_k < (a + 1) * hd), km, 0.0)
        km_hi = km_a.astype(BF16)
        km_lo = (km_a - km_hi.astype(F32)).astype(BF16)
        gate = _nt(km_hi, qa) + _nt(km_lo, qa)
        g = jnp.where(past, gate, -jnp.inf)
        sel = blk < 0
        for _ in range(MOBA_TOPK):
            top = jnp.max(g, axis=0, keepdims=True)
            first = jnp.min(jnp.where(g == top, blk_f, 1e9), axis=0, keepdims=True)
            hit = blk_f == first
            sel = sel | hit
            g = jnp.where(hit, -jnp.inf, g)
        sel = sel & past
        mfar_ref[a] = jnp.where(sel & (blk <= c - 2), 0.0, NEG)
        mtail_ref[a] = jnp.where((blk == c) | (sel & (blk == c - 1)), 0.0, NEG)

        s = _nt(k_tail, qa)
        s = jnp.concatenate([s[:mb] + tab_ref[a, left] + mtail_ref[a, pl.ds(tb, 1), :],
                             s[mb:] + tab_ref[a, right] + mtail_ref[a, pl.ds(tb + 1, 1), :]], axis=0)
        m0 = jnp.max(s, axis=0, keepdims=True)
        p = jnp.exp(s - m0)
        l0 = jnp.sum(p, axis=0, keepdims=True)
        pb = p.astype(BF16)
        acc0 = _mm(vt_ref[0, tb], pb[:mb]) + _mm(vt_ref[0, tb + 1], pb[mb:])
        state.append((m0, l0, acc0))

    n_far = jnp.where(c >= 2, ((c - 1) * mb + kb - 1) // kb, 0)
    last_step = k_ref.shape[1] // kb - 1

    def far_scores(j):
        kj = k_ref[0, pl.ds(pl.multiple_of(j * kb, kb), kb), :]
        out = []
        for a in range(2):
            s = _nt(kj, q_pad[a])
            s = jnp.concatenate([s[t * mb:(t + 1) * mb] + mfar_ref[a, pl.ds(j * per + t, 1), :]
                                 for t in range(per)], axis=0)
            out.append((s, jnp.max(s, axis=0, keepdims=True)))
        return tuple(out)

    def body(j, carry):
        scores, stats = carry
        nxt = far_scores(jnp.minimum(j + 1, last_step))
        out = []
        for a in range(2):
            m, l, acc = stats[a]
            s, s_max = scores[a]
            m_new = jnp.maximum(m, s_max)
            alpha = jnp.exp(m - m_new)
            p = jnp.exp(s - m_new)
            l = alpha * l + jnp.sum(p, axis=0, keepdims=True)
            pb = p.astype(BF16)
            acc = alpha * acc
            for t in range(per):
                acc = acc + _mm(vt_ref[0, j * per + t], pb[t * mb:(t + 1) * mb])
            out.append((m_new, l, acc))
        return nxt, tuple(out)

    _, ((_, l_a, acc_a), (_, l_b, acc_b)) = lax.fori_loop(0, n_far, body, (far_scores(0), tuple(state)))
    row = lax.broadcasted_iota(jnp.int32, (LANES, mb), 0)
    o_t = jnp.where(row < hd, acc_a / l_a, acc_b / l_b)
    o_ref[0] = o_t.T.astype(BF16)


def _moba(qm, km, vt, kmean, tab):
    b, s, _ = qm.shape
    pairs = MOBA_HEADS // 2
    nblk = s // MOBA_BLOCK
    grid = (b, pairs, nblk)
    return pl.pallas_call(
        _moba_kernel,
        grid=grid,
        in_specs=[
            pl.BlockSpec((1, MOBA_BLOCK, LANES), lambda bi, hp, c: (bi, c, hp)),
            pl.BlockSpec((1, s, LANES), lambda bi, hp, c: (bi, 0, hp)),
            pl.BlockSpec((1, nblk, LANES, MOBA_BLOCK), lambda bi, hp, c: (bi, 0, hp, 0)),
            pl.BlockSpec((1, nblk, LANES), lambda bi, hp, c: (bi, 0, hp)),
            pl.BlockSpec((2, 3, MOBA_BLOCK, MOBA_BLOCK), lambda bi, hp, c: (hp, 0, 0, 0)),
        ],
        out_specs=pl.BlockSpec((1, MOBA_BLOCK, LANES), lambda bi, hp, c: (bi, c, hp)),
        out_shape=jax.ShapeDtypeStruct((b, s, MOBA_W), BF16),
        scratch_shapes=[pltpu.VMEM((2, nblk, MOBA_BLOCK), F32), pltpu.VMEM((2, nblk, MOBA_BLOCK), F32)],
        compiler_params=_cparams(3),
        name="moba",
    )(qm, km, vt, kmean, tab)


def _local_kernel(qs_ref, ks_ref, kh_ref, vs_ref, vh_ref, tab_ref, sink_ref, qe_ref, kv_ref, osw_ref, ome_ref):
    i = pl.program_id(1)
    qb = SWA_BLOCK
    group = SWA_Q_HEADS // SWA_KV_HEADS
    no_prev = jnp.where(i == 0, NEG, 0.0)
    lane = lax.broadcasted_iota(jnp.int32, (qb, LANES), 1)
    for blk in range(ROW_TILE // qb):
        r0 = blk * qb
        rows = slice(r0, r0 + qb)
        for hk in range(SWA_KV_HEADS):
            cols = slice(hk * LANES, (hk + 1) * LANES)
            if blk == 0:
                kk = jnp.concatenate([kh_ref[0, :, cols], ks_ref[0, 0:qb, cols]], axis=0)
                vv = jnp.concatenate([vh_ref[0, :, cols], vs_ref[0, 0:qb, cols]], axis=0)
            else:
                kk = ks_ref[0, r0 - qb:r0 + qb, cols]
                vv = vs_ref[0, r0 - qb:r0 + qb, cols]
            outs = []
            for gi in range(group):
                h = hk * group + gi
                q = qs_ref[0, rows, h * LANES:(h + 1) * LANES]
                s = _nt(q, kk) + tab_ref[h]
                if blk == 0:
                    s = jnp.concatenate([s[:, :qb] + no_prev, s[:, qb:]], axis=1)
                sink = sink_ref[h:h + 1, 0:1]
                m = jnp.maximum(jnp.max(s, axis=-1, keepdims=True), sink)
                p = jnp.exp(s - m)
                den = jnp.sum(p, axis=-1, keepdims=True) + jnp.exp(sink - m)
                outs.append(_mm(p.astype(BF16), vv) / den)
            for jj in range(group // 2):
                j = hk * (group // 2) + jj
                osw_ref[0, rows, j * LANES:(j + 1) * LANES] = jnp.where(
                    lane < SWA_HEAD_DIM, outs[2 * jj], outs[2 * jj + 1]).astype(BF16)

    scale = 1.0 / math.sqrt(MEM_HEAD_DIM)
    for hd in range(MEM_HEADS):
        cols = slice(hd * MEM_HEAD_DIM, (hd + 1) * MEM_HEAD_DIM)
        q = qe_ref[0, :, cols]
        kk = kv_ref[0, :, cols]
        vv = kv_ref[0, :, MEM_W + hd * MEM_HEAD_DIM:MEM_W + (hd + 1) * MEM_HEAD_DIM]
        s = _nt(q, kk) * scale
        m = jnp.max(s, axis=-1, keepdims=True)
        p = jnp.exp(s - m)
        den = jnp.sum(p, axis=-1, keepdims=True)
        ome_ref[0, :, cols] = (_mm(p.astype(BF16), vv) / den).astype(BF16)


def _local(qs, ks, vs, tab, sinks, qe, kv):
    b, s, _ = qs.shape
    per_tile = ROW_TILE // SWA_BLOCK
    tile = lambda n: pl.BlockSpec((1, ROW_TILE, n), lambda bi, i: (bi, i, 0))
    halo = pl.BlockSpec((1, SWA_BLOCK, SWA_KV_HEADS * LANES),
                        lambda bi, i: (bi, jnp.maximum(i * per_tile - 1, 0), 0))
    m_len = kv.shape[1]
    return pl.pallas_call(
        _local_kernel,
        grid=(b, s // ROW_TILE),
        in_specs=[
            tile(SWA_Q_HEADS * LANES),
            tile(SWA_KV_HEADS * LANES), halo,
            tile(SWA_KV_HEADS * LANES), halo,
            _resident((SWA_Q_HEADS, SWA_BLOCK, 2 * SWA_BLOCK)),
            _resident((SWA_Q_HEADS, LANES)),
            tile(MEM_W),
            pl.BlockSpec((1, m_len, 2 * MEM_W), lambda bi, i: (bi, 0, 0)),
        ],
        out_specs=(tile(SWA_QW), tile(MEM_W)),
        out_shape=(jax.ShapeDtypeStruct((b, s, SWA_QW), BF16), jax.ShapeDtypeStruct((b, s, MEM_W), BF16)),
        compiler_params=_cparams(2),
        name="local",
    )(qs, ks, ks, vs, vs, tab, sinks, qe, kv)


def _merge_kernel(x_ref, omb_ref, osw_ref, ome_ref, gpre_ref, gpost_ref, wg_ref, wb_ref, wo_ref, out_ref):
    x = x_ref[...]
    h = _rms(x, gpre_ref[...]).astype(BF16)
    branch = (omb_ref[...], osw_ref[...], ome_ref[...])
    half = D_MODEL // 2
    parts = []
    for nc in range(2):
        acc = None
        for br in range(3):
            c0 = br * D_MODEL + nc * half
            gate = jax.nn.sigmoid(_mm(h, wg_ref[:, c0:c0 + half]))
            term = gate * _mm(branch[br], wb_ref[br, :, nc * half:(nc + 1) * half])
            acc = term if acc is None else acc + term
        parts.append(acc.astype(BF16))
    y = _mm(jnp.concatenate(parts, axis=1), wo_ref[...])
    out_ref[...] = x + _rms(y, gpost_ref[...])


def _merge(x2, omb, osw, ome, gpre, gpost, wg, wb, wo):
    t = x2.shape[0]
    row = lambda n: pl.BlockSpec((ROW_TILE, n), lambda i: (i, 0))
    return pl.pallas_call(
        _merge_kernel,
        grid=(t // ROW_TILE,),
        in_specs=[row(D_MODEL), row(MOBA_W), row(SWA_QW), row(MEM_W),
                  _resident((1, D_MODEL)), _resident((1, D_MODEL)),
                  _resident((D_MODEL, 3 * D_MODEL)), _resident((3, MOBA_W, D_MODEL)),
                  _resident((D_MODEL, D_MODEL))],
        out_specs=row(D_MODEL),
        out_shape=jax.ShapeDtypeStruct((t, D_MODEL), F32),
        compiler_params=_cparams(1),
        name="merge",
    )(x2, omb, osw, ome, gpre, gpost, wg, wb, wo)


def _ffn_kernel(x_ref, xh_ref, gpre_ref, gpost_ref, wup_ref, cw_ref, cb_ref, wd_ref, out_ref, acc_ref,
                *, tiles_per_seq):
    i = pl.program_id(0)
    x = x_ref[...]
    xe = jnp.concatenate([xh_ref[...], x], axis=0)
    he = _rms(xe, gpre_ref[...])
    row = lax.broadcasted_iota(jnp.int32, (ROW_TILE + HALO, 1), 0)
    seq_start = (i % tiles_per_seq) == 0
    he = jnp.where((row < HALO) & seq_start, 0.0, he).astype(BF16)

    def conv(c0):
        u = _mm(he, wup_ref[:, c0:c0 + FFN_CHUNK])
        w = cw_ref[:, c0:c0 + FFN_CHUNK]
        out = cb_ref[:, c0:c0 + FFN_CHUNK] + w[0:1] * pltpu.roll(u, 2, 0)
        out = out + w[1:2] * pltpu.roll(u, 1, 0)
        out = out + w[2:3] * u
        return out[HALO:]

    for fc in range(FFN_HIDDEN // FFN_CHUNK):
        c0 = fc * FFN_CHUNK
        act = (jax.nn.gelu(conv(c0), approximate=True) * conv(FFN_HIDDEN + c0)).astype(BF16)
        part = _mm(act, wd_ref[c0:c0 + FFN_CHUNK, :])
        if fc == 0:
            acc_ref[...] = part
        else:
            acc_ref[...] += part
    out_ref[...] = x + _rms(acc_ref[...], gpost_ref[...])


def _ffn(x2, gpre, gpost, wup, cw, cb, wd, seq_len):
    t = x2.shape[0]
    row = pl.BlockSpec((ROW_TILE, D_MODEL), lambda i: (i, 0))
    halo = pl.BlockSpec((HALO, D_MODEL), lambda i: (jnp.maximum(i * (ROW_TILE // HALO) - 1, 0), 0))
    return pl.pallas_call(
        functools.partial(_ffn_kernel, tiles_per_seq=seq_len // ROW_TILE),
        grid=(t // ROW_TILE,),
        in_specs=[row, halo, _resident((1, D_MODEL)), _resident((1, D_MODEL)),
                  _resident((D_MODEL, 2 * FFN_HIDDEN)), _resident((CONV_WIDTH, 2 * FFN_HIDDEN)),
                  _resident((1, 2 * FFN_HIDDEN)), _resident((FFN_HIDDEN, D_MODEL))],
        out_specs=row,
        out_shape=jax.ShapeDtypeStruct((t, D_MODEL), F32),
        scratch_shapes=[pltpu.VMEM((ROW_TILE, D_MODEL), F32)],
        compiler_params=_cparams(1),
        name="ffn",
    )(x2, x2, gpre, gpost, wup, cw, cb, wd)


def _t5_bucket(dist):
    n = jnp.maximum(dist, 0)
    nf = jnp.maximum(n, 1).astype(F32)
    large = MAX_EXACT + (jnp.log(nf / MAX_EXACT) / math.log(REL_MAX_DISTANCE / MAX_EXACT)
                         * (NUM_BUCKETS - MAX_EXACT)).astype(jnp.int32)
    large = jnp.minimum(large, NUM_BUCKETS - 1)
    return jnp.where(n < MAX_EXACT, n, large)


def _bias_tables(rel_bias):
    bm = rel_bias[:, :MOBA_HEADS].T.astype(F32)
    bs = rel_bias[:, MOBA_HEADS:].T.astype(F32)

    def lookup(table, dist):
        bucket = _t5_bucket(dist)[None]
        out = jnp.zeros((table.shape[0],) + dist.shape, F32)
        for bkt in range(NUM_BUCKETS):
            out = jnp.where(bucket == bkt, table[:, bkt][:, None, None], out)
        return out

    kj = jnp.arange(MOBA_BLOCK)[:, None]
    qi = jnp.arange(MOBA_BLOCK)[None, :]
    far = bm[:, NUM_BUCKETS - 1][:, None, None]
    own = jnp.where((qi - kj) >= 0, lookup(bm, qi - kj) - far, NEG)
    prev = lookup(bm, MOBA_BLOCK + qi - kj) - far
    tab_moba = jnp.stack([prev, own, jnp.full_like(own, NEG)], axis=1)
    qi = jnp.arange(SWA_BLOCK)[:, None]
    kj = jnp.arange(2 * SWA_BLOCK)[None, :]
    dist = SWA_BLOCK + qi - kj
    tab_swa = jnp.where((dist >= 0) & (dist < SWA_WINDOW), lookup(bs, dist), NEG)
    return tab_moba, tab_swa


def _pad_heads(w, heads, dup=False):
    w3 = w.reshape(w.shape[0], heads, -1)
    other = w3 if dup else jnp.zeros_like(w3)
    return jnp.concatenate([w3, other], axis=2).reshape(w.shape[0], heads * LANES)


def _proj_weight(w_in):
    o = 0
    parts = {}
    for name, width in (("qm", MOBA_W), ("km", MOBA_W), ("vm", MOBA_W), ("qs", SWA_QW), ("ks", SWA_KVW),
                        ("vs", SWA_KVW), ("qe", MEM_W)):
        parts[name] = w_in[:, o:o + width]
        o += width
    q_scale = 1.0 / math.sqrt(MOBA_HEAD_DIM)
    w = jnp.concatenate([
        parts["qm"] * q_scale, parts["km"],
        _pad_heads(parts["qs"] * q_scale, SWA_Q_HEADS), _pad_heads(parts["ks"], SWA_KV_HEADS),
        _pad_heads(parts["vs"], SWA_KV_HEADS, dup=True), parts["qe"]], axis=1)
    return w.astype(BF16), parts["vm"].T.astype(BF16), w_in[:, o:].astype(BF16)


def kernel(x, mem, norm_mix_pre, norm_mix_post, norm_ffn_pre, norm_ffn_post, norm_mem, w_in, rel_bias, swa_sinks,
           w_mem_kv, w_branch_moba, w_branch_swa, w_branch_mem, w_out, w_ffn_up, ffn_conv_w, ffn_conv_b,
           w_ffn_down):
    b, s, d = x.shape
    m_len = mem.shape[1]
    assert d == D_MODEL and s % ROW_TILE == 0 and s // MOBA_BLOCK <= 32 and m_len % 8 == 0
    depth = w_in.shape[0]
    tab_moba, tab_swa = _bias_tables(rel_bias)
    x2 = x.reshape(b * s, d).astype(F32)
    mem2 = mem.reshape(b * m_len, d).astype(F32)
    vec = lambda v: v.reshape(1, -1).astype(F32)
    for l in range(depth):
        w_qkv, w_vt, w_gate = _proj_weight(w_in[l])
        qm, km, vt, qs, ks, vs, qe, kmean = _proj(x2, vec(norm_mix_pre[l]), w_qkv, w_vt)
        kv = _memkv(mem2, vec(norm_mem[l]), w_mem_kv[l].astype(BF16)).reshape(b, m_len, 2 * MEM_W)
        nblk = s // MOBA_BLOCK
        r3 = lambda a: a.reshape(b, s, a.shape[-1])
        o_mb = _moba(r3(qm), r3(km), vt.reshape(b, nblk, MOBA_W, MOBA_BLOCK), kmean.reshape(b, nblk, MOBA_W),
                     tab_moba)
        sinks = jnp.broadcast_to(swa_sinks[l].astype(F32)[:, None], (SWA_Q_HEADS, LANES))
        o_sw, o_me = _local(r3(qs), r3(ks), r3(vs), tab_swa, sinks, r3(qe), kv)
        w_branch = jnp.stack([w_branch_moba[l], w_branch_swa[l], w_branch_mem[l]]).astype(BF16)
        x2 = _merge(x2, o_mb.reshape(b * s, -1), o_sw.reshape(b * s, -1), o_me.reshape(b * s, -1),
                    vec(norm_mix_pre[l]), vec(norm_mix_post[l]), w_gate, w_branch, w_out[l].astype(BF16))
        x2 = _ffn(x2, vec(norm_ffn_pre[l]), vec(norm_ffn_post[l]), w_ffn_up[l].astype(BF16),
                  ffn_conv_w[l].astype(F32), vec(ffn_conv_b[l]), w_ffn_down[l].astype(BF16), s)
    return x2.reshape(b, s, d).astype(x.dtype)
```

```python
import functools
import math

import jax
import jax.numpy as jnp
from jax import lax
from jax.experimental import pallas as pl
from jax.experimental.pallas import tpu as pltpu

D_MODEL = 1024
MOBA_HEADS = 8
MOBA_HEAD_DIM = 64
MOBA_BLOCK = 256
MOBA_TOPK = 3
SWA_Q_HEADS = 8
SWA_KV_HEADS = 2
SWA_HEAD_DIM = 64
SWA_WINDOW = 128
SWA_BLOCK = 128
MEM_HEADS = 4
MEM_HEAD_DIM = 128
NUM_BUCKETS = 32
MAX_EXACT = NUM_BUCKETS // 2
REL_MAX_DISTANCE = 128
FFN_HIDDEN = 2816
CONV_WIDTH = 3
RMS_EPS = 1e-6

MOBA_W = MOBA_HEADS * MOBA_HEAD_DIM
SWA_QW = SWA_Q_HEADS * SWA_HEAD_DIM
SWA_KVW = SWA_KV_HEADS * SWA_HEAD_DIM
MEM_W = MEM_HEADS * MEM_HEAD_DIM

LANES = 128
NEG = -1e30
VMEM_LIMIT = 56 * 1024 * 1024
ROW_TILE = 512
MOBA_KEYS_PER_STEP = 512
FFN_CHUNK = 256
HALO = 8
SUM_ROWS = 16

BF16 = jnp.bfloat16
F32 = jnp.float32

SWA_K_COLS = 2 * SWA_KV_HEADS * LANES
SWA_VT_ROWS = SWA_KV_HEADS * LANES
OFF_QM = 0
OFF_KM = OFF_QM + MOBA_W
OFF_QS = OFF_KM + MOBA_W
OFF_KS = OFF_QS + SWA_QW
OFF_QE = OFF_KS + SWA_K_COLS
PROJ_W = OFF_QE + MEM_W
LOG2E = math.log2(math.e)


def _mm(a, b):
    return jnp.dot(a, b, preferred_element_type=F32)


def _nt(a, b):
    return lax.dot_general(a, b, (((1,), (1,)), ((), ())), preferred_element_type=F32)


def _rms(xf, g):
    r = lax.rsqrt(jnp.mean(xf * xf, axis=-1, keepdims=True) + RMS_EPS)
    return xf * r * g


def _cparams(n_axes):
    return pltpu.CompilerParams(dimension_semantics=("arbitrary",) * n_axes, vmem_limit_bytes=VMEM_LIMIT)


def _resident(shape):
    nd = len(shape)
    return pl.BlockSpec(shape, lambda *_: (0,) * nd, pipeline_mode=pl.Buffered(1))


def _proj_kernel(x_ref, g_ref, w_ref, wt_ref, qm_ref, km_ref, vt_ref, qs_ref, ks_ref, vst_ref, qe_ref, kmean_ref):
    h = _rms(x_ref[...], g_ref[...]).astype(BF16)

    def seg(c0, n):
        return _mm(h, w_ref[:, c0:c0 + n])

    qm_ref[...] = seg(OFF_QM, MOBA_W).astype(BF16)
    k = seg(OFF_KM, MOBA_W)
    km_ref[...] = k.astype(BF16)
    qs_ref[...] = seg(OFF_QS, SWA_QW).astype(BF16)
    ks_ref[...] = seg(OFF_KS, SWA_K_COLS).astype(BF16)
    qe_ref[...] = seg(OFF_QE, MEM_W).astype(BF16)
    vt = _nt(wt_ref[...], h).astype(BF16)
    for r in range(ROW_TILE // MOBA_BLOCK):
        rows = slice(r * MOBA_BLOCK, (r + 1) * MOBA_BLOCK)
        kmean_ref[0, r:r + 1, :] = jnp.sum(k[rows], axis=0, keepdims=True) * (1.0 / MOBA_BLOCK)
        vt_ref[r] = vt[:MOBA_W, rows]
    for r in range(ROW_TILE // SWA_BLOCK):
        vst_ref[r] = vt[MOBA_W:, r * SWA_BLOCK:(r + 1) * SWA_BLOCK]


def _proj(x2, g, w, wt):
    t = x2.shape[0]
    n_tiles = t // ROW_TILE
    bpt = ROW_TILE // MOBA_BLOCK
    spt = ROW_TILE // SWA_BLOCK
    row = lambda n: pl.BlockSpec((ROW_TILE, n), lambda i: (i, 0))
    out_shapes = (
        jax.ShapeDtypeStruct((t, MOBA_W), BF16),
        jax.ShapeDtypeStruct((t, MOBA_W), BF16),
        jax.ShapeDtypeStruct((t // MOBA_BLOCK, MOBA_W, MOBA_BLOCK), BF16),
        jax.ShapeDtypeStruct((t, SWA_QW), BF16),
        jax.ShapeDtypeStruct((t, SWA_K_COLS), BF16),
        jax.ShapeDtypeStruct((t // SWA_BLOCK, SWA_VT_ROWS, SWA_BLOCK), BF16),
        jax.ShapeDtypeStruct((t, MEM_W), BF16),
        jax.ShapeDtypeStruct((n_tiles, bpt, MOBA_W), F32),
    )
    out_specs = (row(MOBA_W), row(MOBA_W), pl.BlockSpec((bpt, MOBA_W, MOBA_BLOCK), lambda i: (i, 0, 0)),
                 row(SWA_QW), row(SWA_K_COLS), pl.BlockSpec((spt, SWA_VT_ROWS, SWA_BLOCK), lambda i: (i, 0, 0)),
                 row(MEM_W), pl.BlockSpec((1, bpt, MOBA_W), lambda i: (i, 0, 0)))
    return pl.pallas_call(
        _proj_kernel,
        grid=(n_tiles,),
        in_specs=[row(D_MODEL), _resident((1, D_MODEL)), _resident((D_MODEL, PROJ_W)),
                  _resident((MOBA_W + SWA_VT_ROWS, D_MODEL))],
        out_specs=out_specs,
        out_shape=out_shapes,
        compiler_params=_cparams(1),
        name="proj",
    )(x2, g, w, wt)


def _memkv_kernel(m_ref, g_ref, wk_ref, wvt_ref, k_ref, vt_ref):
    h = _rms(m_ref[0], g_ref[...]).astype(BF16)
    k_ref[0] = _mm(h, wk_ref[...]).astype(BF16)
    vt_ref[0] = _nt(wvt_ref[...], h).astype(BF16)


def _memkv(mem, g, wk, wvt):
    b, m_len, _ = mem.shape
    return pl.pallas_call(
        _memkv_kernel,
        grid=(b,),
        in_specs=[pl.BlockSpec((1, m_len, D_MODEL), lambda i: (i, 0, 0)), _resident((1, D_MODEL)),
                  _resident((D_MODEL, MEM_W)), _resident((MEM_W, D_MODEL))],
        out_specs=(pl.BlockSpec((1, m_len, MEM_W), lambda i: (i, 0, 0)),
                   pl.BlockSpec((1, MEM_W, m_len), lambda i: (i, 0, 0))),
        out_shape=(jax.ShapeDtypeStruct((b, m_len, MEM_W), BF16), jax.ShapeDtypeStruct((b, MEM_W, m_len), BF16)),
        compiler_params=_cparams(1),
        name="memkv",
    )(mem, g, wk, wvt)


def _moba_kernel(q_ref, k_ref, vt_ref, kmean_ref, tab_ref, o_ref, mfar_ref, mtail_ref):
    c = pl.program_id(2)
    mb = MOBA_BLOCK
    kb = MOBA_KEYS_PER_STEP
    per = kb // mb
    nblk = kmean_ref.shape[1]
    hd = MOBA_HEAD_DIM
    lane_q = lax.broadcasted_iota(jnp.int32, (mb, LANES), 1)
    lane_k = lax.broadcasted_iota(jnp.int32, (nblk, LANES), 1)
    blk = lax.broadcasted_iota(jnp.int32, (nblk, mb), 0)
    blk_f = blk.astype(F32)
    past = blk < c
    tb = jnp.maximum(c - 1, 0)
    kt = pl.multiple_of(tb * mb, mb)
    left = jnp.where(c == 0, 1, 0)
    right = jnp.where(c == 0, 2, 1)

    ones = jnp.ones((SUM_ROWS, mb), BF16)

    def values(block):
        return jnp.concatenate([vt_ref[0, block], ones], axis=0)

    qf = q_ref[0].astype(F32)
    km = kmean_ref[0]
    k_tail = k_ref[0, pl.ds(kt, 2 * mb), :]
    q_pad = []
    state = []
    for a in range(2):
        qa = jnp.where((lane_q >= a * hd) & (lane_q < (a + 1) * hd), qf, 0.0).astype(BF16)
        q_pad.append(qa)
        km_a = jnp.where((lane_k >= a * hd) & (lane_k < (a + 1) * hd), km, 0.0)
        km_hi = km_a.astype(BF16)
        km_lo = (km_a - km_hi.astype(F32)).astype(BF16)
        gate = _nt(km_hi, qa) + _nt(km_lo, qa)
        g = jnp.where(past, gate, -jnp.inf)
        sel = blk < 0
        for _ in range(MOBA_TOPK):
            top = jnp.max(g, axis=0, keepdims=True)
            first = jnp.min(jnp.where(g == top, blk_f, 1e9), axis=0, keepdims=True)
            hit = blk_f == first
            sel = sel | hit
            g = jnp.where(hit, -jnp.inf, g)
        sel = sel & past
        mfar_ref[a] = jnp.where(sel & (blk <= c - 2), 0.0, NEG)
        mtail_ref[a] = jnp.where((blk == c) | (sel & (blk == c - 1)), 0.0, NEG)

        s = _nt(k_tail, qa)
        s = jnp.concatenate([s[:mb] + tab_ref[a, left] + mtail_ref[a, pl.ds(tb, 1), :],
                             s[mb:] + tab_ref[a, right] + mtail_ref[a, pl.ds(tb + 1, 1), :]], axis=0)
        m0 = jnp.max(s, axis=0, keepdims=True)
        pb = jnp.exp2(s - m0).astype(BF16)
        state.append((m0, _mm(values(tb), pb[:mb]) + _mm(values(tb + 1), pb[mb:])))

    n_far = jnp.where(c >= 2, ((c - 1) * mb + kb - 1) // kb, 0)
    last_step = k_ref.shape[1] // kb - 1

    def far_scores(j):
        kj = k_ref[0, pl.ds(pl.multiple_of(j * kb, kb), kb), :]
        out = []
        for a in range(2):
            s = _nt(kj, q_pad[a])
            s = jnp.concatenate([s[t * mb:(t + 1) * mb] + mfar_ref[a, pl.ds(j * per + t, 1), :]
                                 for t in range(per)], axis=0)
            out.append((s, jnp.max(s, axis=0, keepdims=True)))
        return tuple(out)

    def body(j, carry):
        scores, stats = carry
        nxt = far_scores(jnp.minimum(j + 1, last_step))
        out = []
        for a in range(2):
            m, acc = stats[a]
            s, s_max = scores[a]
            m_new = jnp.maximum(m, s_max)
            pb = jnp.exp2(s - m_new).astype(BF16)
            acc = jnp.exp2(m - m_new) * acc
            for t in range(per):
                acc = acc + _mm(values(j * per + t), pb[t * mb:(t + 1) * mb])
            out.append((m_new, acc))
        return nxt, tuple(out)

    _, ((_, acc_a), (_, acc_b)) = lax.fori_loop(0, n_far, body, (far_scores(0), tuple(state)))
    row = lax.broadcasted_iota(jnp.int32, (LANES, mb), 0)
    o_t = jnp.where(row < hd, acc_a[:LANES] / acc_a[LANES:LANES + 1], acc_b[:LANES] / acc_b[LANES:LANES + 1])
    o_ref[0] = o_t.T.astype(BF16)


def _moba(qm, km, vt, kmean, tab):
    b, s, _ = qm.shape
    pairs = MOBA_HEADS // 2
    nblk = s // MOBA_BLOCK
    grid = (b, pairs, nblk)
    return pl.pallas_call(
        _moba_kernel,
        grid=grid,
        in_specs=[
            pl.BlockSpec((1, MOBA_BLOCK, LANES), lambda bi, hp, c: (bi, c, hp)),
            pl.BlockSpec((1, s, LANES), lambda bi, hp, c: (bi, 0, hp)),
            pl.BlockSpec((1, nblk, LANES, MOBA_BLOCK), lambda bi, hp, c: (bi, 0, hp, 0)),
            pl.BlockSpec((1, nblk, LANES), lambda bi, hp, c: (bi, 0, hp)),
            pl.BlockSpec((2, 3, MOBA_BLOCK, MOBA_BLOCK), lambda bi, hp, c: (hp, 0, 0, 0)),
        ],
        out_specs=pl.BlockSpec((1, MOBA_BLOCK, LANES), lambda bi, hp, c: (bi, c, hp)),
        out_shape=jax.ShapeDtypeStruct((b, s, MOBA_W), BF16),
        scratch_shapes=[pltpu.VMEM((2, nblk, MOBA_BLOCK), F32), pltpu.VMEM((2, nblk, MOBA_BLOCK), F32)],
        compiler_params=_cparams(3),
        name="moba",
    )(qm, km, vt, kmean, tab)


def _local_kernel(qs_ref, ks_ref, kh_ref, vst_ref, vsh_ref, tab_ref, sink_ref, qe_ref, kme_ref, vmt_ref,
                  osw_ref, ome_ref):
    i = pl.program_id(1)
    qb = SWA_BLOCK
    pairs_per_kv = SWA_Q_HEADS // SWA_KV_HEADS // 2
    no_prev = jnp.where(i == 0, NEG, 0.0)
    row = lax.broadcasted_iota(jnp.int32, (LANES, qb), 0)
    for blk in range(ROW_TILE // qb):
        rows = slice(blk * qb, (blk + 1) * qb)
        for hk in range(SWA_KV_HEADS):
            vrows = slice(hk * LANES, (hk + 1) * LANES)
            v_prev = vsh_ref[0, 0, vrows] if blk == 0 else vst_ref[0, blk - 1, vrows]
            vv = jnp.concatenate([v_prev, vst_ref[0, blk, vrows]], axis=1)
            for jj in range(pairs_per_kv):
                j = hk * pairs_per_kv + jj
                qp = qs_ref[0, rows, j * LANES:(j + 1) * LANES]
                outs = []
                for par in range(2):
                    h = 2 * j + par
                    kc = slice((2 * hk + par) * LANES, (2 * hk + par + 1) * LANES)
                    if blk == 0:
                        kk = jnp.concatenate([kh_ref[0, :, kc], ks_ref[0, 0:qb, kc]], axis=0)
                    else:
                        kk = ks_ref[0, (blk - 1) * qb:(blk + 1) * qb, kc]
                    s = _nt(kk, qp) + tab_ref[h]
                    if blk == 0:
                        s = jnp.concatenate([s[:qb] + no_prev, s[qb:]], axis=0)
                    sink = sink_ref[h:h + 1, :]
                    m = jnp.maximum(jnp.max(s, axis=0, keepdims=True), sink)
                    p = jnp.exp2(s - m)
                    den = jnp.sum(p, axis=0, keepdims=True) + jnp.exp2(sink - m)
                    outs.append(_mm(vv, p.astype(BF16)) / den)
                o_t = jnp.where(row < SWA_HEAD_DIM, outs[0], outs[1])
                osw_ref[0, rows, j * LANES:(j + 1) * LANES] = o_t.T.astype(BF16)

    scale = LOG2E / math.sqrt(MEM_HEAD_DIM)
    for hd in range(MEM_HEADS):
        cols = slice(hd * MEM_HEAD_DIM, (hd + 1) * MEM_HEAD_DIM)
        kk = kme_ref[0, :, cols]
        vv = vmt_ref[0, cols, :]
        for blk in range(ROW_TILE // qb):
            rows = slice(blk * qb, (blk + 1) * qb)
            s = _nt(kk, qe_ref[0, rows, cols]) * scale
            m = jnp.max(s, axis=0, keepdims=True)
            p = jnp.exp2(s - m)
            den = jnp.sum(p, axis=0, keepdims=True)
            ome_ref[0, rows, cols] = (_mm(vv, p.astype(BF16)) / den).T.astype(BF16)


def _local(qs, ks, vst, tab, sinks, qe, kme, vmt):
    b, s, _ = qs.shape
    per_tile = ROW_TILE // SWA_BLOCK
    m_len = kme.shape[1]
    tile = lambda n: pl.BlockSpec((1, ROW_TILE, n), lambda bi, i: (bi, i, 0))
    prev_block = lambda bi, i: (bi, jnp.maximum(i * per_tile - 1, 0), 0)
    return pl.pallas_call(
        _local_kernel,
        grid=(b, s // ROW_TILE),
        in_specs=[
            tile(SWA_QW),
            tile(SWA_K_COLS), pl.BlockSpec((1, SWA_BLOCK, SWA_K_COLS), prev_block),
            pl.BlockSpec((1, per_tile, SWA_VT_ROWS, SWA_BLOCK), lambda bi, i: (bi, i, 0, 0)),
            pl.BlockSpec((1, 1, SWA_VT_ROWS, SWA_BLOCK), lambda bi, i: prev_block(bi, i) + (0,)),
            _resident((SWA_Q_HEADS, 2 * SWA_BLOCK, SWA_BLOCK)),
            _resident((SWA_Q_HEADS, LANES)),
            tile(MEM_W),
            pl.BlockSpec((1, m_len, MEM_W), lambda bi, i: (bi, 0, 0)),
            pl.BlockSpec((1, MEM_W, m_len), lambda bi, i: (bi, 0, 0)),
        ],
        out_specs=(tile(SWA_QW), tile(MEM_W)),
        out_shape=(jax.ShapeDtypeStruct((b, s, SWA_QW), BF16), jax.ShapeDtypeStruct((b, s, MEM_W), BF16)),
        compiler_params=_cparams(2),
        name="local",
    )(qs, ks, ks, vst, vst, tab, sinks, qe, kme, vmt)


def _merge_kernel(x_ref, omb_ref, osw_ref, ome_ref, gpre_ref, gpost_ref, wg_ref, wb_ref, wo_ref, out_ref):
    x = x_ref[...]
    h = _rms(x, gpre_ref[...]).astype(BF16)
    branch = (omb_ref[...], osw_ref[...], ome_ref[...])
    half = D_MODEL // 2
    parts = []
    for nc in range(2):
        acc = None
        for br in range(3):
            c0 = br * D_MODEL + nc * half
            gate = jax.nn.sigmoid(_mm(h, wg_ref[:, c0:c0 + half]))
            term = gate * _mm(branch[br], wb_ref[br, :, nc * half:(nc + 1) * half])
            acc = term if acc is None else acc + term
        parts.append(acc.astype(BF16))
    y = _mm(jnp.concatenate(parts, axis=1), wo_ref[...])
    out_ref[...] = x + _rms(y, gpost_ref[...])


def _merge(x2, omb, osw, ome, gpre, gpost, wg, wb, wo):
    t = x2.shape[0]
    row = lambda n: pl.BlockSpec((ROW_TILE, n), lambda i: (i, 0))
    return pl.pallas_call(
        _merge_kernel,
        grid=(t // ROW_TILE,),
        in_specs=[row(D_MODEL), row(MOBA_W), row(SWA_QW), row(MEM_W),
                  _resident((1, D_MODEL)), _resident((1, D_MODEL)),
                  _resident((D_MODEL, 3 * D_MODEL)), _resident((3, MOBA_W, D_MODEL)),
                  _resident((D_MODEL, D_MODEL))],
        out_specs=row(D_MODEL),
        out_shape=jax.ShapeDtypeStruct((t, D_MODEL), F32),
        compiler_params=_cparams(1),
        name="merge",
    )(x2, omb, osw, ome, gpre, gpost, wg, wb, wo)


def _ffn_kernel(x_ref, xh_ref, gpre_ref, gpost_ref, wup_ref, cw_ref, cb_ref, wd_ref, out_ref, acc_ref,
                *, tiles_per_seq):
    i = pl.program_id(0)
    x = x_ref[...]
    xe = jnp.concatenate([xh_ref[...], x], axis=0)
    he = _rms(xe, gpre_ref[...])
    row = lax.broadcasted_iota(jnp.int32, (ROW_TILE + HALO, 1), 0)
    seq_start = (i % tiles_per_seq) == 0
    he = jnp.where((row < HALO) & seq_start, 0.0, he).astype(BF16)

    def conv(c0):
        u = _mm(he, wup_ref[:, c0:c0 + FFN_CHUNK])
        w = cw_ref[:, c0:c0 + FFN_CHUNK]
        out = cb_ref[:, c0:c0 + FFN_CHUNK] + w[0:1] * pltpu.roll(u, 2, 0)
        out = out + w[1:2] * pltpu.roll(u, 1, 0)
        out = out + w[2:3] * u
        return out[HALO:]

    for fc in range(FFN_HIDDEN // FFN_CHUNK):
        c0 = fc * FFN_CHUNK
        act = (jax.nn.gelu(conv(c0), approximate=True) * conv(FFN_HIDDEN + c0)).astype(BF16)
        part = _mm(act, wd_ref[c0:c0 + FFN_CHUNK, :])
        if fc == 0:
            acc_ref[...] = part
        else:
            acc_ref[...] += part
    out_ref[...] = x + _rms(acc_ref[...], gpost_ref[...])


def _ffn(x2, gpre, gpost, wup, cw, cb, wd, seq_len):
    t = x2.shape[0]
    row = pl.BlockSpec((ROW_TILE, D_MODEL), lambda i: (i, 0))
    halo = pl.BlockSpec((HALO, D_MODEL), lambda i: (jnp.maximum(i * (ROW_TILE // HALO) - 1, 0), 0))
    return pl.pallas_call(
        functools.partial(_ffn_kernel, tiles_per_seq=seq_len // ROW_TILE),
        grid=(t // ROW_TILE,),
        in_specs=[row, halo, _resident((1, D_MODEL)), _resident((1, D_MODEL)),
                  _resident((D_MODEL, 2 * FFN_HIDDEN)), _resident((CONV_WIDTH, 2 * FFN_HIDDEN)),
                  _resident((1, 2 * FFN_HIDDEN)), _resident((FFN_HIDDEN, D_MODEL))],
        out_specs=row,
        out_shape=jax.ShapeDtypeStruct((t, D_MODEL), F32),
        scratch_shapes=[pltpu.VMEM((ROW_TILE, D_MODEL), F32)],
        compiler_params=_cparams(1),
        name="ffn",
    )(x2, x2, gpre, gpost, wup, cw, cb, wd)


def _t5_bucket(dist):
    n = jnp.maximum(dist, 0)
    nf = jnp.maximum(n, 1).astype(F32)
    large = MAX_EXACT + (jnp.log(nf / MAX_EXACT) / math.log(REL_MAX_DISTANCE / MAX_EXACT)
                         * (NUM_BUCKETS - MAX_EXACT)).astype(jnp.int32)
    large = jnp.minimum(large, NUM_BUCKETS - 1)
    return jnp.where(n < MAX_EXACT, n, large)


def _bias_tables(rel_bias):
    bm = rel_bias[:, :MOBA_HEADS].T.astype(F32)
    bs = rel_bias[:, MOBA_HEADS:].T.astype(F32)

    def lookup(table, dist):
        bucket = _t5_bucket(dist)[None]
        out = jnp.zeros((table.shape[0],) + dist.shape, F32)
        for bkt in range(NUM_BUCKETS):
            out = jnp.where(bucket == bkt, table[:, bkt][:, None, None], out)
        return out

    kj = jnp.arange(MOBA_BLOCK)[:, None]
    qi = jnp.arange(MOBA_BLOCK)[None, :]
    far = bm[:, NUM_BUCKETS - 1][:, None, None]
    own = jnp.where((qi - kj) >= 0, (lookup(bm, qi - kj) - far) * LOG2E, NEG)
    prev = (lookup(bm, MOBA_BLOCK + qi - kj) - far) * LOG2E
    tab_moba = jnp.stack([prev, own, jnp.full_like(own, NEG)], axis=1)
    kj = jnp.arange(2 * SWA_BLOCK)[:, None]
    qi = jnp.arange(SWA_BLOCK)[None, :]
    dist = SWA_BLOCK + qi - kj
    tab_swa = jnp.where((dist >= 0) & (dist < SWA_WINDOW), lookup(bs, dist) * LOG2E, NEG)
    return tab_moba, tab_swa


def _head_variants(w, heads):
    w3 = w.reshape(w.shape[0], heads, 1, -1)
    z = jnp.zeros_like(w3)
    lo = jnp.concatenate([w3, z], axis=3)
    hi = jnp.concatenate([z, w3], axis=3)
    return jnp.concatenate([lo, hi], axis=2).reshape(w.shape[0], heads * 2 * LANES)


def _proj_weight(w_in):
    o = 0
    parts = {}
    for name, width in (("qm", MOBA_W), ("km", MOBA_W), ("vm", MOBA_W), ("qs", SWA_QW), ("ks", SWA_KVW),
                        ("vs", SWA_KVW), ("qe", MEM_W)):
        parts[name] = w_in[:, o:o + width]
        o += width
    q_scale = LOG2E / math.sqrt(MOBA_HEAD_DIM)
    w = jnp.concatenate([parts["qm"] * q_scale, parts["km"], parts["qs"] * q_scale,
                         _head_variants(parts["ks"], SWA_KV_HEADS), parts["qe"]], axis=1)
    vs3 = parts["vs"].reshape(-1, SWA_KV_HEADS, SWA_HEAD_DIM)
    vs_dup = jnp.concatenate([vs3, vs3], axis=2).reshape(-1, SWA_VT_ROWS)
    w_t = jnp.concatenate([parts["vm"], vs_dup], axis=1).T
    return w.astype(BF16), w_t.astype(BF16), w_in[:, o:].astype(BF16)


def kernel(x, mem, norm_mix_pre, norm_mix_post, norm_ffn_pre, norm_ffn_post, norm_mem, w_in, rel_bias, swa_sinks,
           w_mem_kv, w_branch_moba, w_branch_swa, w_branch_mem, w_out, w_ffn_up, ffn_conv_w, ffn_conv_b,
           w_ffn_down):
    b, s, d = x.shape
    m_len = mem.shape[1]
    assert d == D_MODEL and s % ROW_TILE == 0 and s // MOBA_BLOCK <= 32 and m_len % LANES == 0
    depth = w_in.shape[0]
    tab_moba, tab_swa = _bias_tables(rel_bias)
    x2 = x.reshape(b * s, d).astype(F32)
    vec = lambda v: v.reshape(1, -1).astype(F32)
    nblk = s // MOBA_BLOCK
    r3 = lambda a: a.reshape(b, s, a.shape[-1])
    for l in range(depth):
        w_qkv, w_t, w_gate = _proj_weight(w_in[l])
        qm, km, vt, qs, ks, vst, qe, kmean = _proj(x2, vec(norm_mix_pre[l]), w_qkv, w_t)
        kme, vmt = _memkv(mem.astype(F32), vec(norm_mem[l]), w_mem_kv[l][:, :MEM_W].astype(BF16),
                          w_mem_kv[l][:, MEM_W:].T.astype(BF16))
        o_mb = _moba(r3(qm), r3(km), vt.reshape(b, nblk, MOBA_W, MOBA_BLOCK), kmean.reshape(b, nblk, MOBA_W),
                     tab_moba)
        sinks = jnp.broadcast_to(swa_sinks[l].astype(F32)[:, None] * LOG2E, (SWA_Q_HEADS, LANES))
        o_sw, o_me = _local(r3(qs), r3(ks), vst.reshape(b, s // SWA_BLOCK, SWA_VT_ROWS, SWA_BLOCK), tab_swa, sinks,
                            r3(qe), kme, vmt)
        w_branch = jnp.stack([w_branch_moba[l], w_branch_swa[l], w_branch_mem[l]]).astype(BF16)
        x2 = _merge(x2, o_mb.reshape(b * s, -1), o_sw.reshape(b * s, -1), o_me.reshape(b * s, -1),
                    vec(norm_mix_pre[l]), vec(norm_mix_post[l]), w_gate, w_branch, w_out[l].astype(BF16))
        x2 = _ffn(x2, vec(norm_ffn_pre[l]), vec(norm_ffn_post[l]), w_ffn_up[l].astype(BF16),
                  ffn_conv_w[l].astype(F32), vec(ffn_conv_b[l]), w_ffn_down[l].astype(BF16), s)
    return x2.reshape(b, s, d).astype(x.dtype)
```

```python
import functools
import math

import jax
import jax.numpy as jnp
from jax import lax
from jax.experimental import pallas as pl
from jax.experimental.pallas import tpu as pltpu

D_MODEL = 1024
MOBA_HEADS = 8
MOBA_HEAD_DIM = 64
MOBA_BLOCK = 256
MOBA_TOPK = 3
SWA_Q_HEADS = 8
SWA_KV_HEADS = 2
SWA_HEAD_DIM = 64
SWA_WINDOW = 128
SWA_BLOCK = 128
MEM_HEADS = 4
MEM_HEAD_DIM = 128
NUM_BUCKETS = 32
MAX_EXACT = NUM_BUCKETS // 2
REL_MAX_DISTANCE = 128
FFN_HIDDEN = 2816
CONV_WIDTH = 3
RMS_EPS = 1e-6

MOBA_W = MOBA_HEADS * MOBA_HEAD_DIM
SWA_QW = SWA_Q_HEADS * SWA_HEAD_DIM
SWA_KVW = SWA_KV_HEADS * SWA_HEAD_DIM
MEM_W = MEM_HEADS * MEM_HEAD_DIM

LANES = 128
NEG = -1e30
VMEM_LIMIT = 56 * 1024 * 1024
ROW_TILE = 512
MOBA_KEYS_PER_STEP = 512
FFN_CHUNK = 256
FFN_GROUP = 4
HALO = 8
SUM_ROWS = 16

BF16 = jnp.bfloat16
F32 = jnp.float32

SWA_K_COLS = 2 * SWA_KV_HEADS * LANES
SWA_VT_ROWS = SWA_KV_HEADS * LANES
OFF_QM = 0
OFF_KM = OFF_QM + MOBA_W
OFF_QS = OFF_KM + MOBA_W
OFF_KS = OFF_QS + SWA_QW
OFF_QE = OFF_KS + SWA_K_COLS
PROJ_W = OFF_QE + MEM_W
LOG2E = math.log2(math.e)


def _mm(a, b):
    return jnp.dot(a, b, preferred_element_type=F32)


def _nt(a, b):
    return lax.dot_general(a, b, (((1,), (1,)), ((), ())), preferred_element_type=F32)


def _rms(xf, g):
    r = lax.rsqrt(jnp.mean(xf * xf, axis=-1, keepdims=True) + RMS_EPS)
    return xf * r * g


def _cparams(n_axes):
    return pltpu.CompilerParams(dimension_semantics=("arbitrary",) * n_axes, vmem_limit_bytes=VMEM_LIMIT)


def _resident(shape):
    nd = len(shape)
    return pl.BlockSpec(shape, lambda *_: (0,) * nd, pipeline_mode=pl.Buffered(1))


def _proj_kernel(x_ref, g_ref, w_ref, wt_ref, qm_ref, km_ref, vt_ref, qs_ref, ks_ref, vst_ref, qe_ref, kmean_ref):
    h = _rms(x_ref[...], g_ref[...]).astype(BF16)

    def seg(c0, n):
        return _mm(h, w_ref[:, c0:c0 + n])

    qm_ref[...] = seg(OFF_QM, MOBA_W).astype(BF16)
    k = seg(OFF_KM, MOBA_W)
    km_ref[...] = k.astype(BF16)
    qs_ref[...] = seg(OFF_QS, SWA_QW).astype(BF16)
    ks_ref[...] = seg(OFF_KS, SWA_K_COLS).astype(BF16)
    qe_ref[...] = seg(OFF_QE, MEM_W).astype(BF16)
    vt = _nt(wt_ref[...], h).astype(BF16)
    for r in range(ROW_TILE // MOBA_BLOCK):
        rows = slice(r * MOBA_BLOCK, (r + 1) * MOBA_BLOCK)
        kmean_ref[0, r:r + 1, :] = jnp.sum(k[rows], axis=0, keepdims=True) * (1.0 / MOBA_BLOCK)
        vt_ref[r] = vt[:MOBA_W, rows]
    for r in range(ROW_TILE // SWA_BLOCK):
        vst_ref[r] = vt[MOBA_W:, r * SWA_BLOCK:(r + 1) * SWA_BLOCK]


def _proj(x2, g, w, wt):
    t = x2.shape[0]
    n_tiles = t // ROW_TILE
    bpt = ROW_TILE // MOBA_BLOCK
    spt = ROW_TILE // SWA_BLOCK
    row = lambda n: pl.BlockSpec((ROW_TILE, n), lambda i: (i, 0))
    out_shapes = (
        jax.ShapeDtypeStruct((t, MOBA_W), BF16),
        jax.ShapeDtypeStruct((t, MOBA_W), BF16),
        jax.ShapeDtypeStruct((t // MOBA_BLOCK, MOBA_W, MOBA_BLOCK), BF16),
        jax.ShapeDtypeStruct((t, SWA_QW), BF16),
        jax.ShapeDtypeStruct((t, SWA_K_COLS), BF16),
        jax.ShapeDtypeStruct((t // SWA_BLOCK, SWA_VT_ROWS, SWA_BLOCK), BF16),
        jax.ShapeDtypeStruct((t, MEM_W), BF16),
        jax.ShapeDtypeStruct((n_tiles, bpt, MOBA_W), F32),
    )
    out_specs = (row(MOBA_W), row(MOBA_W), pl.BlockSpec((bpt, MOBA_W, MOBA_BLOCK), lambda i: (i, 0, 0)),
                 row(SWA_QW), row(SWA_K_COLS), pl.BlockSpec((spt, SWA_VT_ROWS, SWA_BLOCK), lambda i: (i, 0, 0)),
                 row(MEM_W), pl.BlockSpec((1, bpt, MOBA_W), lambda i: (i, 0, 0)))
    return pl.pallas_call(
        _proj_kernel,
        grid=(n_tiles,),
        in_specs=[row(D_MODEL), _resident((1, D_MODEL)), _resident((D_MODEL, PROJ_W)),
                  _resident((MOBA_W + SWA_VT_ROWS, D_MODEL))],
        out_specs=out_specs,
        out_shape=out_shapes,
        compiler_params=_cparams(1),
        name="proj",
    )(x2, g, w, wt)


def _memkv_kernel(m_ref, g_ref, wk_ref, wvt_ref, k_ref, vt_ref):
    h = _rms(m_ref[0], g_ref[...]).astype(BF16)
    k_ref[0] = _mm(h, wk_ref[...]).astype(BF16)
    vt_ref[0] = _nt(wvt_ref[...], h).astype(BF16)


def _memkv(mem, g, wk, wvt):
    b, m_len, _ = mem.shape
    return pl.pallas_call(
        _memkv_kernel,
        grid=(b,),
        in_specs=[pl.BlockSpec((1, m_len, D_MODEL), lambda i: (i, 0, 0)), _resident((1, D_MODEL)),
                  _resident((D_MODEL, MEM_W)), _resident((MEM_W, D_MODEL))],
        out_specs=(pl.BlockSpec((1, m_len, MEM_W), lambda i: (i, 0, 0)),
                   pl.BlockSpec((1, MEM_W, m_len), lambda i: (i, 0, 0))),
        out_shape=(jax.ShapeDtypeStruct((b, m_len, MEM_W), BF16), jax.ShapeDtypeStruct((b, MEM_W, m_len), BF16)),
        compiler_params=_cparams(1),
        name="memkv",
    )(mem, g, wk, wvt)


def _moba_kernel(q_ref, k_ref, vt_ref, kmean_ref, tab_ref, o_ref, mfar_ref, mtail_ref):
    c = pl.program_id(2)
    mb = MOBA_BLOCK
    kb = MOBA_KEYS_PER_STEP
    per = kb // mb
    nblk = kmean_ref.shape[1]
    hd = MOBA_HEAD_DIM
    lane_q = lax.broadcasted_iota(jnp.int32, (mb, LANES), 1)
    lane_k = lax.broadcasted_iota(jnp.int32, (nblk, LANES), 1)
    blk = lax.broadcasted_iota(jnp.int32, (nblk, mb), 0)
    blk_f = blk.astype(F32)
    past = blk < c
    tb = jnp.maximum(c - 1, 0)
    kt = pl.multiple_of(tb * mb, mb)
    left = jnp.where(c == 0, 1, 0)
    right = jnp.where(c == 0, 2, 1)

    ones = jnp.ones((SUM_ROWS, mb), BF16)

    def values(block):
        return jnp.concatenate([vt_ref[0, block], ones], axis=0)

    qf = q_ref[0].astype(F32)
    km = kmean_ref[0]
    k_tail = k_ref[0, pl.ds(kt, 2 * mb), :]
    q_pad = []
    state = []
    for a in range(2):
        qa = jnp.where((lane_q >= a * hd) & (lane_q < (a + 1) * hd), qf, 0.0).astype(BF16)
        q_pad.append(qa)
        km_a = jnp.where((lane_k >= a * hd) & (lane_k < (a + 1) * hd), km, 0.0)
        km_hi = km_a.astype(BF16)
        km_lo = (km_a - km_hi.astype(F32)).astype(BF16)
        gate = _nt(km_hi, qa) + _nt(km_lo, qa)
        g = jnp.where(past, gate, -jnp.inf)
        sel = blk < 0
        for _ in range(MOBA_TOPK):
            top = jnp.max(g, axis=0, keepdims=True)
            first = jnp.min(jnp.where(g == top, blk_f, 1e9), axis=0, keepdims=True)
            hit = blk_f == first
            sel = sel | hit
            g = jnp.where(hit, -jnp.inf, g)
        sel = sel & past
        mfar_ref[a] = jnp.where(sel & (blk <= c - 2), 0.0, NEG)
        mtail_ref[a] = jnp.where((blk == c) | (sel & (blk == c - 1)), 0.0, NEG)

        s = _nt(k_tail, qa)
        s = jnp.concatenate([s[:mb] + tab_ref[a, left] + mtail_ref[a, pl.ds(tb, 1), :],
                             s[mb:] + tab_ref[a, right] + mtail_ref[a, pl.ds(tb + 1, 1), :]], axis=0)
        m0 = jnp.max(s, axis=0, keepdims=True)
        pb = jnp.exp2(s - m0).astype(BF16)
        state.append((m0, _mm(values(tb), pb[:mb]) + _mm(values(tb + 1), pb[mb:])))

    n_far = jnp.where(c >= 2, ((c - 1) * mb + kb - 1) // kb, 0)
    last_step = k_ref.shape[1] // kb - 1

    def far_scores(j):
        kj = k_ref[0, pl.ds(pl.multiple_of(j * kb, kb), kb), :]
        out = []
        for a in range(2):
            s = _nt(kj, q_pad[a])
            s = jnp.concatenate([s[t * mb:(t + 1) * mb] + mfar_ref[a, pl.ds(j * per + t, 1), :]
                                 for t in range(per)], axis=0)
            out.append((s, jnp.max(s, axis=0, keepdims=True)))
        return tuple(out)

    def absorb(stats, scores, j):
        out = []
        for a in range(2):
            m, acc = stats[a]
            s, s_max = scores[a]
            m_new = jnp.maximum(m, s_max)
            pb = jnp.exp2(s - m_new).astype(BF16)
            acc = jnp.exp2(m - m_new) * acc
            for t in range(per):
                acc = acc + _mm(values(j * per + t), pb[t * mb:(t + 1) * mb])
            out.append((m_new, acc))
        return tuple(out)

    def two_steps(i, carry):
        scores, stats = carry
        j = 2 * i
        ahead = far_scores(j + 1)
        stats = absorb(stats, scores, j)
        nxt = far_scores(jnp.minimum(j + 2, last_step))
        return nxt, absorb(stats, ahead, j + 1)

    def last_odd_step(_, carry):
        scores, stats = carry
        return scores, absorb(stats, scores, n_far - 1)

    carry = lax.fori_loop(0, n_far // 2, two_steps, (far_scores(0), tuple(state)))
    _, ((_, acc_a), (_, acc_b)) = lax.fori_loop(0, n_far % 2, last_odd_step, carry)
    row = lax.broadcasted_iota(jnp.int32, (LANES, mb), 0)
    o_t = jnp.where(row < hd, acc_a[:LANES] / acc_a[LANES:LANES + 1], acc_b[:LANES] / acc_b[LANES:LANES + 1])
    o_ref[0] = o_t.T.astype(BF16)


def _moba(qm, km, vt, kmean, tab):
    b, s, _ = qm.shape
    pairs = MOBA_HEADS // 2
    nblk = s // MOBA_BLOCK
    grid = (b, pairs, nblk)
    return pl.pallas_call(
        _moba_kernel,
        grid=grid,
        in_specs=[
            pl.BlockSpec((1, MOBA_BLOCK, LANES), lambda bi, hp, c: (bi, c, hp)),
            pl.BlockSpec((1, s, LANES), lambda bi, hp, c: (bi, 0, hp)),
            pl.BlockSpec((1, nblk, LANES, MOBA_BLOCK), lambda bi, hp, c: (bi, 0, hp, 0)),
            pl.BlockSpec((1, nblk, LANES), lambda bi, hp, c: (bi, 0, hp)),
            pl.BlockSpec((2, 3, MOBA_BLOCK, MOBA_BLOCK), lambda bi, hp, c: (hp, 0, 0, 0)),
        ],
        out_specs=pl.BlockSpec((1, MOBA_BLOCK, LANES), lambda bi, hp, c: (bi, c, hp)),
        out_shape=jax.ShapeDtypeStruct((b, s, MOBA_W), BF16),
        scratch_shapes=[pltpu.VMEM((2, nblk, MOBA_BLOCK), F32), pltpu.VMEM((2, nblk, MOBA_BLOCK), F32)],
        compiler_params=_cparams(3),
        name="moba",
    )(qm, km, vt, kmean, tab)


def _local_kernel(qs_ref, ks_ref, kh_ref, vst_ref, vsh_ref, tab_ref, sink_ref, qe_ref, kme_ref, vmt_ref,
                  osw_ref, ome_ref):
    i = pl.program_id(1)
    qb = SWA_BLOCK
    pairs_per_kv = SWA_Q_HEADS // SWA_KV_HEADS // 2
    no_prev = jnp.where(i == 0, NEG, 0.0)
    row = lax.broadcasted_iota(jnp.int32, (LANES, qb), 0)
    for blk in range(ROW_TILE // qb):
        rows = slice(blk * qb, (blk + 1) * qb)
        for hk in range(SWA_KV_HEADS):
            vrows = slice(hk * LANES, (hk + 1) * LANES)
            v_prev = vsh_ref[0, 0, vrows] if blk == 0 else vst_ref[0, blk - 1, vrows]
            vv = jnp.concatenate([v_prev, vst_ref[0, blk, vrows]], axis=1)
            for jj in range(pairs_per_kv):
                j = hk * pairs_per_kv + jj
                qp = qs_ref[0, rows, j * LANES:(j + 1) * LANES]
                outs = []
                for par in range(2):
                    h = 2 * j + par
                    kc = slice((2 * hk + par) * LANES, (2 * hk + par + 1) * LANES)
                    if blk == 0:
                        kk = jnp.concatenate([kh_ref[0, :, kc], ks_ref[0, 0:qb, kc]], axis=0)
                    else:
                        kk = ks_ref[0, (blk - 1) * qb:(blk + 1) * qb, kc]
                    s = _nt(kk, qp) + tab_ref[h]
                    if blk == 0:
                        s = jnp.concatenate([s[:qb] + no_prev, s[qb:]], axis=0)
                    sink = sink_ref[h:h + 1, :]
                    m = jnp.maximum(jnp.max(s, axis=0, keepdims=True), sink)
                    p = jnp.exp2(s - m)
                    den = jnp.sum(p, axis=0, keepdims=True) + jnp.exp2(sink - m)
                    outs.append(_mm(vv, p.astype(BF16)) / den)
                o_t = jnp.where(row < SWA_HEAD_DIM, outs[0], outs[1])
                osw_ref[0, rows, j * LANES:(j + 1) * LANES] = o_t.T.astype(BF16)

    scale = LOG2E / math.sqrt(MEM_HEAD_DIM)
    for hd in range(MEM_HEADS):
        cols = slice(hd * MEM_HEAD_DIM, (hd + 1) * MEM_HEAD_DIM)
        kk = kme_ref[0, :, cols]
        vv = vmt_ref[0, cols, :]
        for blk in range(ROW_TILE // qb):
            rows = slice(blk * qb, (blk + 1) * qb)
            s = _nt(kk, qe_ref[0, rows, cols]) * scale
            m = jnp.max(s, axis=0, keepdims=True)
            p = jnp.exp2(s - m)
            den = jnp.sum(p, axis=0, keepdims=True)
            ome_ref[0, rows, cols] = (_mm(vv, p.astype(BF16)) / den).T.astype(BF16)


def _local(qs, ks, vst, tab, sinks, qe, kme, vmt):
    b, s, _ = qs.shape
    per_tile = ROW_TILE // SWA_BLOCK
    m_len = kme.shape[1]
    tile = lambda n: pl.BlockSpec((1, ROW_TILE, n), lambda bi, i: (bi, i, 0))
    prev_block = lambda bi, i: (bi, jnp.maximum(i * per_tile - 1, 0), 0)
    return pl.pallas_call(
        _local_kernel,
        grid=(b, s // ROW_TILE),
        in_specs=[
            tile(SWA_QW),
            tile(SWA_K_COLS), pl.BlockSpec((1, SWA_BLOCK, SWA_K_COLS), prev_block),
            pl.BlockSpec((1, per_tile, SWA_VT_ROWS, SWA_BLOCK), lambda bi, i: (bi, i, 0, 0)),
            pl.BlockSpec((1, 1, SWA_VT_ROWS, SWA_BLOCK), lambda bi, i: prev_block(bi, i) + (0,)),
            _resident((SWA_Q_HEADS, 2 * SWA_BLOCK, SWA_BLOCK)),
            _resident((SWA_Q_HEADS, LANES)),
            tile(MEM_W),
            pl.BlockSpec((1, m_len, MEM_W), lambda bi, i: (bi, 0, 0)),
            pl.BlockSpec((1, MEM_W, m_len), lambda bi, i: (bi, 0, 0)),
        ],
        out_specs=(tile(SWA_QW), tile(MEM_W)),
        out_shape=(jax.ShapeDtypeStruct((b, s, SWA_QW), BF16), jax.ShapeDtypeStruct((b, s, MEM_W), BF16)),
        compiler_params=_cparams(2),
        name="local",
    )(qs, ks, ks, vst, vst, tab, sinks, qe, kme, vmt)


def _merge_kernel(x_ref, omb_ref, osw_ref, ome_ref, gpre_ref, gpost_ref, wg_ref, wb_ref, wo_ref, out_ref):
    x = x_ref[...]
    h = _rms(x, gpre_ref[...]).astype(BF16)
    branch = (omb_ref[...], osw_ref[...], ome_ref[...])
    half = D_MODEL // 2
    parts = []
    for nc in range(2):
        acc = None
        for br in range(3):
            c0 = br * D_MODEL + nc * half
            gate = jax.nn.sigmoid(_mm(h, wg_ref[:, c0:c0 + half]))
            term = gate * _mm(branch[br], wb_ref[br, :, nc * half:(nc + 1) * half])
            acc = term if acc is None else acc + term
        parts.append(acc.astype(BF16))
    y = _mm(jnp.concatenate(parts, axis=1), wo_ref[...])
    out_ref[...] = x + _rms(y, gpost_ref[...])


def _merge(x2, omb, osw, ome, gpre, gpost, wg, wb, wo):
    t = x2.shape[0]
    row = lambda n: pl.BlockSpec((ROW_TILE, n), lambda i: (i, 0))
    return pl.pallas_call(
        _merge_kernel,
        grid=(t // ROW_TILE,),
        in_specs=[row(D_MODEL), row(MOBA_W), row(SWA_QW), row(MEM_W),
                  _resident((1, D_MODEL)), _resident((1, D_MODEL)),
                  _resident((D_MODEL, 3 * D_MODEL)), _resident((3, MOBA_W, D_MODEL)),
                  _resident((D_MODEL, D_MODEL))],
        out_specs=row(D_MODEL),
        out_shape=jax.ShapeDtypeStruct((t, D_MODEL), F32),
        compiler_params=_cparams(1),
        name="merge",
    )(x2, omb, osw, ome, gpre, gpost, wg, wb, wo)


def _ffn_kernel(x_ref, xh_ref, gpre_ref, gpost_ref, wup_ref, cw_ref, cb_ref, wd_ref, out_ref, acc_ref, act_ref,
                *, tiles_per_seq):
    i = pl.program_id(0)
    x = x_ref[...]
    xe = jnp.concatenate([xh_ref[...], x], axis=0)
    he = _rms(xe, gpre_ref[...])
    row = lax.broadcasted_iota(jnp.int32, (ROW_TILE + HALO, 1), 0)
    seq_start = (i % tiles_per_seq) == 0
    he = jnp.where((row < HALO) & seq_start, 0.0, he).astype(BF16)

    def up(fc):
        c0 = fc * FFN_CHUNK
        return (_mm(he, wup_ref[:, c0:c0 + FFN_CHUNK]),
                _mm(he, wup_ref[:, FFN_HIDDEN + c0:FFN_HIDDEN + c0 + FFN_CHUNK]))

    def conv(u, c0):
        w = cw_ref[:, c0:c0 + FFN_CHUNK]
        out = cb_ref[:, c0:c0 + FFN_CHUNK] + w[0:1] * pltpu.roll(u, 2, 0)
        out = out + w[1:2] * pltpu.roll(u, 1, 0)
        out = out + w[2:3] * u
        return out[HALO:]

    n_chunks = FFN_HIDDEN // FFN_CHUNK
    u_gate, u_val = up(0)
    for fc in range(n_chunks):
        c0 = fc * FFN_CHUNK
        ahead = up(fc + 1) if fc + 1 < n_chunks else None
        group, k = divmod(fc, FFN_GROUP)
        act = jax.nn.gelu(conv(u_gate, c0), approximate=True) * conv(u_val, FFN_HIDDEN + c0)
        act_ref[group % 2, :, k * FFN_CHUNK:(k + 1) * FFN_CHUNK] = act.astype(BF16)
        if k == FFN_GROUP - 1 or ahead is None:
            g0 = group * FFN_GROUP * FFN_CHUNK
            width = (k + 1) * FFN_CHUNK
            part = _mm(act_ref[group % 2, :, :width], wd_ref[g0:g0 + width, :])
            if group == 0:
                acc_ref[...] = part
            else:
                acc_ref[...] += part
        if ahead is not None:
            u_gate, u_val = ahead
    out_ref[...] = x + _rms(acc_ref[...], gpost_ref[...])


def _ffn(x2, gpre, gpost, wup, cw, cb, wd, seq_len):
    t = x2.shape[0]
    row = pl.BlockSpec((ROW_TILE, D_MODEL), lambda i: (i, 0))
    halo = pl.BlockSpec((HALO, D_MODEL), lambda i: (jnp.maximum(i * (ROW_TILE // HALO) - 1, 0), 0))
    return pl.pallas_call(
        functools.partial(_ffn_kernel, tiles_per_seq=seq_len // ROW_TILE),
        grid=(t // ROW_TILE,),
        in_specs=[row, halo, _resident((1, D_MODEL)), _resident((1, D_MODEL)),
                  _resident((D_MODEL, 2 * FFN_HIDDEN)), _resident((CONV_WIDTH, 2 * FFN_HIDDEN)),
                  _resident((1, 2 * FFN_HIDDEN)), _resident((FFN_HIDDEN, D_MODEL))],
        out_specs=row,
        out_shape=jax.ShapeDtypeStruct((t, D_MODEL), F32),
        scratch_shapes=[pltpu.VMEM((ROW_TILE, D_MODEL), F32),
                        pltpu.VMEM((2, ROW_TILE, FFN_GROUP * FFN_CHUNK), BF16)],
        compiler_params=_cparams(1),
        name="ffn",
    )(x2, x2, gpre, gpost, wup, cw, cb, wd)


def _t5_bucket(dist):
    n = jnp.maximum(dist, 0)
    nf = jnp.maximum(n, 1).astype(F32)
    large = MAX_EXACT + (jnp.log(nf / MAX_EXACT) / math.log(REL_MAX_DISTANCE / MAX_EXACT)
                         * (NUM_BUCKETS - MAX_EXACT)).astype(jnp.int32)
    large = jnp.minimum(large, NUM_BUCKETS - 1)
    return jnp.where(n < MAX_EXACT, n, large)


def _bias_tables(rel_bias):
    bm = rel_bias[:, :MOBA_HEADS].T.astype(F32)
    bs = rel_bias[:, MOBA_HEADS:].T.astype(F32)

    def lookup(table, dist):
        bucket = _t5_bucket(dist)[None]
        out = jnp.zeros((table.shape[0],) + dist.shape, F32)
        for bkt in range(NUM_BUCKETS):
            out = jnp.where(bucket == bkt, table[:, bkt][:, None, None], out)
        return out

    kj = jnp.arange(MOBA_BLOCK)[:, None]
    qi = jnp.arange(MOBA_BLOCK)[None, :]
    far = bm[:, NUM_BUCKETS - 1][:, None, None]
    own = jnp.where((qi - kj) >= 0, (lookup(bm, qi - kj) - far) * LOG2E, NEG)
    prev = (lookup(bm, MOBA_BLOCK + qi - kj) - far) * LOG2E
    tab_moba = jnp.stack([prev, own, jnp.full_like(own, NEG)], axis=1)
    kj = jnp.arange(2 * SWA_BLOCK)[:, None]
    qi = jnp.arange(SWA_BLOCK)[None, :]
    dist = SWA_BLOCK + qi - kj
    tab_swa = jnp.where((dist >= 0) & (dist < SWA_WINDOW), lookup(bs, dist) * LOG2E, NEG)
    return tab_moba, tab_swa


def _head_variants(w, heads):
    w3 = w.reshape(w.shape[0], heads, 1, -1)
    z = jnp.zeros_like(w3)
    lo = jnp.concatenate([w3, z], axis=3)
    hi = jnp.concatenate([z, w3], axis=3)
    return jnp.concatenate([lo, hi], axis=2).reshape(w.shape[0], heads * 2 * LANES)


def _proj_weight(w_in):
    o = 0
    parts = {}
    for name, width in (("qm", MOBA_W), ("km", MOBA_W), ("vm", MOBA_W), ("qs", SWA_QW), ("ks", SWA_KVW),
                        ("vs", SWA_KVW), ("qe", MEM_W)):
        parts[name] = w_in[:, o:o + width]
        o += width
    q_scale = LOG2E / math.sqrt(MOBA_HEAD_DIM)
    w = jnp.concatenate([parts["qm"] * q_scale, parts["km"], parts["qs"] * q_scale,
                         _head_variants(parts["ks"], SWA_KV_HEADS), parts["qe"]], axis=1)
    vs3 = parts["vs"].reshape(-1, SWA_KV_HEADS, SWA_HEAD_DIM)
    vs_dup = jnp.concatenate([vs3, vs3], axis=2).reshape(-1, SWA_VT_ROWS)
    w_t = jnp.concatenate([parts["vm"], vs_dup], axis=1).T
    return w.astype(BF16), w_t.astype(BF16), w_in[:, o:].astype(BF16)


def kernel(x, mem, norm_mix_pre, norm_mix_post, norm_ffn_pre, norm_ffn_post, norm_mem, w_in, rel_bias, swa_sinks,
           w_mem_kv, w_branch_moba, w_branch_swa, w_branch_mem, w_out, w_ffn_up, ffn_conv_w, ffn_conv_b,
           w_ffn_down):
    b, s, d = x.shape
    m_len = mem.shape[1]
    assert d == D_MODEL and s % ROW_TILE == 0 and s // MOBA_BLOCK <= 32 and m_len % LANES == 0
    depth = w_in.shape[0]
    tab_moba, tab_swa = _bias_tables(rel_bias)
    x2 = x.reshape(b * s, d).astype(F32)
    vec = lambda v: v.reshape(1, -1).astype(F32)
    nblk = s // MOBA_BLOCK
    r3 = lambda a: a.reshape(b, s, a.shape[-1])
    for l in range(depth):
        w_qkv, w_t, w_gate = _proj_weight(w_in[l])
        qm, km, vt, qs, ks, vst, qe, kmean = _proj(x2, vec(norm_mix_pre[l]), w_qkv, w_t)
        kme, vmt = _memkv(mem.astype(F32), vec(norm_mem[l]), w_mem_kv[l][:, :MEM_W].astype(BF16),
                          w_mem_kv[l][:, MEM_W:].T.astype(BF16))
        o_mb = _moba(r3(qm), r3(km), vt.reshape(b, nblk, MOBA_W, MOBA_BLOCK), kmean.reshape(b, nblk, MOBA_W),
                     tab_moba)
        sinks = jnp.broadcast_to(swa_sinks[l].astype(F32)[:, None] * LOG2E, (SWA_Q_HEADS, LANES))
        o_sw, o_me = _local(r3(qs), r3(ks), vst.reshape(b, s // SWA_BLOCK, SWA_VT_ROWS, SWA_BLOCK), tab_swa, sinks,
                            r3(qe), kme, vmt)
        w_branch = jnp.stack([w_branch_moba[l], w_branch_swa[l], w_branch_mem[l]]).astype(BF16)
        x2 = _merge(x2, o_mb.reshape(b * s, -1), o_sw.reshape(b * s, -1), o_me.reshape(b * s, -1),
                    vec(norm_mix_pre[l]), vec(norm_mix_post[l]), w_gate, w_branch, w_out[l].astype(BF16))
        x2 = _ffn(x2, vec(norm_ffn_pre[l]), vec(norm_ffn_post[l]), w_ffn_up[l].astype(BF16),
                  ffn_conv_w[l].astype(F32), vec(ffn_conv_b[l]), w_ffn_down[l].astype(BF16), s)
    return x2.reshape(b, s, d).astype(x.dtype)
```

```python
import functools
import math

import jax
import jax.numpy as jnp
from jax import lax
from jax.experimental import pallas as pl
from jax.experimental.pallas import tpu as pltpu

D_MODEL = 1024
MOBA_HEADS = 8
MOBA_HEAD_DIM = 64
MOBA_BLOCK = 256
MOBA_TOPK = 3
SWA_Q_HEADS = 8
SWA_KV_HEADS = 2
SWA_HEAD_DIM = 64
SWA_WINDOW = 128
SWA_BLOCK = 128
MEM_HEADS = 4
MEM_HEAD_DIM = 128
NUM_BUCKETS = 32
MAX_EXACT = NUM_BUCKETS // 2
REL_MAX_DISTANCE = 128
FFN_HIDDEN = 2816
CONV_WIDTH = 3
RMS_EPS = 1e-6

MOBA_W = MOBA_HEADS * MOBA_HEAD_DIM
SWA_QW = SWA_Q_HEADS * SWA_HEAD_DIM
SWA_KVW = SWA_KV_HEADS * SWA_HEAD_DIM
MEM_W = MEM_HEADS * MEM_HEAD_DIM

LANES = 128
NEG = -1e30
VMEM_LIMIT = 56 * 1024 * 1024
ROW_TILE = 512
MOBA_KEYS_PER_STEP = 512
MOBA_STEPS_PER_TRIP = (2, 1)
FFN_CHUNK = 256
FFN_GROUP = 4
HALO = 8
SUM_ROWS = 16
MAX_BLOCKS = 32

BF16 = jnp.bfloat16
F32 = jnp.float32

SWA_K_COLS = 2 * SWA_KV_HEADS * LANES
SWA_VT_ROWS = SWA_KV_HEADS * LANES
OFF_QM = 0
OFF_KM = OFF_QM + MOBA_W
OFF_QS = OFF_KM + MOBA_W
OFF_KS = OFF_QS + SWA_QW
OFF_QE = OFF_KS + SWA_K_COLS
PROJ_W = OFF_QE + MEM_W
LOG2E = math.log2(math.e)


def _mm(a, b):
    return jnp.dot(a, b, preferred_element_type=F32)


def _nt(a, b):
    return lax.dot_general(a, b, (((1,), (1,)), ((), ())), preferred_element_type=F32)


def _rms(xf, g):
    r = lax.rsqrt(jnp.mean(xf * xf, axis=-1, keepdims=True) + RMS_EPS)
    return xf * r * g


def _cparams(n_axes):
    return pltpu.CompilerParams(dimension_semantics=("arbitrary",) * n_axes, vmem_limit_bytes=VMEM_LIMIT)


def _resident(shape):
    nd = len(shape)
    return pl.BlockSpec(shape, lambda *_: (0,) * nd, pipeline_mode=pl.Buffered(1))


def _proj_kernel(x_ref, g_ref, w_ref, wt_ref, qm_ref, km_ref, vt_ref, qs_ref, ks_ref, vst_ref, qe_ref, sel_ref,
                 kmean_ref, *, tiles_per_seq):
    i = pl.program_id(0)
    h = _rms(x_ref[...], g_ref[...]).astype(BF16)

    def seg(c0, n):
        return _mm(h, w_ref[:, c0:c0 + n])

    q = seg(OFF_QM, MOBA_W).astype(BF16)
    qm_ref[...] = q
    k = seg(OFF_KM, MOBA_W)
    km_ref[...] = k.astype(BF16)
    qs_ref[...] = seg(OFF_QS, SWA_QW).astype(BF16)
    ks_ref[...] = seg(OFF_KS, SWA_K_COLS).astype(BF16)
    qe_ref[...] = seg(OFF_QE, MEM_W).astype(BF16)
    vt = _nt(wt_ref[...], h).astype(BF16)
    for r in range(ROW_TILE // SWA_BLOCK):
        vst_ref[r] = vt[MOBA_W:, r * SWA_BLOCK:(r + 1) * SWA_BLOCK]

    @pl.when(i % tiles_per_seq == 0)
    def _():
        kmean_ref[...] = jnp.zeros_like(kmean_ref)

    blocks_per_tile = ROW_TILE // MOBA_BLOCK
    c0 = (i % tiles_per_seq) * blocks_per_tile
    lane = lax.broadcasted_iota(jnp.int32, (1, MOBA_W), 1)
    for r in range(blocks_per_tile):
        rows = slice(r * MOBA_BLOCK, (r + 1) * MOBA_BLOCK)
        vt_ref[r] = vt[:MOBA_W, rows]
        mean = jnp.sum(k[rows], axis=0, keepdims=True) * (1.0 / MOBA_BLOCK)
        for hd in range(MOBA_HEADS):
            in_head = (lane >= hd * MOBA_HEAD_DIM) & (lane < (hd + 1) * MOBA_HEAD_DIM)
            kmean_ref[pl.ds(hd * MAX_BLOCKS + c0 + r, 1), :] = jnp.where(in_head, mean, 0.0)

    km = kmean_ref[...]
    km_hi = km.astype(BF16)
    km_lo = (km - km_hi.astype(F32)).astype(BF16)
    gates = _nt(km_hi, q) + _nt(km_lo, q)
    blk = lax.broadcasted_iota(jnp.int32, (MAX_BLOCKS, ROW_TILE), 0)
    blk_f = blk.astype(F32)
    col = lax.broadcasted_iota(jnp.int32, (MAX_BLOCKS, ROW_TILE), 1)
    q_blk = c0 + sum((col >= r * MOBA_BLOCK).astype(jnp.int32) for r in range(1, blocks_per_tile))
    past = blk < q_blk
    for hd in range(MOBA_HEADS):
        g = jnp.where(past, gates[hd * MAX_BLOCKS:(hd + 1) * MAX_BLOCKS], -jnp.inf)
        sel = blk < 0
        for _ in range(MOBA_TOPK):
            top = jnp.max(g, axis=0, keepdims=True)
            first = jnp.min(jnp.where(g == top, blk_f, 1e9), axis=0, keepdims=True)
            hit = blk_f == first
            sel = sel | hit
            g = jnp.where(hit, -jnp.inf, g)
        sel_ref[0, hd] = jnp.where(sel & past, 1.0, 0.0)


def _proj(x2, g, w, wt, seq_len):
    t = x2.shape[0]
    n_tiles = t // ROW_TILE
    tiles_per_seq = seq_len // ROW_TILE
    bpt = ROW_TILE // MOBA_BLOCK
    spt = ROW_TILE // SWA_BLOCK
    row = lambda n: pl.BlockSpec((ROW_TILE, n), lambda i: (i, 0))
    out_shapes = (
        jax.ShapeDtypeStruct((t, MOBA_W), BF16),
        jax.ShapeDtypeStruct((t, MOBA_W), BF16),
        jax.ShapeDtypeStruct((t // MOBA_BLOCK, MOBA_W, MOBA_BLOCK), BF16),
        jax.ShapeDtypeStruct((t, SWA_QW), BF16),
        jax.ShapeDtypeStruct((t, SWA_K_COLS), BF16),
        jax.ShapeDtypeStruct((t // SWA_BLOCK, SWA_VT_ROWS, SWA_BLOCK), BF16),
        jax.ShapeDtypeStruct((t, MEM_W), BF16),
        jax.ShapeDtypeStruct((t // seq_len, MOBA_HEADS, MAX_BLOCKS, seq_len), F32),
    )
    out_specs = (row(MOBA_W), row(MOBA_W), pl.BlockSpec((bpt, MOBA_W, MOBA_BLOCK), lambda i: (i, 0, 0)),
                 row(SWA_QW), row(SWA_K_COLS), pl.BlockSpec((spt, SWA_VT_ROWS, SWA_BLOCK), lambda i: (i, 0, 0)),
                 row(MEM_W),
                 pl.BlockSpec((1, MOBA_HEADS, MAX_BLOCKS, ROW_TILE),
                              lambda i: (i // tiles_per_seq, 0, 0, i % tiles_per_seq)))
    return pl.pallas_call(
        functools.partial(_proj_kernel, tiles_per_seq=tiles_per_seq),
        grid=(n_tiles,),
        in_specs=[row(D_MODEL), _resident((1, D_MODEL)), _resident((D_MODEL, PROJ_W)),
                  _resident((MOBA_W + SWA_VT_ROWS, D_MODEL))],
        out_specs=out_specs,
        out_shape=out_shapes,
        scratch_shapes=[pltpu.VMEM((MOBA_HEADS * MAX_BLOCKS, MOBA_W), F32)],
        compiler_params=_cparams(1),
        name="proj",
    )(x2, g, w, wt)


def _memkv_kernel(m_ref, g_ref, wk_ref, wvt_ref, k_ref, vt_ref):
    h = _rms(m_ref[0], g_ref[...]).astype(BF16)
    k_ref[0] = _mm(h, wk_ref[...]).astype(BF16)
    vt_ref[0] = _nt(wvt_ref[...], h).astype(BF16)


def _memkv(mem, g, wk, wvt):
    b, m_len, _ = mem.shape
    return pl.pallas_call(
        _memkv_kernel,
        grid=(b,),
        in_specs=[pl.BlockSpec((1, m_len, D_MODEL), lambda i: (i, 0, 0)), _resident((1, D_MODEL)),
                  _resident((D_MODEL, MEM_W)), _resident((MEM_W, D_MODEL))],
        out_specs=(pl.BlockSpec((1, m_len, MEM_W), lambda i: (i, 0, 0)),
                   pl.BlockSpec((1, MEM_W, m_len), lambda i: (i, 0, 0))),
        out_shape=(jax.ShapeDtypeStruct((b, m_len, MEM_W), BF16), jax.ShapeDtypeStruct((b, MEM_W, m_len), BF16)),
        compiler_params=_cparams(1),
        name="memkv",
    )(mem, g, wk, wvt)


def _moba_kernel(q_ref, k_ref, vt_ref, sel_ref, tab_ref, o_ref, mfar_ref, mtail_ref):
    c = pl.program_id(2)
    mb = MOBA_BLOCK
    kb = MOBA_KEYS_PER_STEP
    per = kb // mb
    hd = MOBA_HEAD_DIM
    lane_q = lax.broadcasted_iota(jnp.int32, (mb, LANES), 1)
    blk = lax.broadcasted_iota(jnp.int32, (MAX_BLOCKS, mb), 0)
    tb = jnp.maximum(c - 1, 0)
    kt = pl.multiple_of(tb * mb, mb)
    left = jnp.where(c == 0, 1, 0)
    right = jnp.where(c == 0, 2, 1)

    ones = jnp.ones((SUM_ROWS, mb), BF16)

    def values(block):
        return jnp.concatenate([vt_ref[0, block], ones], axis=0)

    qf = q_ref[0].astype(F32)
    k_tail = k_ref[0, pl.ds(kt, 2 * mb), :]
    q_pad = []
    state = []
    tail_scores = []
    for a in range(2):
        qa = jnp.where((lane_q >= a * hd) & (lane_q < (a + 1) * hd), qf, 0.0).astype(BF16)
        q_pad.append(qa)
        sel = sel_ref[0, a] > 0.5
        mfar_ref[a] = jnp.where(sel & (blk <= c - 2), 0.0, NEG)
        mtail_ref[a] = jnp.where((blk == c) | (sel & (blk == c - 1)), 0.0, NEG)

        s = _nt(k_tail, qa)
        s = jnp.concatenate([s[:mb] + tab_ref[a, left] + mtail_ref[a, pl.ds(tb, 1), :],
                             s[mb:] + tab_ref[a, right] + mtail_ref[a, pl.ds(tb + 1, 1), :]], axis=0)
        tail_scores.append((s, jnp.max(s, axis=0, keepdims=True)))
        state.append((jnp.full((1, mb), -jnp.inf, F32), jnp.zeros((LANES + SUM_ROWS, mb), F32)))

    n_far = jnp.where(c >= 2, ((c - 1) * mb + kb - 1) // kb, 0)
    last_step = k_ref.shape[1] // kb - 1

    def far_scores(j):
        kj = k_ref[0, pl.ds(pl.multiple_of(j * kb, kb), kb), :]
        out = []
        for a in range(2):
            s = _nt(kj, q_pad[a])
            s = jnp.concatenate([s[t * mb:(t + 1) * mb] + mfar_ref[a, pl.ds(j * per + t, 1), :]
                                 for t in range(per)], axis=0)
            out.append((s, jnp.max(s, axis=0, keepdims=True)))
        return tuple(out)

    def absorb(stats, scores, block):
        out = []
        for a in range(2):
            m, acc = stats[a]
            s, s_max = scores[a]
            m_new = jnp.maximum(m, s_max)
            pb = jnp.exp2(s - m_new).astype(BF16)
            acc = jnp.exp2(m - m_new) * acc
            for t in range(per):
                acc = acc + _mm(values(block + t), pb[t * mb:(t + 1) * mb])
            out.append((m_new, acc))
        return tuple(out)

    def steps(count, first, final):
        def trip(i, carry):
            scores, stats, block = carry
            j = first + count * i
            for u in range(count):
                if final and u == count - 1:
                    return scores, absorb(stats, scores, block), block
                ahead = far_scores(jnp.minimum(j + u, last_step))
                stats = absorb(stats, scores, block)
                scores, block = ahead, (j + u) * per
            return scores, stats, block
        return trip

    carry = (tuple(tail_scores), tuple(state), tb)
    done = 0
    assert len(MOBA_STEPS_PER_TRIP) > 1 and MOBA_STEPS_PER_TRIP[-1] == 1
    for count in MOBA_STEPS_PER_TRIP:
        trips = (n_far + 1 - done) // count
        carry = lax.fori_loop(0, trips, steps(count, done, final=count == 1), carry)
        done = done + trips * count
    _, ((_, acc_a), (_, acc_b)), _ = carry
    row = lax.broadcasted_iota(jnp.int32, (LANES, mb), 0)
    o_t = jnp.where(row < hd, acc_a[:LANES] / acc_a[LANES:LANES + 1], acc_b[:LANES] / acc_b[LANES:LANES + 1])
    o_ref[0] = o_t.T.astype(BF16)


def _moba(qm, km, vt, sel, tab):
    b, s, _ = qm.shape
    pairs = MOBA_HEADS // 2
    nblk = s // MOBA_BLOCK
    grid = (b, pairs, nblk)
    return pl.pallas_call(
        _moba_kernel,
        grid=grid,
        in_specs=[
            pl.BlockSpec((1, MOBA_BLOCK, LANES), lambda bi, hp, c: (bi, c, hp)),
            pl.BlockSpec((1, s, LANES), lambda bi, hp, c: (bi, 0, hp)),
            pl.BlockSpec((1, nblk, LANES, MOBA_BLOCK), lambda bi, hp, c: (bi, 0, hp, 0)),
            pl.BlockSpec((1, 2, MAX_BLOCKS, MOBA_BLOCK), lambda bi, hp, c: (bi, hp, 0, c)),
            pl.BlockSpec((2, 3, MOBA_BLOCK, MOBA_BLOCK), lambda bi, hp, c: (hp, 0, 0, 0)),
        ],
        out_specs=pl.BlockSpec((1, MOBA_BLOCK, LANES), lambda bi, hp, c: (bi, c, hp)),
        out_shape=jax.ShapeDtypeStruct((b, s, MOBA_W), BF16),
        scratch_shapes=[pltpu.VMEM((2, MAX_BLOCKS, MOBA_BLOCK), F32), pltpu.VMEM((2, MAX_BLOCKS, MOBA_BLOCK), F32)],
        compiler_params=_cparams(3),
        name="moba",
    )(qm, km, vt, sel, tab)


def _local_kernel(qs_ref, ks_ref, kh_ref, vst_ref, vsh_ref, tab_ref, sink_ref, qe_ref, kme_ref, vmt_ref,
                  osw_ref, ome_ref):
    i = pl.program_id(1)
    qb = SWA_BLOCK
    pairs_per_kv = SWA_Q_HEADS // SWA_KV_HEADS // 2
    no_prev = jnp.where(i == 0, NEG, 0.0)
    row = lax.broadcasted_iota(jnp.int32, (LANES, qb), 0)
    for blk in range(ROW_TILE // qb):
        rows = slice(blk * qb, (blk + 1) * qb)
        for hk in range(SWA_KV_HEADS):
            vrows = slice(hk * LANES, (hk + 1) * LANES)
            v_prev = vsh_ref[0, 0, vrows] if blk == 0 else vst_ref[0, blk - 1, vrows]
            vv = jnp.concatenate([v_prev, vst_ref[0, blk, vrows]], axis=1)
            for jj in range(pairs_per_kv):
                j = hk * pairs_per_kv + jj
                qp = qs_ref[0, rows, j * LANES:(j + 1) * LANES]
                outs = []
                for par in range(2):
                    h = 2 * j + par
                    kc = slice((2 * hk + par) * LANES, (2 * hk + par + 1) * LANES)
                    if blk == 0:
                        kk = jnp.concatenate([kh_ref[0, :, kc], ks_ref[0, 0:qb, kc]], axis=0)
                    else:
                        kk = ks_ref[0, (blk - 1) * qb:(blk + 1) * qb, kc]
                    s = _nt(kk, qp) + tab_ref[h]
                    if blk == 0:
                        s = jnp.concatenate([s[:qb] + no_prev, s[qb:]], axis=0)
                    sink = sink_ref[h:h + 1, :]
                    m = jnp.maximum(jnp.max(s, axis=0, keepdims=True), sink)
                    p = jnp.exp2(s - m)
                    den = jnp.sum(p, axis=0, keepdims=True) + jnp.exp2(sink - m)
                    outs.append(_mm(vv, p.astype(BF16)) / den)
                o_t = jnp.where(row < SWA_HEAD_DIM, outs[0], outs[1])
                osw_ref[0, rows, j * LANES:(j + 1) * LANES] = o_t.T.astype(BF16)

    scale = LOG2E / math.sqrt(MEM_HEAD_DIM)
    for hd in range(MEM_HEADS):
        cols = slice(hd * MEM_HEAD_DIM, (hd + 1) * MEM_HEAD_DIM)
        kk = kme_ref[0, :, cols]
        vv = vmt_ref[0, cols, :]
        for blk in range(ROW_TILE // qb):
            rows = slice(blk * qb, (blk + 1) * qb)
            s = _nt(kk, qe_ref[0, rows, cols]) * scale
            m = jnp.max(s, axis=0, keepdims=True)
            p = jnp.exp2(s - m)
            den = jnp.sum(p, axis=0, keepdims=True)
            ome_ref[0, rows, cols] = (_mm(vv, p.astype(BF16)) / den).T.astype(BF16)


def _local(qs, ks, vst, tab, sinks, qe, kme, vmt):
    b, s, _ = qs.shape
    per_tile = ROW_TILE // SWA_BLOCK
    m_len = kme.shape[1]
    tile = lambda n: pl.BlockSpec((1, ROW_TILE, n), lambda bi, i: (bi, i, 0))
    prev_block = lambda bi, i: (bi, jnp.maximum(i * per_tile - 1, 0), 0)
    return pl.pallas_call(
        _local_kernel,
        grid=(b, s // ROW_TILE),
        in_specs=[
            tile(SWA_QW),
            tile(SWA_K_COLS), pl.BlockSpec((1, SWA_BLOCK, SWA_K_COLS), prev_block),
            pl.BlockSpec((1, per_tile, SWA_VT_ROWS, SWA_BLOCK), lambda bi, i: (bi, i, 0, 0)),
            pl.BlockSpec((1, 1, SWA_VT_ROWS, SWA_BLOCK), lambda bi, i: prev_block(bi, i) + (0,)),
            _resident((SWA_Q_HEADS, 2 * SWA_BLOCK, SWA_BLOCK)),
            _resident((SWA_Q_HEADS, LANES)),
            tile(MEM_W),
            pl.BlockSpec((1, m_len, MEM_W), lambda bi, i: (bi, 0, 0)),
            pl.BlockSpec((1, MEM_W, m_len), lambda bi, i: (bi, 0, 0)),
        ],
        out_specs=(tile(SWA_QW), tile(MEM_W)),
        out_shape=(jax.ShapeDtypeStruct((b, s, SWA_QW), BF16), jax.ShapeDtypeStruct((b, s, MEM_W), BF16)),
        compiler_params=_cparams(2),
        name="local",
    )(qs, ks, ks, vst, vst, tab, sinks, qe, kme, vmt)


def _merge_kernel(x_ref, omb_ref, osw_ref, ome_ref, gpre_ref, gpost_ref, wg_ref, wb_ref, wo_ref, out_ref):
    x = x_ref[...]
    h = _rms(x, gpre_ref[...]).astype(BF16)
    branch = (omb_ref[...], osw_ref[...], ome_ref[...])
    half = D_MODEL // 2
    parts = []
    for nc in range(2):
        acc = None
        for br in range(3):
            c0 = br * D_MODEL + nc * half
            gate = jax.nn.sigmoid(_mm(h, wg_ref[:, c0:c0 + half]))
            term = gate * _mm(branch[br], wb_ref[br, :, nc * half:(nc + 1) * half])
            acc = term if acc is None else acc + term
        parts.append(acc.astype(BF16))
    y = _mm(jnp.concatenate(parts, axis=1), wo_ref[...])
    out_ref[...] = x + _rms(y, gpost_ref[...])


def _merge(x2, omb, osw, ome, gpre, gpost, wg, wb, wo):
    t = x2.shape[0]
    row = lambda n: pl.BlockSpec((ROW_TILE, n), lambda i: (i, 0))
    return pl.pallas_call(
        _merge_kernel,
        grid=(t // ROW_TILE,),
        in_specs=[row(D_MODEL), row(MOBA_W), row(SWA_QW), row(MEM_W),
                  _resident((1, D_MODEL)), _resident((1, D_MODEL)),
                  _resident((D_MODEL, 3 * D_MODEL)), _resident((3, MOBA_W, D_MODEL)),
                  _resident((D_MODEL, D_MODEL))],
        out_specs=row(D_MODEL),
        out_shape=jax.ShapeDtypeStruct((t, D_MODEL), F32),
        compiler_params=_cparams(1),
        name="merge",
    )(x2, omb, osw, ome, gpre, gpost, wg, wb, wo)


def _ffn_kernel(x_ref, xh_ref, gpre_ref, gpost_ref, wup_ref, cw_ref, cb_ref, wd_ref, out_ref, acc_ref, act_ref,
                *, tiles_per_seq):
    i = pl.program_id(0)
    x = x_ref[...]
    xe = jnp.concatenate([xh_ref[...], x], axis=0)
    he = _rms(xe, gpre_ref[...])
    row = lax.broadcasted_iota(jnp.int32, (ROW_TILE + HALO, 1), 0)
    seq_start = (i % tiles_per_seq) == 0
    he = jnp.where((row < HALO) & seq_start, 0.0, he).astype(BF16)

    def up(fc):
        c0 = fc * FFN_CHUNK
        return (_mm(he, wup_ref[:, c0:c0 + FFN_CHUNK]),
                _mm(he, wup_ref[:, FFN_HIDDEN + c0:FFN_HIDDEN + c0 + FFN_CHUNK]))

    def conv(u, c0):
        w = cw_ref[:, c0:c0 + FFN_CHUNK]
        out = cb_ref[:, c0:c0 + FFN_CHUNK] + w[0:1] * pltpu.roll(u, 2, 0)
        out = out + w[1:2] * pltpu.roll(u, 1, 0)
        out = out + w[2:3] * u
        return out[HALO:]

    n_chunks = FFN_HIDDEN // FFN_CHUNK
    u_gate, u_val = up(0)
    for fc in range(n_chunks):
        c0 = fc * FFN_CHUNK
        ahead = up(fc + 1) if fc + 1 < n_chunks else None
        group, k = divmod(fc, FFN_GROUP)
        act = jax.nn.gelu(conv(u_gate, c0), approximate=True) * conv(u_val, FFN_HIDDEN + c0)
        act_ref[group % 2, :, k * FFN_CHUNK:(k + 1) * FFN_CHUNK] = act.astype(BF16)
        if k == FFN_GROUP - 1 or ahead is None:
            g0 = group * FFN_GROUP * FFN_CHUNK
            width = (k + 1) * FFN_CHUNK
            part = _mm(act_ref[group % 2, :, :width], wd_ref[g0:g0 + width, :])
            if group == 0:
                acc_ref[...] = part
            else:
                acc_ref[...] += part
        if ahead is not None:
            u_gate, u_val = ahead
    out_ref[...] = x + _rms(acc_ref[...], gpost_ref[...])


def _ffn(x2, gpre, gpost, wup, cw, cb, wd, seq_len):
    t = x2.shape[0]
    row = pl.BlockSpec((ROW_TILE, D_MODEL), lambda i: (i, 0))
    halo = pl.BlockSpec((HALO, D_MODEL), lambda i: (jnp.maximum(i * (ROW_TILE // HALO) - 1, 0), 0))
    return pl.pallas_call(
        functools.partial(_ffn_kernel, tiles_per_seq=seq_len // ROW_TILE),
        grid=(t // ROW_TILE,),
        in_specs=[row, halo, _resident((1, D_MODEL)), _resident((1, D_MODEL)),
                  _resident((D_MODEL, 2 * FFN_HIDDEN)), _resident((CONV_WIDTH, 2 * FFN_HIDDEN)),
                  _resident((1, 2 * FFN_HIDDEN)), _resident((FFN_HIDDEN, D_MODEL))],
        out_specs=row,
        out_shape=jax.ShapeDtypeStruct((t, D_MODEL), F32),
        scratch_shapes=[pltpu.VMEM((ROW_TILE, D_MODEL), F32),
                        pltpu.VMEM((2, ROW_TILE, FFN_GROUP * FFN_CHUNK), BF16)],
        compiler_params=_cparams(1),
        name="ffn",
    )(x2, x2, gpre, gpost, wup, cw, cb, wd)


def _t5_bucket(dist):
    n = jnp.maximum(dist, 0)
    nf = jnp.maximum(n, 1).astype(F32)
    large = MAX_EXACT + (jnp.log(nf / MAX_EXACT) / math.log(REL_MAX_DISTANCE / MAX_EXACT)
                         * (NUM_BUCKETS - MAX_EXACT)).astype(jnp.int32)
    large = jnp.minimum(large, NUM_BUCKETS - 1)
    return jnp.where(n < MAX_EXACT, n, large)


def _bias_tables(rel_bias):
    bm = rel_bias[:, :MOBA_HEADS].T.astype(F32)
    bs = rel_bias[:, MOBA_HEADS:].T.astype(F32)

    def lookup(table, dist):
        bucket = _t5_bucket(dist)[None]
        out = jnp.zeros((table.shape[0],) + dist.shape, F32)
        for bkt in range(NUM_BUCKETS):
            out = jnp.where(bucket == bkt, table[:, bkt][:, None, None], out)
        return out

    kj = jnp.arange(MOBA_BLOCK)[:, None]
    qi = jnp.arange(MOBA_BLOCK)[None, :]
    far = bm[:, NUM_BUCKETS - 1][:, None, None]
    own = jnp.where((qi - kj) >= 0, (lookup(bm, qi - kj) - far) * LOG2E, NEG)
    prev = (lookup(bm, MOBA_BLOCK + qi - kj) - far) * LOG2E
    tab_moba = jnp.stack([prev, own, jnp.full_like(own, NEG)], axis=1)
    kj = jnp.arange(2 * SWA_BLOCK)[:, None]
    qi = jnp.arange(SWA_BLOCK)[None, :]
    dist = SWA_BLOCK + qi - kj
    tab_swa = jnp.where((dist >= 0) & (dist < SWA_WINDOW), lookup(bs, dist) * LOG2E, NEG)
    return tab_moba, tab_swa


def _head_variants(w, heads):
    w3 = w.reshape(w.shape[0], heads, 1, -1)
    z = jnp.zeros_like(w3)
    lo = jnp.concatenate([w3, z], axis=3)
    hi = jnp.concatenate([z, w3], axis=3)
    return jnp.concatenate([lo, hi], axis=2).reshape(w.shape[0], heads * 2 * LANES)


def _proj_weight(w_in):
    o = 0
    parts = {}
    for name, width in (("qm", MOBA_W), ("km", MOBA_W), ("vm", MOBA_W), ("qs", SWA_QW), ("ks", SWA_KVW),
                        ("vs", SWA_KVW), ("qe", MEM_W)):
        parts[name] = w_in[:, o:o + width]
        o += width
    q_scale = LOG2E / math.sqrt(MOBA_HEAD_DIM)
    w = jnp.concatenate([parts["qm"] * q_scale, parts["km"], parts["qs"] * q_scale,
                         _head_variants(parts["ks"], SWA_KV_HEADS), parts["qe"]], axis=1)
    vs3 = parts["vs"].reshape(-1, SWA_KV_HEADS, SWA_HEAD_DIM)
    vs_dup = jnp.concatenate([vs3, vs3], axis=2).reshape(-1, SWA_VT_ROWS)
    w_t = jnp.concatenate([parts["vm"], vs_dup], axis=1).T
    return w.astype(BF16), w_t.astype(BF16), w_in[:, o:].astype(BF16)


def kernel(x, mem, norm_mix_pre, norm_mix_post, norm_ffn_pre, norm_ffn_post, norm_mem, w_in, rel_bias, swa_sinks,
           w_mem_kv, w_branch_moba, w_branch_swa, w_branch_mem, w_out, w_ffn_up, ffn_conv_w, ffn_conv_b,
           w_ffn_down):
    b, s, d = x.shape
    m_len = mem.shape[1]
    assert d == D_MODEL and s % ROW_TILE == 0 and s // MOBA_BLOCK <= 32 and m_len % LANES == 0
    depth = w_in.shape[0]
    tab_moba, tab_swa = _bias_tables(rel_bias)
    x2 = x.reshape(b * s, d).astype(F32)
    vec = lambda v: v.reshape(1, -1).astype(F32)
    nblk = s // MOBA_BLOCK
    r3 = lambda a: a.reshape(b, s, a.shape[-1])
    for l in range(depth):
        w_qkv, w_t, w_gate = _proj_weight(w_in[l])
        qm, km, vt, qs, ks, vst, qe, sel = _proj(x2, vec(norm_mix_pre[l]), w_qkv, w_t, s)
        kme, vmt = _memkv(mem.astype(F32), vec(norm_mem[l]), w_mem_kv[l][:, :MEM_W].astype(BF16),
                          w_mem_kv[l][:, MEM_W:].T.astype(BF16))
        o_mb = _moba(r3(qm), r3(km), vt.reshape(b, nblk, MOBA_W, MOBA_BLOCK), sel, tab_moba)
        sinks = jnp.broadcast_to(swa_sinks[l].astype(F32)[:, None] * LOG2E, (SWA_Q_HEADS, LANES))
        o_sw, o_me = _local(r3(qs), r3(ks), vst.reshape(b, s // SWA_BLOCK, SWA_VT_ROWS, SWA_BLOCK), tab_swa, sinks,
                            r3(qe), kme, vmt)
        w_branch = jnp.stack([w_branch_moba[l], w_branch_swa[l], w_branch_mem[l]]).astype(BF16)
        x2 = _merge(x2, o_mb.reshape(b * s, -1), o_sw.reshape(b * s, -1), o_me.reshape(b * s, -1),
                    vec(norm_mix_pre[l]), vec(norm_mix_post[l]), w_gate, w_branch, w_out[l].astype(BF16))
        x2 = _ffn(x2, vec(norm_ffn_pre[l]), vec(norm_ffn_post[l]), w_ffn_up[l].astype(BF16),
                  ffn_conv_w[l].astype(F32), vec(ffn_conv_b[l]), w_ffn_down[l].astype(BF16), s)
    return x2.reshape(b, s, d).astype(x.dtype)
```

```python
import functools
import math

import jax
import jax.numpy as jnp
from jax import lax
from jax.experimental import pallas as pl
from jax.experimental.pallas import tpu as pltpu

D_MODEL = 1024
MOBA_HEADS = 8
MOBA_HEAD_DIM = 64
MOBA_BLOCK = 256
MOBA_TOPK = 3
SWA_Q_HEADS = 8
SWA_KV_HEADS = 2
SWA_HEAD_DIM = 64
SWA_WINDOW = 128
SWA_BLOCK = 128
MEM_HEADS = 4
MEM_HEAD_DIM = 128
NUM_BUCKETS = 32
MAX_EXACT = NUM_BUCKETS // 2
REL_MAX_DISTANCE = 128
FFN_HIDDEN = 2816
CONV_WIDTH = 3
RMS_EPS = 1e-6

MOBA_W = MOBA_HEADS * MOBA_HEAD_DIM
SWA_QW = SWA_Q_HEADS * SWA_HEAD_DIM
SWA_KVW = SWA_KV_HEADS * SWA_HEAD_DIM
MEM_W = MEM_HEADS * MEM_HEAD_DIM

LANES = 128
NEG = -1e30
VMEM_LIMIT = 56 * 1024 * 1024
ROW_TILE = 512
MOBA_KEYS_PER_STEP = 512
MOBA_STEPS_PER_TRIP = (4, 2)
FFN_CHUNK = 256
FFN_GROUP = 4
HALO = 8
SUM_ROWS = 16
MAX_BLOCKS = 32

BF16 = jnp.bfloat16
F32 = jnp.float32

SWA_K_COLS = 2 * SWA_KV_HEADS * LANES
SWA_VT_ROWS = SWA_KV_HEADS * LANES
OFF_QM = 0
OFF_KM = OFF_QM + MOBA_W
OFF_QS = OFF_KM + MOBA_W
OFF_KS = OFF_QS + SWA_QW
OFF_QE = OFF_KS + SWA_K_COLS
PROJ_W = OFF_QE + MEM_W
LOG2E = math.log2(math.e)


def _mm(a, b):
    return jnp.dot(a, b, preferred_element_type=F32)


def _nt(a, b):
    return lax.dot_general(a, b, (((1,), (1,)), ((), ())), preferred_element_type=F32)


def _rms(xf, g):
    r = lax.rsqrt(jnp.mean(xf * xf, axis=-1, keepdims=True) + RMS_EPS)
    return xf * r * g


def _cparams(n_axes):
    return pltpu.CompilerParams(dimension_semantics=("arbitrary",) * n_axes, vmem_limit_bytes=VMEM_LIMIT)


def _resident(shape):
    nd = len(shape)
    return pl.BlockSpec(shape, lambda *_: (0,) * nd, pipeline_mode=pl.Buffered(1))


def _proj_kernel(x_ref, g_ref, w_ref, wt_ref, qm_ref, km_ref, vt_ref, qs_ref, ks_ref, vst_ref, qe_ref, sel_ref,
                 kmean_ref, *, tiles_per_seq):
    i = pl.program_id(0)
    h = _rms(x_ref[...], g_ref[...]).astype(BF16)

    def seg(c0, n):
        return _mm(h, w_ref[:, c0:c0 + n])

    q = seg(OFF_QM, MOBA_W).astype(BF16)
    qm_ref[...] = q
    k = seg(OFF_KM, MOBA_W)
    km_ref[...] = k.astype(BF16)
    qs_ref[...] = seg(OFF_QS, SWA_QW).astype(BF16)
    ks_ref[...] = seg(OFF_KS, SWA_K_COLS).astype(BF16)
    qe_ref[...] = seg(OFF_QE, MEM_W).astype(BF16)
    vt = _nt(wt_ref[...], h).astype(BF16)
    for r in range(ROW_TILE // SWA_BLOCK):
        vst_ref[r] = vt[MOBA_W:, r * SWA_BLOCK:(r + 1) * SWA_BLOCK]

    @pl.when(i % tiles_per_seq == 0)
    def _():
        kmean_ref[...] = jnp.zeros_like(kmean_ref)

    blocks_per_tile = ROW_TILE // MOBA_BLOCK
    c0 = (i % tiles_per_seq) * blocks_per_tile
    lane = lax.broadcasted_iota(jnp.int32, (1, MOBA_W), 1)
    for r in range(blocks_per_tile):
        rows = slice(r * MOBA_BLOCK, (r + 1) * MOBA_BLOCK)
        vt_ref[r] = vt[:MOBA_W, rows]
        mean = jnp.sum(k[rows], axis=0, keepdims=True) * (1.0 / MOBA_BLOCK)
        for hd in range(MOBA_HEADS):
            in_head = (lane >= hd * MOBA_HEAD_DIM) & (lane < (hd + 1) * MOBA_HEAD_DIM)
            kmean_ref[pl.ds(hd * MAX_BLOCKS + c0 + r, 1), :] = jnp.where(in_head, mean, 0.0)

    km = kmean_ref[...]
    km_hi = km.astype(BF16)
    km_lo = (km - km_hi.astype(F32)).astype(BF16)
    gates = _nt(km_hi, q) + _nt(km_lo, q)
    blk = lax.broadcasted_iota(jnp.int32, (MAX_BLOCKS, ROW_TILE), 0)
    blk_f = blk.astype(F32)
    col = lax.broadcasted_iota(jnp.int32, (MAX_BLOCKS, ROW_TILE), 1)
    q_blk = c0 + sum((col >= r * MOBA_BLOCK).astype(jnp.int32) for r in range(1, blocks_per_tile))
    past = blk < q_blk
    for hd in range(MOBA_HEADS):
        g = jnp.where(past, gates[hd * MAX_BLOCKS:(hd + 1) * MAX_BLOCKS], -jnp.inf)
        sel = blk < 0
        for _ in range(MOBA_TOPK):
            top = jnp.max(g, axis=0, keepdims=True)
            first = jnp.min(jnp.where(g == top, blk_f, 1e9), axis=0, keepdims=True)
            hit = blk_f == first
            sel = sel | hit
            g = jnp.where(hit, -jnp.inf, g)
        sel_ref[0, hd] = jnp.where(sel & past, 1.0, 0.0)


def _proj(x2, g, w, wt, seq_len):
    t = x2.shape[0]
    n_tiles = t // ROW_TILE
    tiles_per_seq = seq_len // ROW_TILE
    bpt = ROW_TILE // MOBA_BLOCK
    spt = ROW_TILE // SWA_BLOCK
    row = lambda n: pl.BlockSpec((ROW_TILE, n), lambda i: (i, 0))
    out_shapes = (
        jax.ShapeDtypeStruct((t, MOBA_W), BF16),
        jax.ShapeDtypeStruct((t, MOBA_W), BF16),
        jax.ShapeDtypeStruct((t // MOBA_BLOCK, MOBA_W, MOBA_BLOCK), BF16),
        jax.ShapeDtypeStruct((t, SWA_QW), BF16),
        jax.ShapeDtypeStruct((t, SWA_K_COLS), BF16),
        jax.ShapeDtypeStruct((t // SWA_BLOCK, SWA_VT_ROWS, SWA_BLOCK), BF16),
        jax.ShapeDtypeStruct((t, MEM_W), BF16),
        jax.ShapeDtypeStruct((t // seq_len, MOBA_HEADS, MAX_BLOCKS, seq_len), F32),
    )
    out_specs = (row(MOBA_W), row(MOBA_W), pl.BlockSpec((bpt, MOBA_W, MOBA_BLOCK), lambda i: (i, 0, 0)),
                 row(SWA_QW), row(SWA_K_COLS), pl.BlockSpec((spt, SWA_VT_ROWS, SWA_BLOCK), lambda i: (i, 0, 0)),
                 row(MEM_W),
                 pl.BlockSpec((1, MOBA_HEADS, MAX_BLOCKS, ROW_TILE),
                              lambda i: (i // tiles_per_seq, 0, 0, i % tiles_per_seq)))
    return pl.pallas_call(
        functools.partial(_proj_kernel, tiles_per_seq=tiles_per_seq),
        grid=(n_tiles,),
        in_specs=[row(D_MODEL), _resident((1, D_MODEL)), _resident((D_MODEL, PROJ_W)),
                  _resident((MOBA_W + SWA_VT_ROWS, D_MODEL))],
        out_specs=out_specs,
        out_shape=out_shapes,
        scratch_shapes=[pltpu.VMEM((MOBA_HEADS * MAX_BLOCKS, MOBA_W), F32)],
        compiler_params=_cparams(1),
        name="proj",
    )(x2, g, w, wt)


def _memkv_kernel(m_ref, g_ref, wk_ref, wvt_ref, k_ref, vt_ref):
    h = _rms(m_ref[0], g_ref[...]).astype(BF16)
    k_ref[0] = _mm(h, wk_ref[...]).astype(BF16)
    vt_ref[0] = _nt(wvt_ref[...], h).astype(BF16)


def _memkv(mem, g, wk, wvt):
    b, m_len, _ = mem.shape
    return pl.pallas_call(
        _memkv_kernel,
        grid=(b,),
        in_specs=[pl.BlockSpec((1, m_len, D_MODEL), lambda i: (i, 0, 0)), _resident((1, D_MODEL)),
                  _resident((D_MODEL, MEM_W)), _resident((MEM_W, D_MODEL))],
        out_specs=(pl.BlockSpec((1, m_len, MEM_W), lambda i: (i, 0, 0)),
                   pl.BlockSpec((1, MEM_W, m_len), lambda i: (i, 0, 0))),
        out_shape=(jax.ShapeDtypeStruct((b, m_len, MEM_W), BF16), jax.ShapeDtypeStruct((b, MEM_W, m_len), BF16)),
        compiler_params=_cparams(1),
        name="memkv",
    )(mem, g, wk, wvt)


def _moba_kernel(q_ref, k_ref, vt_ref, sel_ref, tab_ref, o_ref, mfar_ref, mtail_ref, s_ref):
    c = pl.program_id(2)
    mb = MOBA_BLOCK
    kb = MOBA_KEYS_PER_STEP
    per = kb // mb
    hd = MOBA_HEAD_DIM
    lane_q = lax.broadcasted_iota(jnp.int32, (mb, LANES), 1)
    blk = lax.broadcasted_iota(jnp.int32, (MAX_BLOCKS, mb), 0)
    tb = jnp.maximum(c - 1, 0)
    kt = pl.multiple_of(tb * mb, mb)
    left = jnp.where(c == 0, 1, 0)
    right = jnp.where(c == 0, 2, 1)

    ones = jnp.ones((SUM_ROWS, mb), BF16)

    def values(block):
        return jnp.concatenate([vt_ref[0, block], ones], axis=0)

    qf = q_ref[0].astype(F32)
    k_tail = k_ref[0, pl.ds(kt, 2 * mb), :]
    q_pad = []
    state = []
    tail_max = []
    for a in range(2):
        qa = jnp.where((lane_q >= a * hd) & (lane_q < (a + 1) * hd), qf, 0.0).astype(BF16)
        q_pad.append(qa)
        sel = sel_ref[0, a] > 0.5
        mfar_ref[a] = jnp.where(sel & (blk <= c - 2), 0.0, NEG)
        mtail_ref[a] = jnp.where((blk == c) | (sel & (blk == c - 1)), 0.0, NEG)

        s = _nt(k_tail, qa)
        s = jnp.concatenate([s[:mb] + tab_ref[a, left] + mtail_ref[a, pl.ds(tb, 1), :],
                             s[mb:] + tab_ref[a, right] + mtail_ref[a, pl.ds(tb + 1, 1), :]], axis=0)
        s_ref[0, a] = s
        tail_max.append(jnp.max(s, axis=0, keepdims=True))
        state.append((jnp.full((1, mb), -jnp.inf, F32), jnp.zeros((LANES + SUM_ROWS, mb), F32)))

    n_far = jnp.where(c >= 2, ((c - 1) * mb + kb - 1) // kb, 0)
    last_step = k_ref.shape[1] // kb - 1

    def far_scores(j, slot):
        kj = k_ref[0, pl.ds(pl.multiple_of(j * kb, kb), kb), :]
        out = []
        for a in range(2):
            s = _nt(kj, q_pad[a])
            s = jnp.concatenate([s[t * mb:(t + 1) * mb] + mfar_ref[a, pl.ds(j * per + t, 1), :]
                                 for t in range(per)], axis=0)
            s_ref[slot, a] = s
            out.append(jnp.max(s, axis=0, keepdims=True))
        return tuple(out)

    def absorb(stats, slot, s_max, block):
        out = []
        for a in range(2):
            m, acc = stats[a]
            m_new = jnp.maximum(m, s_max[a])
            pb = jnp.exp2(s_ref[slot, a] - m_new).astype(BF16)
            acc = jnp.exp2(m - m_new) * acc
            for t in range(per):
                acc = acc + _mm(values(block + t), pb[t * mb:(t + 1) * mb])
            out.append((m_new, acc))
        return tuple(out)

    def steps(count, first):
        def trip(i, carry):
            s_max, stats, block = carry
            for u in range(count):
                j = first + count * i + u
                ahead = far_scores(jnp.minimum(j, last_step), (u + 1) % 2)
                stats = absorb(stats, u % 2, s_max, block)
                s_max, block = ahead, j * per
            return s_max, stats, block
        return trip

    def last_step_alone(_, carry):
        s_max, stats, block = carry
        return s_max, absorb(stats, 0, s_max, block), block

    n_steps = n_far + 1
    carry = (tuple(tail_max), tuple(state), tb)
    done = 0
    for count in MOBA_STEPS_PER_TRIP:
        trips = (n_steps - done) // count
        carry = lax.fori_loop(0, trips, steps(count, done), carry)
        done = done + trips * count
    _, ((_, acc_a), (_, acc_b)), _ = lax.fori_loop(0, n_steps - done, last_step_alone, carry)
    row = lax.broadcasted_iota(jnp.int32, (LANES, mb), 0)
    o_t = jnp.where(row < hd, acc_a[:LANES] / acc_a[LANES:LANES + 1], acc_b[:LANES] / acc_b[LANES:LANES + 1])
    o_ref[0] = o_t.T.astype(BF16)


def _moba(qm, km, vt, sel, tab):
    b, s, _ = qm.shape
    pairs = MOBA_HEADS // 2
    nblk = s // MOBA_BLOCK
    grid = (b, pairs, nblk)
    return pl.pallas_call(
        _moba_kernel,
        grid=grid,
        in_specs=[
            pl.BlockSpec((1, MOBA_BLOCK, LANES), lambda bi, hp, c: (bi, c, hp)),
            pl.BlockSpec((1, s, LANES), lambda bi, hp, c: (bi, 0, hp)),
            pl.BlockSpec((1, nblk, LANES, MOBA_BLOCK), lambda bi, hp, c: (bi, 0, hp, 0)),
            pl.BlockSpec((1, 2, MAX_BLOCKS, MOBA_BLOCK), lambda bi, hp, c: (bi, hp, 0, c)),
            pl.BlockSpec((2, 3, MOBA_BLOCK, MOBA_BLOCK), lambda bi, hp, c: (hp, 0, 0, 0)),
        ],
        out_specs=pl.BlockSpec((1, MOBA_BLOCK, LANES), lambda bi, hp, c: (bi, c, hp)),
        out_shape=jax.ShapeDtypeStruct((b, s, MOBA_W), BF16),
        scratch_shapes=[pltpu.VMEM((2, MAX_BLOCKS, MOBA_BLOCK), F32), pltpu.VMEM((2, MAX_BLOCKS, MOBA_BLOCK), F32),
                        pltpu.VMEM((2, 2, MOBA_KEYS_PER_STEP, MOBA_BLOCK), F32)],
        compiler_params=_cparams(3),
        name="moba",
    )(qm, km, vt, sel, tab)


def _local_kernel(qs_ref, ks_ref, kh_ref, vst_ref, vsh_ref, tab_ref, sink_ref, qe_ref, kme_ref, vmt_ref,
                  osw_ref, ome_ref):
    i = pl.program_id(1)
    qb = SWA_BLOCK
    pairs_per_kv = SWA_Q_HEADS // SWA_KV_HEADS // 2
    no_prev = jnp.where(i == 0, NEG, 0.0)
    row = lax.broadcasted_iota(jnp.int32, (LANES, qb), 0)
    for blk in range(ROW_TILE // qb):
        rows = slice(blk * qb, (blk + 1) * qb)
        for hk in range(SWA_KV_HEADS):
            vrows = slice(hk * LANES, (hk + 1) * LANES)
            v_prev = vsh_ref[0, 0, vrows] if blk == 0 else vst_ref[0, blk - 1, vrows]
            vv = jnp.concatenate([v_prev, vst_ref[0, blk, vrows]], axis=1)
            for jj in range(pairs_per_kv):
                j = hk * pairs_per_kv + jj
                qp = qs_ref[0, rows, j * LANES:(j + 1) * LANES]
                outs = []
                for par in range(2):
                    h = 2 * j + par
                    kc = slice((2 * hk + par) * LANES, (2 * hk + par + 1) * LANES)
                    if blk == 0:
                        kk = jnp.concatenate([kh_ref[0, :, kc], ks_ref[0, 0:qb, kc]], axis=0)
                    else:
                        kk = ks_ref[0, (blk - 1) * qb:(blk + 1) * qb, kc]
                    s = _nt(kk, qp) + tab_ref[h]
                    if blk == 0:
                        s = jnp.concatenate([s[:qb] + no_prev, s[qb:]], axis=0)
                    sink = sink_ref[h:h + 1, :]
                    m = jnp.maximum(jnp.max(s, axis=0, keepdims=True), sink)
                    p = jnp.exp2(s - m)
                    den = jnp.sum(p, axis=0, keepdims=True) + jnp.exp2(sink - m)
                    outs.append(_mm(vv, p.astype(BF16)) / den)
                o_t = jnp.where(row < SWA_HEAD_DIM, outs[0], outs[1])
                osw_ref[0, rows, j * LANES:(j + 1) * LANES] = o_t.T.astype(BF16)

    scale = LOG2E / math.sqrt(MEM_HEAD_DIM)
    for hd in range(MEM_HEADS):
        cols = slice(hd * MEM_HEAD_DIM, (hd + 1) * MEM_HEAD_DIM)
        kk = kme_ref[0, :, cols]
        vv = vmt_ref[0, cols, :]
        for blk in range(ROW_TILE // qb):
            rows = slice(blk * qb, (blk + 1) * qb)
            s = _nt(kk, qe_ref[0, rows, cols]) * scale
            m = jnp.max(s, axis=0, keepdims=True)
            p = jnp.exp2(s - m)
            den = jnp.sum(p, axis=0, keepdims=True)
            ome_ref[0, rows, cols] = (_mm(vv, p.astype(BF16)) / den).T.astype(BF16)


def _local(qs, ks, vst, tab, sinks, qe, kme, vmt):
    b, s, _ = qs.shape
    per_tile = ROW_TILE // SWA_BLOCK
    m_len = kme.shape[1]
    tile = lambda n: pl.BlockSpec((1, ROW_TILE, n), lambda bi, i: (bi, i, 0))
    prev_block = lambda bi, i: (bi, jnp.maximum(i * per_tile - 1, 0), 0)
    return pl.pallas_call(
        _local_kernel,
        grid=(b, s // ROW_TILE),
        in_specs=[
            tile(SWA_QW),
            tile(SWA_K_COLS), pl.BlockSpec((1, SWA_BLOCK, SWA_K_COLS), prev_block),
            pl.BlockSpec((1, per_tile, SWA_VT_ROWS, SWA_BLOCK), lambda bi, i: (bi, i, 0, 0)),
            pl.BlockSpec((1, 1, SWA_VT_ROWS, SWA_BLOCK), lambda bi, i: prev_block(bi, i) + (0,)),
            _resident((SWA_Q_HEADS, 2 * SWA_BLOCK, SWA_BLOCK)),
            _resident((SWA_Q_HEADS, LANES)),
            tile(MEM_W),
            pl.BlockSpec((1, m_len, MEM_W), lambda bi, i: (bi, 0, 0)),
            pl.BlockSpec((1, MEM_W, m_len), lambda bi, i: (bi, 0, 0)),
        ],
        out_specs=(tile(SWA_QW), tile(MEM_W)),
        out_shape=(jax.ShapeDtypeStruct((b, s, SWA_QW), BF16), jax.ShapeDtypeStruct((b, s, MEM_W), BF16)),
        compiler_params=_cparams(2),
        name="local",
    )(qs, ks, ks, vst, vst, tab, sinks, qe, kme, vmt)


def _merge_kernel(x_ref, omb_ref, osw_ref, ome_ref, gpre_ref, gpost_ref, wg_ref, wb_ref, wo_ref, out_ref):
    x = x_ref[...]
    h = _rms(x, gpre_ref[...]).astype(BF16)
    branch = (omb_ref[...], osw_ref[...], ome_ref[...])
    half = D_MODEL // 2
    parts = []
    for nc in range(2):
        acc = None
        for br in range(3):
            c0 = br * D_MODEL + nc * half
            gate = jax.nn.sigmoid(_mm(h, wg_ref[:, c0:c0 + half]))
            term = gate * _mm(branch[br], wb_ref[br, :, nc * half:(nc + 1) * half])
            acc = term if acc is None else acc + term
        parts.append(acc.astype(BF16))
    y = _mm(jnp.concatenate(parts, axis=1), wo_ref[...])
    out_ref[...] = x + _rms(y, gpost_ref[...])


def _merge(x2, omb, osw, ome, gpre, gpost, wg, wb, wo):
    t = x2.shape[0]
    row = lambda n: pl.BlockSpec((ROW_TILE, n), lambda i: (i, 0))
    return pl.pallas_call(
        _merge_kernel,
        grid=(t // ROW_TILE,),
        in_specs=[row(D_MODEL), row(MOBA_W), row(SWA_QW), row(MEM_W),
                  _resident((1, D_MODEL)), _resident((1, D_MODEL)),
                  _resident((D_MODEL, 3 * D_MODEL)), _resident((3, MOBA_W, D_MODEL)),
                  _resident((D_MODEL, D_MODEL))],
        out_specs=row(D_MODEL),
        out_shape=jax.ShapeDtypeStruct((t, D_MODEL), F32),
        compiler_params=_cparams(1),
        name="merge",
    )(x2, omb, osw, ome, gpre, gpost, wg, wb, wo)


def _ffn_kernel(x_ref, xh_ref, gpre_ref, gpost_ref, wup_ref, cw_ref, cb_ref, wd_ref, out_ref, acc_ref, act_ref,
                *, tiles_per_seq):
    i = pl.program_id(0)
    x = x_ref[...]
    xe = jnp.concatenate([xh_ref[...], x], axis=0)
    he = _rms(xe, gpre_ref[...])
    row = lax.broadcasted_iota(jnp.int32, (ROW_TILE + HALO, 1), 0)
    seq_start = (i % tiles_per_seq) == 0
    he = jnp.where((row < HALO) & seq_start, 0.0, he).astype(BF16)

    def up(fc):
        c0 = fc * FFN_CHUNK
        return (_mm(he, wup_ref[:, c0:c0 + FFN_CHUNK]),
                _mm(he, wup_ref[:, FFN_HIDDEN + c0:FFN_HIDDEN + c0 + FFN_CHUNK]))

    def conv(u, c0):
        w = cw_ref[:, c0:c0 + FFN_CHUNK]
        out = cb_ref[:, c0:c0 + FFN_CHUNK] + w[0:1] * pltpu.roll(u, 2, 0)
        out = out + w[1:2] * pltpu.roll(u, 1, 0)
        out = out + w[2:3] * u
        return out[HALO:]

    n_chunks = FFN_HIDDEN // FFN_CHUNK
    u_gate, u_val = up(0)
    for fc in range(n_chunks):
        c0 = fc * FFN_CHUNK
        ahead = up(fc + 1) if fc + 1 < n_chunks else None
        group, k = divmod(fc, FFN_GROUP)
        act = jax.nn.gelu(conv(u_gate, c0), approximate=True) * conv(u_val, FFN_HIDDEN + c0)
        act_ref[group % 2, :, k * FFN_CHUNK:(k + 1) * FFN_CHUNK] = act.astype(BF16)
        if k == FFN_GROUP - 1 or ahead is None:
            g0 = group * FFN_GROUP * FFN_CHUNK
            width = (k + 1) * FFN_CHUNK
            part = _mm(act_ref[group % 2, :, :width], wd_ref[g0:g0 + width, :])
            if group == 0:
                acc_ref[...] = part
            else:
                acc_ref[...] += part
        if ahead is not None:
            u_gate, u_val = ahead
    out_ref[...] = x + _rms(acc_ref[...], gpost_ref[...])


def _ffn(x2, gpre, gpost, wup, cw, cb, wd, seq_len):
    t = x2.shape[0]
    row = pl.BlockSpec((ROW_TILE, D_MODEL), lambda i: (i, 0))
    halo = pl.BlockSpec((HALO, D_MODEL), lambda i: (jnp.maximum(i * (ROW_TILE // HALO) - 1, 0), 0))
    return pl.pallas_call(
        functools.partial(_ffn_kernel, tiles_per_seq=seq_len // ROW_TILE),
        grid=(t // ROW_TILE,),
        in_specs=[row, halo, _resident((1, D_MODEL)), _resident((1, D_MODEL)),
                  _resident((D_MODEL, 2 * FFN_HIDDEN)), _resident((CONV_WIDTH, 2 * FFN_HIDDEN)),
                  _resident((1, 2 * FFN_HIDDEN)), _resident((FFN_HIDDEN, D_MODEL))],
        out_specs=row,
        out_shape=jax.ShapeDtypeStruct((t, D_MODEL), F32),
        scratch_shapes=[pltpu.VMEM((ROW_TILE, D_MODEL), F32),
                        pltpu.VMEM((2, ROW_TILE, FFN_GROUP * FFN_CHUNK), BF16)],
        compiler_params=_cparams(1),
        name="ffn",
    )(x2, x2, gpre, gpost, wup, cw, cb, wd)


def _t5_bucket(dist):
    n = jnp.maximum(dist, 0)
    nf = jnp.maximum(n, 1).astype(F32)
    large = MAX_EXACT + (jnp.log(nf / MAX_EXACT) / math.log(REL_MAX_DISTANCE / MAX_EXACT)
                         * (NUM_BUCKETS - MAX_EXACT)).astype(jnp.int32)
    large = jnp.minimum(large, NUM_BUCKETS - 1)
    return jnp.where(n < MAX_EXACT, n, large)


def _bias_tables(rel_bias):
    bm = rel_bias[:, :MOBA_HEADS].T.astype(F32)
    bs = rel_bias[:, MOBA_HEADS:].T.astype(F32)

    def lookup(table, dist):
        bucket = _t5_bucket(dist)[None]
        out = jnp.zeros((table.shape[0],) + dist.shape, F32)
        for bkt in range(NUM_BUCKETS):
            out = jnp.where(bucket == bkt, table[:, bkt][:, None, None], out)
        return out

    kj = jnp.arange(MOBA_BLOCK)[:, None]
    qi = jnp.arange(MOBA_BLOCK)[None, :]
    far = bm[:, NUM_BUCKETS - 1][:, None, None]
    own = jnp.where((qi - kj) >= 0, (lookup(bm, qi - kj) - far) * LOG2E, NEG)
    prev = (lookup(bm, MOBA_BLOCK + qi - kj) - far) * LOG2E
    tab_moba = jnp.stack([prev, own, jnp.full_like(own, NEG)], axis=1)
    kj = jnp.arange(2 * SWA_BLOCK)[:, None]
    qi = jnp.arange(SWA_BLOCK)[None, :]
    dist = SWA_BLOCK + qi - kj
    tab_swa = jnp.where((dist >= 0) & (dist < SWA_WINDOW), lookup(bs, dist) * LOG2E, NEG)
    return tab_moba, tab_swa


def _head_variants(w, heads):
    w3 = w.reshape(w.shape[0], heads, 1, -1)
    z = jnp.zeros_like(w3)
    lo = jnp.concatenate([w3, z], axis=3)
    hi = jnp.concatenate([z, w3], axis=3)
    return jnp.concatenate([lo, hi], axis=2).reshape(w.shape[0], heads * 2 * LANES)


def _proj_weight(w_in):
    o = 0
    parts = {}
    for name, width in (("qm", MOBA_W), ("km", MOBA_W), ("vm", MOBA_W), ("qs", SWA_QW), ("ks", SWA_KVW),
                        ("vs", SWA_KVW), ("qe", MEM_W)):
        parts[name] = w_in[:, o:o + width]
        o += width
    q_scale = LOG2E / math.sqrt(MOBA_HEAD_DIM)
    w = jnp.concatenate([parts["qm"] * q_scale, parts["km"], parts["qs"] * q_scale,
                         _head_variants(parts["ks"], SWA_KV_HEADS), parts["qe"]], axis=1)
    vs3 = parts["vs"].reshape(-1, SWA_KV_HEADS, SWA_HEAD_DIM)
    vs_dup = jnp.concatenate([vs3, vs3], axis=2).reshape(-1, SWA_VT_ROWS)
    w_t = jnp.concatenate([parts["vm"], vs_dup], axis=1).T
    return w.astype(BF16), w_t.astype(BF16), w_in[:, o:].astype(BF16)


def kernel(x, mem, norm_mix_pre, norm_mix_post, norm_ffn_pre, norm_ffn_post, norm_mem, w_in, rel_bias, swa_sinks,
           w_mem_kv, w_branch_moba, w_branch_swa, w_branch_mem, w_out, w_ffn_up, ffn_conv_w, ffn_conv_b,
           w_ffn_down):
    b, s, d = x.shape
    m_len = mem.shape[1]
    assert d == D_MODEL and s % ROW_TILE == 0 and s // MOBA_BLOCK <= 32 and m_len % LANES == 0
    depth = w_in.shape[0]
    tab_moba, tab_swa = _bias_tables(rel_bias)
    x2 = x.reshape(b * s, d).astype(F32)
    vec = lambda v: v.reshape(1, -1).astype(F32)
    nblk = s // MOBA_BLOCK
    r3 = lambda a: a.reshape(b, s, a.shape[-1])
    for l in range(depth):
        w_qkv, w_t, w_gate = _proj_weight(w_in[l])
        qm, km, vt, qs, ks, vst, qe, sel = _proj(x2, vec(norm_mix_pre[l]), w_qkv, w_t, s)
        kme, vmt = _memkv(mem.astype(F32), vec(norm_mem[l]), w_mem_kv[l][:, :MEM_W].astype(BF16),
                          w_mem_kv[l][:, MEM_W:].T.astype(BF16))
        o_mb = _moba(r3(qm), r3(km), vt.reshape(b, nblk, MOBA_W, MOBA_BLOCK), sel, tab_moba)
        sinks = jnp.broadcast_to(swa_sinks[l].astype(F32)[:, None] * LOG2E, (SWA_Q_HEADS, LANES))
        o_sw, o_me = _local(r3(qs), r3(ks), vst.reshape(b, s // SWA_BLOCK, SWA_VT_ROWS, SWA_BLOCK), tab_swa, sinks,
                            r3(qe), kme, vmt)
        w_branch = jnp.stack([w_branch_moba[l], w_branch_swa[l], w_branch_mem[l]]).astype(BF16)
        x2 = _merge(x2, o_mb.reshape(b * s, -1), o_sw.reshape(b * s, -1), o_me.reshape(b * s, -1),
                    vec(norm_mix_pre[l]), vec(norm_mix_post[l]), w_gate, w_branch, w_out[l].astype(BF16))
        x2 = _ffn(x2, vec(norm_ffn_pre[l]), vec(norm_ffn_post[l]), w_ffn_up[l].astype(BF16),
                  ffn_conv_w[l].astype(F32), vec(ffn_conv_b[l]), w_ffn_down[l].astype(BF16), s)
    return x2.reshape(b, s, d).astype(x.dtype)
```

```python
import functools
import math

import jax
import jax.numpy as jnp
from jax import lax
from jax.experimental import pallas as pl
from jax.experimental.pallas import tpu as pltpu

D_MODEL = 1024
MOBA_HEADS = 8
MOBA_HEAD_DIM = 64
MOBA_BLOCK = 256
MOBA_TOPK = 3
SWA_Q_HEADS = 8
SWA_KV_HEADS = 2
SWA_HEAD_DIM = 64
SWA_WINDOW = 128
SWA_BLOCK = 128
MEM_HEADS = 4
MEM_HEAD_DIM = 128
NUM_BUCKETS = 32
MAX_EXACT = NUM_BUCKETS // 2
REL_MAX_DISTANCE = 128
FFN_HIDDEN = 2816
CONV_WIDTH = 3
RMS_EPS = 1e-6

MOBA_W = MOBA_HEADS * MOBA_HEAD_DIM
SWA_QW = SWA_Q_HEADS * SWA_HEAD_DIM
SWA_KVW = SWA_KV_HEADS * SWA_HEAD_DIM
MEM_W = MEM_HEADS * MEM_HEAD_DIM

LANES = 128
NEG = -1e30
VMEM_LIMIT = 56 * 1024 * 1024
ROW_TILE = 512
MOBA_KEYS_PER_STEP = 512
MOBA_STEPS_PER_TRIP = (8, 4, 2)
FFN_CHUNK = 256
FFN_GROUP = 4
HALO = 8
SUM_ROWS = 16
MAX_BLOCKS = 32

BF16 = jnp.bfloat16
F32 = jnp.float32

SWA_K_COLS = 2 * SWA_KV_HEADS * LANES
SWA_VT_ROWS = SWA_KV_HEADS * LANES
OFF_QM = 0
OFF_KM = OFF_QM + MOBA_W
OFF_QS = OFF_KM + MOBA_W
OFF_KS = OFF_QS + SWA_QW
OFF_QE = OFF_KS + SWA_K_COLS
PROJ_W = OFF_QE + MEM_W
LOG2E = math.log2(math.e)


def _mm(a, b):
    return jnp.dot(a, b, preferred_element_type=F32)


def _nt(a, b):
    return lax.dot_general(a, b, (((1,), (1,)), ((), ())), preferred_element_type=F32)


def _rms(xf, g):
    r = lax.rsqrt(jnp.mean(xf * xf, axis=-1, keepdims=True) + RMS_EPS)
    return xf * r * g


def _cparams(n_axes):
    return pltpu.CompilerParams(dimension_semantics=("arbitrary",) * n_axes, vmem_limit_bytes=VMEM_LIMIT)


def _resident(shape):
    nd = len(shape)
    return pl.BlockSpec(shape, lambda *_: (0,) * nd, pipeline_mode=pl.Buffered(1))


def _proj_kernel(x_ref, g_ref, w_ref, wt_ref, qm_ref, km_ref, vt_ref, qs_ref, ks_ref, vst_ref, qe_ref, sel_ref,
                 kmean_ref, *, tiles_per_seq):
    i = pl.program_id(0)
    h = _rms(x_ref[...], g_ref[...]).astype(BF16)

    def seg(c0, n):
        return _mm(h, w_ref[:, c0:c0 + n])

    q = seg(OFF_QM, MOBA_W).astype(BF16)
    qm_ref[...] = q
    k = seg(OFF_KM, MOBA_W)
    km_ref[...] = k.astype(BF16)
    qs_ref[...] = seg(OFF_QS, SWA_QW).astype(BF16)
    ks_ref[...] = seg(OFF_KS, SWA_K_COLS).astype(BF16)
    qe_ref[...] = seg(OFF_QE, MEM_W).astype(BF16)
    vt = _nt(wt_ref[...], h).astype(BF16)
    for r in range(ROW_TILE // SWA_BLOCK):
        vst_ref[r] = vt[MOBA_W:, r * SWA_BLOCK:(r + 1) * SWA_BLOCK]

    @pl.when(i % tiles_per_seq == 0)
    def _():
        kmean_ref[...] = jnp.zeros_like(kmean_ref)

    blocks_per_tile = ROW_TILE // MOBA_BLOCK
    c0 = (i % tiles_per_seq) * blocks_per_tile
    lane = lax.broadcasted_iota(jnp.int32, (1, MOBA_W), 1)
    for r in range(blocks_per_tile):
        rows = slice(r * MOBA_BLOCK, (r + 1) * MOBA_BLOCK)
        vt_ref[r] = vt[:MOBA_W, rows]
        mean = jnp.sum(k[rows], axis=0, keepdims=True) * (1.0 / MOBA_BLOCK)
        for hd in range(MOBA_HEADS):
            in_head = (lane >= hd * MOBA_HEAD_DIM) & (lane < (hd + 1) * MOBA_HEAD_DIM)
            kmean_ref[pl.ds(hd * MAX_BLOCKS + c0 + r, 1), :] = jnp.where(in_head, mean, 0.0)

    km = kmean_ref[...]
    km_hi = km.astype(BF16)
    km_lo = (km - km_hi.astype(F32)).astype(BF16)
    gates = _nt(km_hi, q) + _nt(km_lo, q)
    blk = lax.broadcasted_iota(jnp.int32, (MAX_BLOCKS, ROW_TILE), 0)
    blk_f = blk.astype(F32)
    col = lax.broadcasted_iota(jnp.int32, (MAX_BLOCKS, ROW_TILE), 1)
    q_blk = c0 + sum((col >= r * MOBA_BLOCK).astype(jnp.int32) for r in range(1, blocks_per_tile))
    past = blk < q_blk
    for hd in range(MOBA_HEADS):
        g = jnp.where(past, gates[hd * MAX_BLOCKS:(hd + 1) * MAX_BLOCKS], -jnp.inf)
        sel = blk < 0
        for _ in range(MOBA_TOPK):
            top = jnp.max(g, axis=0, keepdims=True)
            first = jnp.min(jnp.where(g == top, blk_f, 1e9), axis=0, keepdims=True)
            hit = blk_f == first
            sel = sel | hit
            g = jnp.where(hit, -jnp.inf, g)
        sel_ref[0, hd] = jnp.where(sel & past, 1.0, 0.0)


def _proj(x2, g, w, wt, seq_len):
    t = x2.shape[0]
    n_tiles = t // ROW_TILE
    tiles_per_seq = seq_len // ROW_TILE
    bpt = ROW_TILE // MOBA_BLOCK
    spt = ROW_TILE // SWA_BLOCK
    row = lambda n: pl.BlockSpec((ROW_TILE, n), lambda i: (i, 0))
    out_shapes = (
        jax.ShapeDtypeStruct((t, MOBA_W), BF16),
        jax.ShapeDtypeStruct((t, MOBA_W), BF16),
        jax.ShapeDtypeStruct((t // MOBA_BLOCK, MOBA_W, MOBA_BLOCK), BF16),
        jax.ShapeDtypeStruct((t, SWA_QW), BF16),
        jax.ShapeDtypeStruct((t, SWA_K_COLS), BF16),
        jax.ShapeDtypeStruct((t // SWA_BLOCK, SWA_VT_ROWS, SWA_BLOCK), BF16),
        jax.ShapeDtypeStruct((t, MEM_W), BF16),
        jax.ShapeDtypeStruct((t // seq_len, MOBA_HEADS, MAX_BLOCKS, seq_len), F32),
    )
    out_specs = (row(MOBA_W), row(MOBA_W), pl.BlockSpec((bpt, MOBA_W, MOBA_BLOCK), lambda i: (i, 0, 0)),
                 row(SWA_QW), row(SWA_K_COLS), pl.BlockSpec((spt, SWA_VT_ROWS, SWA_BLOCK), lambda i: (i, 0, 0)),
                 row(MEM_W),
                 pl.BlockSpec((1, MOBA_HEADS, MAX_BLOCKS, ROW_TILE),
                              lambda i: (i // tiles_per_seq, 0, 0, i % tiles_per_seq)))
    return pl.pallas_call(
        functools.partial(_proj_kernel, tiles_per_seq=tiles_per_seq),
        grid=(n_tiles,),
        in_specs=[row(D_MODEL), _resident((1, D_MODEL)), _resident((D_MODEL, PROJ_W)),
                  _resident((MOBA_W + SWA_VT_ROWS, D_MODEL))],
        out_specs=out_specs,
        out_shape=out_shapes,
        scratch_shapes=[pltpu.VMEM((MOBA_HEADS * MAX_BLOCKS, MOBA_W), F32)],
        compiler_params=_cparams(1),
        name="proj",
    )(x2, g, w, wt)


def _memkv_kernel(m_ref, g_ref, wk_ref, wvt_ref, k_ref, vt_ref):
    h = _rms(m_ref[0], g_ref[...]).astype(BF16)
    k_ref[0] = _mm(h, wk_ref[...]).astype(BF16)
    vt_ref[0] = _nt(wvt_ref[...], h).astype(BF16)


def _memkv(mem, g, wk, wvt):
    b, m_len, _ = mem.shape
    return pl.pallas_call(
        _memkv_kernel,
        grid=(b,),
        in_specs=[pl.BlockSpec((1, m_len, D_MODEL), lambda i: (i, 0, 0)), _resident((1, D_MODEL)),
                  _resident((D_MODEL, MEM_W)), _resident((MEM_W, D_MODEL))],
        out_specs=(pl.BlockSpec((1, m_len, MEM_W), lambda i: (i, 0, 0)),
                   pl.BlockSpec((1, MEM_W, m_len), lambda i: (i, 0, 0))),
        out_shape=(jax.ShapeDtypeStruct((b, m_len, MEM_W), BF16), jax.ShapeDtypeStruct((b, MEM_W, m_len), BF16)),
        compiler_params=_cparams(1),
        name="memkv",
    )(mem, g, wk, wvt)


def _moba_kernel(q_ref, k_ref, vt_ref, sel_ref, tab_ref, o_ref, mfar_ref, mtail_ref, s_ref):
    c = pl.program_id(2)
    mb = MOBA_BLOCK
    kb = MOBA_KEYS_PER_STEP
    per = kb // mb
    hd = MOBA_HEAD_DIM
    lane_q = lax.broadcasted_iota(jnp.int32, (mb, LANES), 1)
    blk = lax.broadcasted_iota(jnp.int32, (MAX_BLOCKS, mb), 0)
    tb = jnp.maximum(c - 1, 0)
    kt = pl.multiple_of(tb * mb, mb)
    left = jnp.where(c == 0, 1, 0)
    right = jnp.where(c == 0, 2, 1)

    ones = jnp.ones((SUM_ROWS, mb), BF16)

    def values(block, a):
        return jnp.concatenate([vt_ref[0, block, a * hd:(a + 1) * hd], ones], axis=0)

    qf = q_ref[0].astype(F32)
    k_tail = k_ref[0, pl.ds(kt, 2 * mb), :]
    q_pad = []
    state = []
    tail_max = []
    for a in range(2):
        qa = jnp.where((lane_q >= a * hd) & (lane_q < (a + 1) * hd), qf, 0.0).astype(BF16)
        q_pad.append(qa)
        sel = sel_ref[0, a] > 0.5
        mfar_ref[a] = jnp.where(sel & (blk <= c - 2), 0.0, NEG)
        mtail_ref[a] = jnp.where((blk == c) | (sel & (blk == c - 1)), 0.0, NEG)

        s = _nt(k_tail, qa)
        s = jnp.concatenate([s[:mb] + tab_ref[a, left] + mtail_ref[a, pl.ds(tb, 1), :],
                             s[mb:] + tab_ref[a, right] + mtail_ref[a, pl.ds(tb + 1, 1), :]], axis=0)
        s_ref[0, a] = s
        tail_max.append(jnp.max(s, axis=0, keepdims=True))
        state.append((jnp.full((1, mb), -jnp.inf, F32), jnp.zeros((hd + SUM_ROWS, mb), F32)))

    n_far = jnp.where(c >= 2, ((c - 1) * mb + kb - 1) // kb, 0)
    last_step = k_ref.shape[1] // kb - 1

    def far_scores(j, slot):
        kj = k_ref[0, pl.ds(pl.multiple_of(j * kb, kb), kb), :]
        out = []
        for a in range(2):
            s = _nt(kj, q_pad[a])
            s = jnp.concatenate([s[t * mb:(t + 1) * mb] + mfar_ref[a, pl.ds(j * per + t, 1), :]
                                 for t in range(per)], axis=0)
            s_ref[slot, a] = s
            out.append(jnp.max(s, axis=0, keepdims=True))
        return tuple(out)

    def absorb(stats, slot, s_max, block):
        out = []
        for a in range(2):
            m, acc = stats[a]
            m_new = jnp.maximum(m, s_max[a])
            pb = jnp.exp2(s_ref[slot, a] - m_new).astype(BF16)
            acc = jnp.exp2(m - m_new) * acc
            for t in range(per):
                acc = acc + _mm(values(block + t, a), pb[t * mb:(t + 1) * mb])
            out.append((m_new, acc))
        return tuple(out)

    def steps(count, first):
        def trip(i, carry):
            s_max, stats, block = carry
            for u in range(count):
                j = first + count * i + u
                ahead = far_scores(jnp.minimum(j, last_step), (u + 1) % 2)
                stats = absorb(stats, u % 2, s_max, block)
                s_max, block = ahead, j * per
            return s_max, stats, block
        return trip

    def last_step_alone(_, carry):
        s_max, stats, block = carry
        return s_max, absorb(stats, 0, s_max, block), block

    n_steps = n_far + 1
    carry = (tuple(tail_max), tuple(state), tb)
    done = 0
    for count in MOBA_STEPS_PER_TRIP:
        trips = (n_steps - done) // count
        carry = lax.fori_loop(0, trips, steps(count, done), carry)
        done = done + trips * count
    _, ((_, acc_a), (_, acc_b)), _ = lax.fori_loop(0, n_steps - done, last_step_alone, carry)
    o_t = jnp.concatenate([acc[:hd] / acc[hd:hd + 1] for acc in (acc_a, acc_b)], axis=0)
    o_ref[0] = o_t.T.astype(BF16)


def _moba(qm, km, vt, sel, tab):
    b, s, _ = qm.shape
    pairs = MOBA_HEADS // 2
    nblk = s // MOBA_BLOCK
    grid = (b, pairs, nblk)
    return pl.pallas_call(
        _moba_kernel,
        grid=grid,
        in_specs=[
            pl.BlockSpec((1, MOBA_BLOCK, LANES), lambda bi, hp, c: (bi, c, hp)),
            pl.BlockSpec((1, s, LANES), lambda bi, hp, c: (bi, 0, hp)),
            pl.BlockSpec((1, nblk, LANES, MOBA_BLOCK), lambda bi, hp, c: (bi, 0, hp, 0)),
            pl.BlockSpec((1, 2, MAX_BLOCKS, MOBA_BLOCK), lambda bi, hp, c: (bi, hp, 0, c)),
            pl.BlockSpec((2, 3, MOBA_BLOCK, MOBA_BLOCK), lambda bi, hp, c: (hp, 0, 0, 0)),
        ],
        out_specs=pl.BlockSpec((1, MOBA_BLOCK, LANES), lambda bi, hp, c: (bi, c, hp)),
        out_shape=jax.ShapeDtypeStruct((b, s, MOBA_W), BF16),
        scratch_shapes=[pltpu.VMEM((2, MAX_BLOCKS, MOBA_BLOCK), F32), pltpu.VMEM((2, MAX_BLOCKS, MOBA_BLOCK), F32),
                        pltpu.VMEM((2, 2, MOBA_KEYS_PER_STEP, MOBA_BLOCK), F32)],
        compiler_params=_cparams(3),
        name="moba",
    )(qm, km, vt, sel, tab)


def _local_kernel(qs_ref, ks_ref, kh_ref, vst_ref, vsh_ref, tab_ref, sink_ref, qe_ref, kme_ref, vmt_ref,
                  osw_ref, ome_ref):
    i = pl.program_id(1)
    qb = SWA_BLOCK
    pairs_per_kv = SWA_Q_HEADS // SWA_KV_HEADS // 2
    no_prev = jnp.where(i == 0, NEG, 0.0)
    row = lax.broadcasted_iota(jnp.int32, (LANES, qb), 0)
    for blk in range(ROW_TILE // qb):
        rows = slice(blk * qb, (blk + 1) * qb)
        for hk in range(SWA_KV_HEADS):
            vrows = slice(hk * LANES, (hk + 1) * LANES)
            v_prev = vsh_ref[0, 0, vrows] if blk == 0 else vst_ref[0, blk - 1, vrows]
            vv = jnp.concatenate([v_prev, vst_ref[0, blk, vrows]], axis=1)
            for jj in range(pairs_per_kv):
                j = hk * pairs_per_kv + jj
                qp = qs_ref[0, rows, j * LANES:(j + 1) * LANES]
                outs = []
                for par in range(2):
                    h = 2 * j + par
                    kc = slice((2 * hk + par) * LANES, (2 * hk + par + 1) * LANES)
                    if blk == 0:
                        kk = jnp.concatenate([kh_ref[0, :, kc], ks_ref[0, 0:qb, kc]], axis=0)
                    else:
                        kk = ks_ref[0, (blk - 1) * qb:(blk + 1) * qb, kc]
                    s = _nt(kk, qp) + tab_ref[h]
                    if blk == 0:
                        s = jnp.concatenate([s[:qb] + no_prev, s[qb:]], axis=0)
                    sink = sink_ref[h:h + 1, :]
                    m = jnp.maximum(jnp.max(s, axis=0, keepdims=True), sink)
                    p = jnp.exp2(s - m)
                    den = jnp.sum(p, axis=0, keepdims=True) + jnp.exp2(sink - m)
                    outs.append(_mm(vv, p.astype(BF16)) / den)
                o_t = jnp.where(row < SWA_HEAD_DIM, outs[0], outs[1])
                osw_ref[0, rows, j * LANES:(j + 1) * LANES] = o_t.T.astype(BF16)

    scale = LOG2E / math.sqrt(MEM_HEAD_DIM)
    for hd in range(MEM_HEADS):
        cols = slice(hd * MEM_HEAD_DIM, (hd + 1) * MEM_HEAD_DIM)
        kk = kme_ref[0, :, cols]
        vv = vmt_ref[0, cols, :]
        for blk in range(ROW_TILE // qb):
            rows = slice(blk * qb, (blk + 1) * qb)
            s = _nt(kk, qe_ref[0, rows, cols]) * scale
            m = jnp.max(s, axis=0, keepdims=True)
            p = jnp.exp2(s - m)
            den = jnp.sum(p, axis=0, keepdims=True)
            ome_ref[0, rows, cols] = (_mm(vv, p.astype(BF16)) / den).T.astype(BF16)


def _local(qs, ks, vst, tab, sinks, qe, kme, vmt):
    b, s, _ = qs.shape
    per_tile = ROW_TILE // SWA_BLOCK
    m_len = kme.shape[1]
    tile = lambda n: pl.BlockSpec((1, ROW_TILE, n), lambda bi, i: (bi, i, 0))
    prev_block = lambda bi, i: (bi, jnp.maximum(i * per_tile - 1, 0), 0)
    return pl.pallas_call(
        _local_kernel,
        grid=(b, s // ROW_TILE),
        in_specs=[
            tile(SWA_QW),
            tile(SWA_K_COLS), pl.BlockSpec((1, SWA_BLOCK, SWA_K_COLS), prev_block),
            pl.BlockSpec((1, per_tile, SWA_VT_ROWS, SWA_BLOCK), lambda bi, i: (bi, i, 0, 0)),
            pl.BlockSpec((1, 1, SWA_VT_ROWS, SWA_BLOCK), lambda bi, i: prev_block(bi, i) + (0,)),
            _resident((SWA_Q_HEADS, 2 * SWA_BLOCK, SWA_BLOCK)),
            _resident((SWA_Q_HEADS, LANES)),
            tile(MEM_W),
            pl.BlockSpec((1, m_len, MEM_W), lambda bi, i: (bi, 0, 0)),
            pl.BlockSpec((1, MEM_W, m_len), lambda bi, i: (bi, 0, 0)),
        ],
        out_specs=(tile(SWA_QW), tile(MEM_W)),
        out_shape=(jax.ShapeDtypeStruct((b, s, SWA_QW), BF16), jax.ShapeDtypeStruct((b, s, MEM_W), BF16)),
        compiler_params=_cparams(2),
        name="local",
    )(qs, ks, ks, vst, vst, tab, sinks, qe, kme, vmt)


def _merge_kernel(x_ref, omb_ref, osw_ref, ome_ref, gpre_ref, gpost_ref, wg_ref, wb_ref, wo_ref, out_ref):
    x = x_ref[...]
    h = _rms(x, gpre_ref[...]).astype(BF16)
    branch = (omb_ref[...], osw_ref[...], ome_ref[...])
    half = D_MODEL // 2
    parts = []
    for nc in range(2):
        acc = None
        for br in range(3):
            c0 = br * D_MODEL + nc * half
            gate = jax.nn.sigmoid(_mm(h, wg_ref[:, c0:c0 + half]))
            term = gate * _mm(branch[br], wb_ref[br, :, nc * half:(nc + 1) * half])
            acc = term if acc is None else acc + term
        parts.append(acc.astype(BF16))
    y = _mm(jnp.concatenate(parts, axis=1), wo_ref[...])
    out_ref[...] = x + _rms(y, gpost_ref[...])


def _merge(x2, omb, osw, ome, gpre, gpost, wg, wb, wo):
    t = x2.shape[0]
    row = lambda n: pl.BlockSpec((ROW_TILE, n), lambda i: (i, 0))
    return pl.pallas_call(
        _merge_kernel,
        grid=(t // ROW_TILE,),
        in_specs=[row(D_MODEL), row(MOBA_W), row(SWA_QW), row(MEM_W),
                  _resident((1, D_MODEL)), _resident((1, D_MODEL)),
                  _resident((D_MODEL, 3 * D_MODEL)), _resident((3, MOBA_W, D_MODEL)),
                  _resident((D_MODEL, D_MODEL))],
        out_specs=row(D_MODEL),
        out_shape=jax.ShapeDtypeStruct((t, D_MODEL), F32),
        compiler_params=_cparams(1),
        name="merge",
    )(x2, omb, osw, ome, gpre, gpost, wg, wb, wo)


def _ffn_kernel(x_ref, xh_ref, gpre_ref, gpost_ref, wup_ref, cw_ref, cb_ref, wd_ref, out_ref, acc_ref, act_ref,
                *, tiles_per_seq):
    i = pl.program_id(0)
    x = x_ref[...]
    xe = jnp.concatenate([xh_ref[...], x], axis=0)
    he = _rms(xe, gpre_ref[...])
    row = lax.broadcasted_iota(jnp.int32, (ROW_TILE + HALO, 1), 0)
    seq_start = (i % tiles_per_seq) == 0
    he = jnp.where((row < HALO) & seq_start, 0.0, he).astype(BF16)

    def up(fc):
        c0 = fc * FFN_CHUNK
        return (_mm(he, wup_ref[:, c0:c0 + FFN_CHUNK]),
                _mm(he, wup_ref[:, FFN_HIDDEN + c0:FFN_HIDDEN + c0 + FFN_CHUNK]))

    def conv(u, c0):
        w = cw_ref[:, c0:c0 + FFN_CHUNK]
        out = cb_ref[:, c0:c0 + FFN_CHUNK] + w[0:1] * pltpu.roll(u, 2, 0)
        out = out + w[1:2] * pltpu.roll(u, 1, 0)
        out = out + w[2:3] * u
        return out[HALO:]

    n_chunks = FFN_HIDDEN // FFN_CHUNK
    u_gate, u_val = up(0)
    for fc in range(n_chunks):
        c0 = fc * FFN_CHUNK
        ahead = up(fc + 1) if fc + 1 < n_chunks else None
        group, k = divmod(fc, FFN_GROUP)
        act = jax.nn.gelu(conv(u_gate, c0), approximate=True) * conv(u_val, FFN_HIDDEN + c0)
        act_ref[group % 2, :, k * FFN_CHUNK:(k + 1) * FFN_CHUNK] = act.astype(BF16)
        if k == FFN_GROUP - 1 or ahead is None:
            g0 = group * FFN_GROUP * FFN_CHUNK
            width = (k + 1) * FFN_CHUNK
            part = _mm(act_ref[group % 2, :, :width], wd_ref[g0:g0 + width, :])
            if group == 0:
                acc_ref[...] = part
            else:
                acc_ref[...] += part
        if ahead is not None:
            u_gate, u_val = ahead
    out_ref[...] = x + _rms(acc_ref[...], gpost_ref[...])


def _ffn(x2, gpre, gpost, wup, cw, cb, wd, seq_len):
    t = x2.shape[0]
    row = pl.BlockSpec((ROW_TILE, D_MODEL), lambda i: (i, 0))
    halo = pl.BlockSpec((HALO, D_MODEL), lambda i: (jnp.maximum(i * (ROW_TILE // HALO) - 1, 0), 0))
    return pl.pallas_call(
        functools.partial(_ffn_kernel, tiles_per_seq=seq_len // ROW_TILE),
        grid=(t // ROW_TILE,),
        in_specs=[row, halo, _resident((1, D_MODEL)), _resident((1, D_MODEL)),
                  _resident((D_MODEL, 2 * FFN_HIDDEN)), _resident((CONV_WIDTH, 2 * FFN_HIDDEN)),
                  _resident((1, 2 * FFN_HIDDEN)), _resident((FFN_HIDDEN, D_MODEL))],
        out_specs=row,
        out_shape=jax.ShapeDtypeStruct((t, D_MODEL), F32),
        scratch_shapes=[pltpu.VMEM((ROW_TILE, D_MODEL), F32),
                        pltpu.VMEM((2, ROW_TILE, FFN_GROUP * FFN_CHUNK), BF16)],
        compiler_params=_cparams(1),
        name="ffn",
    )(x2, x2, gpre, gpost, wup, cw, cb, wd)


def _t5_bucket(dist):
    n = jnp.maximum(dist, 0)
    nf = jnp.maximum(n, 1).astype(F32)
    large = MAX_EXACT + (jnp.log(nf / MAX_EXACT) / math.log(REL_MAX_DISTANCE / MAX_EXACT)
                         * (NUM_BUCKETS - MAX_EXACT)).astype(jnp.int32)
    large = jnp.minimum(large, NUM_BUCKETS - 1)
    return jnp.where(n < MAX_EXACT, n, large)


def _bias_tables(rel_bias):
    bm = rel_bias[:, :MOBA_HEADS].T.astype(F32)
    bs = rel_bias[:, MOBA_HEADS:].T.astype(F32)

    def lookup(table, dist):
        bucket = _t5_bucket(dist)[None]
        out = jnp.zeros((table.shape[0],) + dist.shape, F32)
        for bkt in range(NUM_BUCKETS):
            out = jnp.where(bucket == bkt, table[:, bkt][:, None, None], out)
        return out

    kj = jnp.arange(MOBA_BLOCK)[:, None]
    qi = jnp.arange(MOBA_BLOCK)[None, :]
    far = bm[:, NUM_BUCKETS - 1][:, None, None]
    own = jnp.where((qi - kj) >= 0, (lookup(bm, qi - kj) - far) * LOG2E, NEG)
    prev = (lookup(bm, MOBA_BLOCK + qi - kj) - far) * LOG2E
    tab_moba = jnp.stack([prev, own, jnp.full_like(own, NEG)], axis=1)
    kj = jnp.arange(2 * SWA_BLOCK)[:, None]
    qi = jnp.arange(SWA_BLOCK)[None, :]
    dist = SWA_BLOCK + qi - kj
    tab_swa = jnp.where((dist >= 0) & (dist < SWA_WINDOW), lookup(bs, dist) * LOG2E, NEG)
    return tab_moba, tab_swa


def _head_variants(w, heads):
    w3 = w.reshape(w.shape[0], heads, 1, -1)
    z = jnp.zeros_like(w3)
    lo = jnp.concatenate([w3, z], axis=3)
    hi = jnp.concatenate([z, w3], axis=3)
    return jnp.concatenate([lo, hi], axis=2).reshape(w.shape[0], heads * 2 * LANES)


def _proj_weight(w_in):
    o = 0
    parts = {}
    for name, width in (("qm", MOBA_W), ("km", MOBA_W), ("vm", MOBA_W), ("qs", SWA_QW), ("ks", SWA_KVW),
                        ("vs", SWA_KVW), ("qe", MEM_W)):
        parts[name] = w_in[:, o:o + width]
        o += width
    q_scale = LOG2E / math.sqrt(MOBA_HEAD_DIM)
    w = jnp.concatenate([parts["qm"] * q_scale, parts["km"], parts["qs"] * q_scale,
                         _head_variants(parts["ks"], SWA_KV_HEADS), parts["qe"]], axis=1)
    vs3 = parts["vs"].reshape(-1, SWA_KV_HEADS, SWA_HEAD_DIM)
    vs_dup = jnp.concatenate([vs3, vs3], axis=2).reshape(-1, SWA_VT_ROWS)
    w_t = jnp.concatenate([parts["vm"], vs_dup], axis=1).T
    return w.astype(BF16), w_t.astype(BF16), w_in[:, o:].astype(BF16)


def kernel(x, mem, norm_mix_pre, norm_mix_post, norm_ffn_pre, norm_ffn_post, norm_mem, w_in, rel_bias, swa_sinks,
           w_mem_kv, w_branch_moba, w_branch_swa, w_branch_mem, w_out, w_ffn_up, ffn_conv_w, ffn_conv_b,
           w_ffn_down):
    b, s, d = x.shape
    m_len = mem.shape[1]
    assert d == D_MODEL and s % ROW_TILE == 0 and s // MOBA_BLOCK <= 32 and m_len % LANES == 0
    depth = w_in.shape[0]
    tab_moba, tab_swa = _bias_tables(rel_bias)
    x2 = x.reshape(b * s, d).astype(F32)
    vec = lambda v: v.reshape(1, -1).astype(F32)
    nblk = s // MOBA_BLOCK
    r3 = lambda a: a.reshape(b, s, a.shape[-1])
    for l in range(depth):
        w_qkv, w_t, w_gate = _proj_weight(w_in[l])
        qm, km, vt, qs, ks, vst, qe, sel = _proj(x2, vec(norm_mix_pre[l]), w_qkv, w_t, s)
        kme, vmt = _memkv(mem.astype(F32), vec(norm_mem[l]), w_mem_kv[l][:, :MEM_W].astype(BF16),
                          w_mem_kv[l][:, MEM_W:].T.astype(BF16))
        o_mb = _moba(r3(qm), r3(km), vt.reshape(b, nblk, MOBA_W, MOBA_BLOCK), sel, tab_moba)
        sinks = jnp.broadcast_to(swa_sinks[l].astype(F32)[:, None] * LOG2E, (SWA_Q_HEADS, LANES))
        o_sw, o_me = _local(r3(qs), r3(ks), vst.reshape(b, s // SWA_BLOCK, SWA_VT_ROWS, SWA_BLOCK), tab_swa, sinks,
                            r3(qe), kme, vmt)
        w_branch = jnp.stack([w_branch_moba[l], w_branch_swa[l], w_branch_mem[l]]).astype(BF16)
        x2 = _merge(x2, o_mb.reshape(b * s, -1), o_sw.reshape(b * s, -1), o_me.reshape(b * s, -1),
                    vec(norm_mix_pre[l]), vec(norm_mix_post[l]), w_gate, w_branch, w_out[l].astype(BF16))
        x2 = _ffn(x2, vec(norm_ffn_pre[l]), vec(norm_ffn_post[l]), w_ffn_up[l].astype(BF16),
                  ffn_conv_w[l].astype(F32), vec(ffn_conv_b[l]), w_ffn_down[l].astype(BF16), s)
    return x2.reshape(b, s, d).astype(x.dtype)
```

```python
import functools
import math

import jax
import jax.numpy as jnp
from jax import lax
from jax.experimental import pallas as pl
from jax.experimental.pallas import tpu as pltpu

D_MODEL = 1024
MOBA_HEADS = 8
MOBA_HEAD_DIM = 64
MOBA_BLOCK = 256
MOBA_TOPK = 3
SWA_Q_HEADS = 8
SWA_KV_HEADS = 2
SWA_HEAD_DIM = 64
SWA_WINDOW = 128
SWA_BLOCK = 128
MEM_HEADS = 4
MEM_HEAD_DIM = 128
NUM_BUCKETS = 32
MAX_EXACT = NUM_BUCKETS // 2
REL_MAX_DISTANCE = 128
FFN_HIDDEN = 2816
CONV_WIDTH = 3
RMS_EPS = 1e-6

MOBA_W = MOBA_HEADS * MOBA_HEAD_DIM
SWA_QW = SWA_Q_HEADS * SWA_HEAD_DIM
SWA_KVW = SWA_KV_HEADS * SWA_HEAD_DIM
MEM_W = MEM_HEADS * MEM_HEAD_DIM

LANES = 128
NEG = -1e30
VMEM_LIMIT = 56 * 1024 * 1024
ROW_TILE = 512
MOBA_KEYS_PER_STEP = 512
MOBA_STEPS_PER_TRIP = (4, 2)
FFN_CHUNK = 256
FFN_GROUP = 4
HALO = 8
SUM_ROWS = 16
MAX_BLOCKS = 32

BF16 = jnp.bfloat16
F32 = jnp.float32

SWA_K_COLS = 2 * SWA_KV_HEADS * LANES
SWA_VT_ROWS = SWA_KV_HEADS * LANES
OFF_QM = 0
OFF_KM = OFF_QM + MOBA_W
OFF_QS = OFF_KM + MOBA_W
OFF_KS = OFF_QS + SWA_QW
OFF_QE = OFF_KS + SWA_K_COLS
PROJ_W = OFF_QE + MEM_W
LOG2E = math.log2(math.e)


def _mm(a, b):
    return jnp.dot(a, b, preferred_element_type=F32)


def _nt(a, b):
    return lax.dot_general(a, b, (((1,), (1,)), ((), ())), preferred_element_type=F32)


def _rms(xf, g):
    r = lax.rsqrt(jnp.mean(xf * xf, axis=-1, keepdims=True) + RMS_EPS)
    return xf * r * g


def _cparams(n_axes):
    return pltpu.CompilerParams(dimension_semantics=("arbitrary",) * n_axes, vmem_limit_bytes=VMEM_LIMIT)


def _resident(shape):
    nd = len(shape)
    return pl.BlockSpec(shape, lambda *_: (0,) * nd, pipeline_mode=pl.Buffered(1))


def _proj_kernel(x_ref, g_ref, w_ref, wt_ref, qm_ref, km_ref, vt_ref, qs_ref, ks_ref, vst_ref, qe_ref, sel_ref,
                 kmean_ref, *, tiles_per_seq):
    i = pl.program_id(0)
    h = _rms(x_ref[...], g_ref[...]).astype(BF16)

    def seg(c0, n):
        return _mm(h, w_ref[:, c0:c0 + n])

    q = seg(OFF_QM, MOBA_W).astype(BF16)
    qm_ref[...] = q
    k = seg(OFF_KM, MOBA_W)
    km_ref[...] = k.astype(BF16)
    qs_ref[...] = seg(OFF_QS, SWA_QW).astype(BF16)
    ks_ref[...] = seg(OFF_KS, SWA_K_COLS).astype(BF16)
    qe_ref[...] = seg(OFF_QE, MEM_W).astype(BF16)
    vt = _nt(wt_ref[...], h).astype(BF16)
    for r in range(ROW_TILE // SWA_BLOCK):
        vst_ref[r] = vt[MOBA_W:, r * SWA_BLOCK:(r + 1) * SWA_BLOCK]

    @pl.when(i % tiles_per_seq == 0)
    def _():
        kmean_ref[...] = jnp.zeros_like(kmean_ref)

    blocks_per_tile = ROW_TILE // MOBA_BLOCK
    c0 = (i % tiles_per_seq) * blocks_per_tile
    lane = lax.broadcasted_iota(jnp.int32, (1, MOBA_W), 1)
    for r in range(blocks_per_tile):
        rows = slice(r * MOBA_BLOCK, (r + 1) * MOBA_BLOCK)
        vt_ref[r] = vt[:MOBA_W, rows]
        mean = jnp.sum(k[rows], axis=0, keepdims=True) * (1.0 / MOBA_BLOCK)
        for hd in range(MOBA_HEADS):
            in_head = (lane >= hd * MOBA_HEAD_DIM) & (lane < (hd + 1) * MOBA_HEAD_DIM)
            kmean_ref[pl.ds(hd * MAX_BLOCKS + c0 + r, 1), :] = jnp.where(in_head, mean, 0.0)

    km = kmean_ref[...]
    km_hi = km.astype(BF16)
    km_lo = (km - km_hi.astype(F32)).astype(BF16)
    gates = _nt(km_hi, q) + _nt(km_lo, q)
    blk = lax.broadcasted_iota(jnp.int32, (MAX_BLOCKS, ROW_TILE), 0)
    blk_f = blk.astype(F32)
    col = lax.broadcasted_iota(jnp.int32, (MAX_BLOCKS, ROW_TILE), 1)
    q_blk = c0 + sum((col >= r * MOBA_BLOCK).astype(jnp.int32) for r in range(1, blocks_per_tile))
    past = blk < q_blk
    for hd in range(MOBA_HEADS):
        g = jnp.where(past, gates[hd * MAX_BLOCKS:(hd + 1) * MAX_BLOCKS], -jnp.inf)
        sel = blk < 0
        for _ in range(MOBA_TOPK):
            top = jnp.max(g, axis=0, keepdims=True)
            first = jnp.min(jnp.where(g == top, blk_f, 1e9), axis=0, keepdims=True)
            hit = blk_f == first
            sel = sel | hit
            g = jnp.where(hit, -jnp.inf, g)
        sel_ref[0, hd] = jnp.where(sel & past, 1.0, 0.0)


def _proj(x2, g, w, wt, seq_len):
    t = x2.shape[0]
    n_tiles = t // ROW_TILE
    tiles_per_seq = seq_len // ROW_TILE
    bpt = ROW_TILE // MOBA_BLOCK
    spt = ROW_TILE // SWA_BLOCK
    row = lambda n: pl.BlockSpec((ROW_TILE, n), lambda i: (i, 0))
    out_shapes = (
        jax.ShapeDtypeStruct((t, MOBA_W), BF16),
        jax.ShapeDtypeStruct((t, MOBA_W), BF16),
        jax.ShapeDtypeStruct((t // MOBA_BLOCK, MOBA_W, MOBA_BLOCK), BF16),
        jax.ShapeDtypeStruct((t, SWA_QW), BF16),
        jax.ShapeDtypeStruct((t, SWA_K_COLS), BF16),
        jax.ShapeDtypeStruct((t // SWA_BLOCK, SWA_VT_ROWS, SWA_BLOCK), BF16),
        jax.ShapeDtypeStruct((t, MEM_W), BF16),
        jax.ShapeDtypeStruct((t // seq_len, MOBA_HEADS, MAX_BLOCKS, seq_len), F32),
    )
    out_specs = (row(MOBA_W), row(MOBA_W), pl.BlockSpec((bpt, MOBA_W, MOBA_BLOCK), lambda i: (i, 0, 0)),
                 row(SWA_QW), row(SWA_K_COLS), pl.BlockSpec((spt, SWA_VT_ROWS, SWA_BLOCK), lambda i: (i, 0, 0)),
                 row(MEM_W),
                 pl.BlockSpec((1, MOBA_HEADS, MAX_BLOCKS, ROW_TILE),
                              lambda i: (i // tiles_per_seq, 0, 0, i % tiles_per_seq)))
    return pl.pallas_call(
        functools.partial(_proj_kernel, tiles_per_seq=tiles_per_seq),
        grid=(n_tiles,),
        in_specs=[row(D_MODEL), _resident((1, D_MODEL)), _resident((D_MODEL, PROJ_W)),
                  _resident((MOBA_W + SWA_VT_ROWS, D_MODEL))],
        out_specs=out_specs,
        out_shape=out_shapes,
        scratch_shapes=[pltpu.VMEM((MOBA_HEADS * MAX_BLOCKS, MOBA_W), F32)],
        compiler_params=_cparams(1),
        name="proj",
    )(x2, g, w, wt)


def _memkv_kernel(m_ref, g_ref, wk_ref, wvt_ref, k_ref, vt_ref):
    h = _rms(m_ref[0], g_ref[...]).astype(BF16)
    k_ref[0] = _mm(h, wk_ref[...]).astype(BF16)
    vt_ref[0] = _nt(wvt_ref[...], h).astype(BF16)


def _memkv(mem, g, wk, wvt):
    b, m_len, _ = mem.shape
    return pl.pallas_call(
        _memkv_kernel,
        grid=(b,),
        in_specs=[pl.BlockSpec((1, m_len, D_MODEL), lambda i: (i, 0, 0)), _resident((1, D_MODEL)),
                  _resident((D_MODEL, MEM_W)), _resident((MEM_W, D_MODEL))],
        out_specs=(pl.BlockSpec((1, m_len, MEM_W), lambda i: (i, 0, 0)),
                   pl.BlockSpec((1, MEM_W, m_len), lambda i: (i, 0, 0))),
        out_shape=(jax.ShapeDtypeStruct((b, m_len, MEM_W), BF16), jax.ShapeDtypeStruct((b, MEM_W, m_len), BF16)),
        compiler_params=_cparams(1),
        name="memkv",
    )(mem, g, wk, wvt)


def _moba_kernel(q_ref, k_ref, vt_ref, sel_ref, tab_ref, o_ref, mfar_ref, mtail_ref, s_ref):
    c = pl.program_id(2)
    mb = MOBA_BLOCK
    kb = MOBA_KEYS_PER_STEP
    per = kb // mb
    hd = MOBA_HEAD_DIM
    lane_q = lax.broadcasted_iota(jnp.int32, (mb, LANES), 1)
    blk = lax.broadcasted_iota(jnp.int32, (MAX_BLOCKS, mb), 0)
    tb = jnp.maximum(c - 1, 0)
    kt = pl.multiple_of(tb * mb, mb)
    left = jnp.where(c == 0, 1, 0)
    right = jnp.where(c == 0, 2, 1)

    ones = jnp.ones((SUM_ROWS, mb), BF16)

    def values(block, a):
        return jnp.concatenate([vt_ref[0, block, a * hd:(a + 1) * hd], ones], axis=0)

    qf = q_ref[0].astype(F32)
    k_tail = k_ref[0, pl.ds(kt, 2 * mb), :]
    q_pad = []
    state = []
    tail_max = []
    for a in range(2):
        qa = jnp.where((lane_q >= a * hd) & (lane_q < (a + 1) * hd), qf, 0.0).astype(BF16)
        q_pad.append(qa)
        sel = sel_ref[0, a] > 0.5
        mfar_ref[a] = jnp.where(sel & (blk <= c - 2), 0.0, NEG)
        mtail_ref[a] = jnp.where((blk == c) | (sel & (blk == c - 1)), 0.0, NEG)

        s = _nt(k_tail, qa)
        s = jnp.concatenate([s[:mb] + tab_ref[a, left] + mtail_ref[a, pl.ds(tb, 1), :],
                             s[mb:] + tab_ref[a, right] + mtail_ref[a, pl.ds(tb + 1, 1), :]], axis=0)
        s_ref[0, a] = s
        tail_max.append(jnp.max(s, axis=0, keepdims=True))
        state.append((jnp.full((1, mb), -jnp.inf, F32), jnp.zeros((hd + SUM_ROWS, mb), F32)))

    n_far = jnp.where(c >= 2, ((c - 1) * mb + kb - 1) // kb, 0)
    last_step = k_ref.shape[1] // kb - 1

    def far_scores(j, slot):
        kj = k_ref[0, pl.ds(pl.multiple_of(j * kb, kb), kb), :]
        out = []
        for a in range(2):
            s = _nt(kj, q_pad[a])
            s = jnp.concatenate([s[t * mb:(t + 1) * mb] + mfar_ref[a, pl.ds(j * per + t, 1), :]
                                 for t in range(per)], axis=0)
            s_ref[slot, a] = s
            out.append(jnp.max(s, axis=0, keepdims=True))
        return tuple(out)

    def absorb(stats, slot, s_max, block):
        out = []
        for a in range(2):
            m, acc = stats[a]
            m_new = jnp.maximum(m, s_max[a])
            pb = jnp.exp2(s_ref[slot, a] - m_new).astype(BF16)
            acc = jnp.exp2(m - m_new) * acc
            for t in range(per):
                acc = acc + _mm(values(block + t, a), pb[t * mb:(t + 1) * mb])
            out.append((m_new, acc))
        return tuple(out)

    def steps(count, first):
        def trip(i, carry):
            s_max, stats, block = carry
            for u in range(count):
                j = first + count * i + u
                ahead = far_scores(jnp.minimum(j, last_step), (u + 1) % 2)
                stats = absorb(stats, u % 2, s_max, block)
                s_max, block = ahead, j * per
            return s_max, stats, block
        return trip

    def last_step_alone(_, carry):
        s_max, stats, block = carry
        return s_max, absorb(stats, 0, s_max, block), block

    n_steps = n_far + 1
    carry = (tuple(tail_max), tuple(state), tb)
    done = 0
    for count in MOBA_STEPS_PER_TRIP:
        trips = (n_steps - done) // count
        carry = lax.fori_loop(0, trips, steps(count, done), carry)
        done = done + trips * count
    _, ((_, acc_a), (_, acc_b)), _ = lax.fori_loop(0, n_steps - done, last_step_alone, carry)
    o_t = jnp.concatenate([acc[:hd] / acc[hd:hd + 1] for acc in (acc_a, acc_b)], axis=0)
    o_ref[0] = o_t.T.astype(BF16)


def _moba(qm, km, vt, sel, tab):
    b, s, _ = qm.shape
    pairs = MOBA_HEADS // 2
    nblk = s // MOBA_BLOCK
    grid = (b, pairs, nblk)
    return pl.pallas_call(
        _moba_kernel,
        grid=grid,
        in_specs=[
            pl.BlockSpec((1, MOBA_BLOCK, LANES), lambda bi, hp, c: (bi, c, hp)),
            pl.BlockSpec((1, s, LANES), lambda bi, hp, c: (bi, 0, hp)),
            pl.BlockSpec((1, nblk, LANES, MOBA_BLOCK), lambda bi, hp, c: (bi, 0, hp, 0)),
            pl.BlockSpec((1, 2, MAX_BLOCKS, MOBA_BLOCK), lambda bi, hp, c: (bi, hp, 0, c)),
            pl.BlockSpec((2, 3, MOBA_BLOCK, MOBA_BLOCK), lambda bi, hp, c: (hp, 0, 0, 0)),
        ],
        out_specs=pl.BlockSpec((1, MOBA_BLOCK, LANES), lambda bi, hp, c: (bi, c, hp)),
        out_shape=jax.ShapeDtypeStruct((b, s, MOBA_W), BF16),
        scratch_shapes=[pltpu.VMEM((2, MAX_BLOCKS, MOBA_BLOCK), F32), pltpu.VMEM((2, MAX_BLOCKS, MOBA_BLOCK), F32),
                        pltpu.VMEM((2, 2, MOBA_KEYS_PER_STEP, MOBA_BLOCK), F32)],
        compiler_params=_cparams(3),
        name="moba",
    )(qm, km, vt, sel, tab)


def _local_kernel(qs_ref, ks_ref, kh_ref, vst_ref, vsh_ref, tab_ref, sink_ref, qe_ref, kme_ref, vmt_ref,
                  osw_ref, ome_ref):
    i = pl.program_id(1)
    qb = SWA_BLOCK
    pairs_per_kv = SWA_Q_HEADS // SWA_KV_HEADS // 2
    no_prev = jnp.where(i == 0, NEG, 0.0)
    row = lax.broadcasted_iota(jnp.int32, (LANES, qb), 0)
    for blk in range(ROW_TILE // qb):
        rows = slice(blk * qb, (blk + 1) * qb)
        for hk in range(SWA_KV_HEADS):
            vrows = slice(hk * LANES, (hk + 1) * LANES)
            v_prev = vsh_ref[0, 0, vrows] if blk == 0 else vst_ref[0, blk - 1, vrows]
            vv = jnp.concatenate([v_prev, vst_ref[0, blk, vrows]], axis=1)
            for jj in range(pairs_per_kv):
                j = hk * pairs_per_kv + jj
                qp = qs_ref[0, rows, j * LANES:(j + 1) * LANES]
                outs = []
                for par in range(2):
                    h = 2 * j + par
                    kc = slice((2 * hk + par) * LANES, (2 * hk + par + 1) * LANES)
                    if blk == 0:
                        kk = jnp.concatenate([kh_ref[0, :, kc], ks_ref[0, 0:qb, kc]], axis=0)
                    else:
                        kk = ks_ref[0, (blk - 1) * qb:(blk + 1) * qb, kc]
                    s = _nt(kk, qp) + tab_ref[h]
                    if blk == 0:
                        s = jnp.concatenate([s[:qb] + no_prev, s[qb:]], axis=0)
                    sink = sink_ref[h:h + 1, :]
                    m = jnp.maximum(jnp.max(s, axis=0, keepdims=True), sink)
                    p = jnp.exp2(s - m)
                    den = jnp.sum(p, axis=0, keepdims=True) + jnp.exp2(sink - m)
                    outs.append(_mm(vv, p.astype(BF16)) / den)
                o_t = jnp.where(row < SWA_HEAD_DIM, outs[0], outs[1])
                osw_ref[0, rows, j * LANES:(j + 1) * LANES] = o_t.T.astype(BF16)

    scale = LOG2E / math.sqrt(MEM_HEAD_DIM)
    for hd in range(MEM_HEADS):
        cols = slice(hd * MEM_HEAD_DIM, (hd + 1) * MEM_HEAD_DIM)
        kk = kme_ref[0, :, cols]
        vv = vmt_ref[0, cols, :]
        for blk in range(ROW_TILE // qb):
            rows = slice(blk * qb, (blk + 1) * qb)
            s = _nt(kk, qe_ref[0, rows, cols]) * scale
            m = jnp.max(s, axis=0, keepdims=True)
            p = jnp.exp2(s - m)
            den = jnp.sum(p, axis=0, keepdims=True)
            ome_ref[0, rows, cols] = (_mm(vv, p.astype(BF16)) / den).T.astype(BF16)


def _local(qs, ks, vst, tab, sinks, qe, kme, vmt):
    b, s, _ = qs.shape
    per_tile = ROW_TILE // SWA_BLOCK
    m_len = kme.shape[1]
    tile = lambda n: pl.BlockSpec((1, ROW_TILE, n), lambda bi, i: (bi, i, 0))
    prev_block = lambda bi, i: (bi, jnp.maximum(i * per_tile - 1, 0), 0)
    return pl.pallas_call(
        _local_kernel,
        grid=(b, s // ROW_TILE),
        in_specs=[
            tile(SWA_QW),
            tile(SWA_K_COLS), pl.BlockSpec((1, SWA_BLOCK, SWA_K_COLS), prev_block),
            pl.BlockSpec((1, per_tile, SWA_VT_ROWS, SWA_BLOCK), lambda bi, i: (bi, i, 0, 0)),
            pl.BlockSpec((1, 1, SWA_VT_ROWS, SWA_BLOCK), lambda bi, i: prev_block(bi, i) + (0,)),
            _resident((SWA_Q_HEADS, 2 * SWA_BLOCK, SWA_BLOCK)),
            _resident((SWA_Q_HEADS, LANES)),
            tile(MEM_W),
            pl.BlockSpec((1, m_len, MEM_W), lambda bi, i: (bi, 0, 0)),
            pl.BlockSpec((1, MEM_W, m_len), lambda bi, i: (bi, 0, 0)),
        ],
        out_specs=(tile(SWA_QW), tile(MEM_W)),
        out_shape=(jax.ShapeDtypeStruct((b, s, SWA_QW), BF16), jax.ShapeDtypeStruct((b, s, MEM_W), BF16)),
        compiler_params=_cparams(2),
        name="local",
    )(qs, ks, ks, vst, vst, tab, sinks, qe, kme, vmt)


def _merge_kernel(x_ref, omb_ref, osw_ref, ome_ref, gpre_ref, gpost_ref, wg_ref, wb_ref, wo_ref, out_ref):
    x = x_ref[...]
    h = _rms(x, gpre_ref[...]).astype(BF16)
    branch = (omb_ref[...], osw_ref[...], ome_ref[...])
    half = D_MODEL // 2
    parts = []
    for nc in range(2):
        acc = None
        for br in range(3):
            c0 = br * D_MODEL + nc * half
            gate = jax.nn.sigmoid(_mm(h, wg_ref[:, c0:c0 + half]))
            term = gate * _mm(branch[br], wb_ref[br, :, nc * half:(nc + 1) * half])
            acc = term if acc is None else acc + term
        parts.append(acc.astype(BF16))
    y = _mm(jnp.concatenate(parts, axis=1), wo_ref[...])
    out_ref[...] = x + _rms(y, gpost_ref[...])


def _merge(x2, omb, osw, ome, gpre, gpost, wg, wb, wo):
    t = x2.shape[0]
    row = lambda n: pl.BlockSpec((ROW_TILE, n), lambda i: (i, 0))
    return pl.pallas_call(
        _merge_kernel,
        grid=(t // ROW_TILE,),
        in_specs=[row(D_MODEL), row(MOBA_W), row(SWA_QW), row(MEM_W),
                  _resident((1, D_MODEL)), _resident((1, D_MODEL)),
                  _resident((D_MODEL, 3 * D_MODEL)), _resident((3, MOBA_W, D_MODEL)),
                  _resident((D_MODEL, D_MODEL))],
        out_specs=row(D_MODEL),
        out_shape=jax.ShapeDtypeStruct((t, D_MODEL), F32),
        compiler_params=_cparams(1),
        name="merge",
    )(x2, omb, osw, ome, gpre, gpost, wg, wb, wo)


def _ffn_kernel(x_ref, xh_ref, gpre_ref, gpost_ref, wup_ref, cw_ref, cb_ref, wd_ref, out_ref, acc_ref, act_ref,
                *, tiles_per_seq):
    i = pl.program_id(0)
    x = x_ref[...]
    xe = jnp.concatenate([xh_ref[...], x], axis=0)
    he = _rms(xe, gpre_ref[...])
    row = lax.broadcasted_iota(jnp.int32, (ROW_TILE + HALO, 1), 0)
    seq_start = (i % tiles_per_seq) == 0
    he = jnp.where((row < HALO) & seq_start, 0.0, he).astype(BF16)

    def up(fc):
        c0 = fc * FFN_CHUNK
        return (_mm(he, wup_ref[:, c0:c0 + FFN_CHUNK]),
                _mm(he, wup_ref[:, FFN_HIDDEN + c0:FFN_HIDDEN + c0 + FFN_CHUNK]))

    def conv(u, c0):
        w = cw_ref[:, c0:c0 + FFN_CHUNK]
        out = cb_ref[:, c0:c0 + FFN_CHUNK] + w[0:1] * pltpu.roll(u, 2, 0)
        out = out + w[1:2] * pltpu.roll(u, 1, 0)
        out = out + w[2:3] * u
        return out[HALO:]

    n_chunks = FFN_HIDDEN // FFN_CHUNK
    u_gate, u_val = up(0)
    for fc in range(n_chunks):
        c0 = fc * FFN_CHUNK
        ahead = up(fc + 1) if fc + 1 < n_chunks else None
        group, k = divmod(fc, FFN_GROUP)
        act = jax.nn.gelu(conv(u_gate, c0), approximate=True) * conv(u_val, FFN_HIDDEN + c0)
        act_ref[group % 2, :, k * FFN_CHUNK:(k + 1) * FFN_CHUNK] = act.astype(BF16)
        if k == FFN_GROUP - 1 or ahead is None:
            g0 = group * FFN_GROUP * FFN_CHUNK
            width = (k + 1) * FFN_CHUNK
            part = _mm(act_ref[group % 2, :, :width], wd_ref[g0:g0 + width, :])
            if group == 0:
                acc_ref[...] = part
            else:
                acc_ref[...] += part
        if ahead is not None:
            u_gate, u_val = ahead
    out_ref[...] = x + _rms(acc_ref[...], gpost_ref[...])


def _ffn(x2, gpre, gpost, wup, cw, cb, wd, seq_len):
    t = x2.shape[0]
    row = pl.BlockSpec((ROW_TILE, D_MODEL), lambda i: (i, 0))
    halo = pl.BlockSpec((HALO, D_MODEL), lambda i: (jnp.maximum(i * (ROW_TILE // HALO) - 1, 0), 0))
    return pl.pallas_call(
        functools.partial(_ffn_kernel, tiles_per_seq=seq_len // ROW_TILE),
        grid=(t // ROW_TILE,),
        in_specs=[row, halo, _resident((1, D_MODEL)), _resident((1, D_MODEL)),
                  _resident((D_MODEL, 2 * FFN_HIDDEN)), _resident((CONV_WIDTH, 2 * FFN_HIDDEN)),
                  _resident((1, 2 * FFN_HIDDEN)), _resident((FFN_HIDDEN, D_MODEL))],
        out_specs=row,
        out_shape=jax.ShapeDtypeStruct((t, D_MODEL), F32),
        scratch_shapes=[pltpu.VMEM((ROW_TILE, D_MODEL), F32),
                        pltpu.VMEM((2, ROW_TILE, FFN_GROUP * FFN_CHUNK), BF16)],
        compiler_params=_cparams(1),
        name="ffn",
    )(x2, x2, gpre, gpost, wup, cw, cb, wd)


def _t5_bucket(dist):
    n = jnp.maximum(dist, 0)
    nf = jnp.maximum(n, 1).astype(F32)
    large = MAX_EXACT + (jnp.log(nf / MAX_EXACT) / math.log(REL_MAX_DISTANCE / MAX_EXACT)
                         * (NUM_BUCKETS - MAX_EXACT)).astype(jnp.int32)
    large = jnp.minimum(large, NUM_BUCKETS - 1)
    return jnp.where(n < MAX_EXACT, n, large)


def _bias_tables(rel_bias):
    bm = rel_bias[:, :MOBA_HEADS].T.astype(F32)
    bs = rel_bias[:, MOBA_HEADS:].T.astype(F32)

    def lookup(table, dist):
        bucket = _t5_bucket(dist)[None]
        out = jnp.zeros((table.shape[0],) + dist.shape, F32)
        for bkt in range(NUM_BUCKETS):
            out = jnp.where(bucket == bkt, table[:, bkt][:, None, None], out)
        return out

    kj = jnp.arange(MOBA_BLOCK)[:, None]
    qi = jnp.arange(MOBA_BLOCK)[None, :]
    far = bm[:, NUM_BUCKETS - 1][:, None, None]
    own = jnp.where((qi - kj) >= 0, (lookup(bm, qi - kj) - far) * LOG2E, NEG)
    prev = (lookup(bm, MOBA_BLOCK + qi - kj) - far) * LOG2E
    tab_moba = jnp.stack([prev, own, jnp.full_like(own, NEG)], axis=1)
    kj = jnp.arange(2 * SWA_BLOCK)[:, None]
    qi = jnp.arange(SWA_BLOCK)[None, :]
    dist = SWA_BLOCK + qi - kj
    tab_swa = jnp.where((dist >= 0) & (dist < SWA_WINDOW), lookup(bs, dist) * LOG2E, NEG)
    return tab_moba, tab_swa


def _head_variants(w, heads):
    w3 = w.reshape(w.shape[0], heads, 1, -1)
    z = jnp.zeros_like(w3)
    lo = jnp.concatenate([w3, z], axis=3)
    hi = jnp.concatenate([z, w3], axis=3)
    return jnp.concatenate([lo, hi], axis=2).reshape(w.shape[0], heads * 2 * LANES)


def _proj_weight(w_in):
    o = 0
    parts = {}
    for name, width in (("qm", MOBA_W), ("km", MOBA_W), ("vm", MOBA_W), ("qs", SWA_QW), ("ks", SWA_KVW),
                        ("vs", SWA_KVW), ("qe", MEM_W)):
        parts[name] = w_in[:, o:o + width]
        o += width
    q_scale = LOG2E / math.sqrt(MOBA_HEAD_DIM)
    w = jnp.concatenate([parts["qm"] * q_scale, parts["km"], parts["qs"] * q_scale,
                         _head_variants(parts["ks"], SWA_KV_HEADS), parts["qe"]], axis=1)
    vs3 = parts["vs"].reshape(-1, SWA_KV_HEADS, SWA_HEAD_DIM)
    vs_dup = jnp.concatenate([vs3, vs3], axis=2).reshape(-1, SWA_VT_ROWS)
    w_t = jnp.concatenate([parts["vm"], vs_dup], axis=1).T
    return w.astype(BF16), w_t.astype(BF16), w_in[:, o:].astype(BF16)


def kernel(x, mem, norm_mix_pre, norm_mix_post, norm_ffn_pre, norm_ffn_post, norm_mem, w_in, rel_bias, swa_sinks,
           w_mem_kv, w_branch_moba, w_branch_swa, w_branch_mem, w_out, w_ffn_up, ffn_conv_w, ffn_conv_b,
           w_ffn_down):
    b, s, d = x.shape
    m_len = mem.shape[1]
    assert d == D_MODEL and s % ROW_TILE == 0 and s // MOBA_BLOCK <= 32 and m_len % LANES == 0
    depth = w_in.shape[0]
    tab_moba, tab_swa = _bias_tables(rel_bias)
    x2 = x.reshape(b * s, d).astype(F32)
    vec = lambda v: v.reshape(1, -1).astype(F32)
    nblk = s // MOBA_BLOCK
    r3 = lambda a: a.reshape(b, s, a.shape[-1])
    for l in range(depth):
        w_qkv, w_t, w_gate = _proj_weight(w_in[l])
        qm, km, vt, qs, ks, vst, qe, sel = _proj(x2, vec(norm_mix_pre[l]), w_qkv, w_t, s)
        kme, vmt = _memkv(mem.astype(F32), vec(norm_mem[l]), w_mem_kv[l][:, :MEM_W].astype(BF16),
                          w_mem_kv[l][:, MEM_W:].T.astype(BF16))
        o_mb = _moba(r3(qm), r3(km), vt.reshape(b, nblk, MOBA_W, MOBA_BLOCK), sel, tab_moba)
        sinks = jnp.broadcast_to(swa_sinks[l].astype(F32)[:, None] * LOG2E, (SWA_Q_HEADS, LANES))
        o_sw, o_me = _local(r3(qs), r3(ks), vst.reshape(b, s // SWA_BLOCK, SWA_VT_ROWS, SWA_BLOCK), tab_swa, sinks,
                            r3(qe), kme, vmt)
        w_branch = jnp.stack([w_branch_moba[l], w_branch_swa[l], w_branch_mem[l]]).astype(BF16)
        x2 = _merge(x2, o_mb.reshape(b * s, -1), o_sw.reshape(b * s, -1), o_me.reshape(b * s, -1),
                    vec(norm_mix_pre[l]), vec(norm_mix_post[l]), w_gate, w_branch, w_out[l].astype(BF16))
        x2 = _ffn(x2, vec(norm_ffn_pre[l]), vec(norm_ffn_post[l]), w_ffn_up[l].astype(BF16),
                  ffn_conv_w[l].astype(F32), vec(ffn_conv_b[l]), w_ffn_down[l].astype(BF16), s)
    return x2.reshape(b, s, d).astype(x.dtype)
```

```python
import functools
import math

import jax
import jax.numpy as jnp
from jax import lax
from jax.experimental import pallas as pl
from jax.experimental.pallas import tpu as pltpu

D_MODEL = 1024
MOBA_HEADS = 8
MOBA_HEAD_DIM = 64
MOBA_BLOCK = 256
MOBA_TOPK = 3
SWA_Q_HEADS = 8
SWA_KV_HEADS = 2
SWA_HEAD_DIM = 64
SWA_WINDOW = 128
SWA_BLOCK = 128
MEM_HEADS = 4
MEM_HEAD_DIM = 128
NUM_BUCKETS = 32
MAX_EXACT = NUM_BUCKETS // 2
REL_MAX_DISTANCE = 128
FFN_HIDDEN = 2816
CONV_WIDTH = 3
RMS_EPS = 1e-6

MOBA_W = MOBA_HEADS * MOBA_HEAD_DIM
SWA_QW = SWA_Q_HEADS * SWA_HEAD_DIM
SWA_KVW = SWA_KV_HEADS * SWA_HEAD_DIM
MEM_W = MEM_HEADS * MEM_HEAD_DIM

LANES = 128
NEG = -1e30
VMEM_LIMIT = 56 * 1024 * 1024
ROW_TILE = 512
MOBA_KEYS_PER_STEP = 512
MOBA_STEPS_PER_TRIP = (4, 2)
MOBA_HEADS_PER_STEP = 4
FFN_CHUNK = 256
FFN_GROUP = 4
HALO = 8
SUM_ROWS = 16
MAX_BLOCKS = 32

BF16 = jnp.bfloat16
F32 = jnp.float32

SWA_K_COLS = 2 * SWA_KV_HEADS * LANES
SWA_VT_ROWS = SWA_KV_HEADS * LANES
OFF_QM = 0
OFF_KM = OFF_QM + MOBA_W
OFF_QS = OFF_KM + MOBA_W
OFF_KS = OFF_QS + SWA_QW
OFF_QE = OFF_KS + SWA_K_COLS
PROJ_W = OFF_QE + MEM_W
LOG2E = math.log2(math.e)


def _mm(a, b):
    return jnp.dot(a, b, preferred_element_type=F32)


def _nt(a, b):
    return lax.dot_general(a, b, (((1,), (1,)), ((), ())), preferred_element_type=F32)


def _rms(xf, g):
    r = lax.rsqrt(jnp.mean(xf * xf, axis=-1, keepdims=True) + RMS_EPS)
    return xf * r * g


def _cparams(n_axes):
    return pltpu.CompilerParams(dimension_semantics=("arbitrary",) * n_axes, vmem_limit_bytes=VMEM_LIMIT)


def _resident(shape):
    nd = len(shape)
    return pl.BlockSpec(shape, lambda *_: (0,) * nd, pipeline_mode=pl.Buffered(1))


def _proj_kernel(x_ref, g_ref, w_ref, wt_ref, qm_ref, km_ref, vt_ref, qs_ref, ks_ref, vst_ref, qe_ref, sel_ref,
                 kmean_ref, *, tiles_per_seq):
    i = pl.program_id(0)
    h = _rms(x_ref[...], g_ref[...]).astype(BF16)

    def seg(c0, n):
        return _mm(h, w_ref[:, c0:c0 + n])

    q = seg(OFF_QM, MOBA_W).astype(BF16)
    qm_ref[...] = q
    k = seg(OFF_KM, MOBA_W)
    km_ref[...] = k.astype(BF16)
    qs_ref[...] = seg(OFF_QS, SWA_QW).astype(BF16)
    ks_ref[...] = seg(OFF_KS, SWA_K_COLS).astype(BF16)
    qe_ref[...] = seg(OFF_QE, MEM_W).astype(BF16)
    vt = _nt(wt_ref[...], h).astype(BF16)
    for r in range(ROW_TILE // SWA_BLOCK):
        vst_ref[r] = vt[MOBA_W:, r * SWA_BLOCK:(r + 1) * SWA_BLOCK]

    @pl.when(i % tiles_per_seq == 0)
    def _():
        kmean_ref[...] = jnp.zeros_like(kmean_ref)

    blocks_per_tile = ROW_TILE // MOBA_BLOCK
    c0 = (i % tiles_per_seq) * blocks_per_tile
    lane = lax.broadcasted_iota(jnp.int32, (1, MOBA_W), 1)
    for r in range(blocks_per_tile):
        rows = slice(r * MOBA_BLOCK, (r + 1) * MOBA_BLOCK)
        vt_ref[r] = vt[:MOBA_W, rows]
        mean = jnp.sum(k[rows], axis=0, keepdims=True) * (1.0 / MOBA_BLOCK)
        for hd in range(MOBA_HEADS):
            in_head = (lane >= hd * MOBA_HEAD_DIM) & (lane < (hd + 1) * MOBA_HEAD_DIM)
            kmean_ref[pl.ds(hd * MAX_BLOCKS + c0 + r, 1), :] = jnp.where(in_head, mean, 0.0)

    km = kmean_ref[...]
    km_hi = km.astype(BF16)
    km_lo = (km - km_hi.astype(F32)).astype(BF16)
    gates = _nt(km_hi, q) + _nt(km_lo, q)
    blk = lax.broadcasted_iota(jnp.int32, (MAX_BLOCKS, ROW_TILE), 0)
    blk_f = blk.astype(F32)
    col = lax.broadcasted_iota(jnp.int32, (MAX_BLOCKS, ROW_TILE), 1)
    q_blk = c0 + sum((col >= r * MOBA_BLOCK).astype(jnp.int32) for r in range(1, blocks_per_tile))
    past = blk < q_blk
    for hd in range(MOBA_HEADS):
        g = jnp.where(past, gates[hd * MAX_BLOCKS:(hd + 1) * MAX_BLOCKS], -jnp.inf)
        sel = blk < 0
        for _ in range(MOBA_TOPK):
            top = jnp.max(g, axis=0, keepdims=True)
            first = jnp.min(jnp.where(g == top, blk_f, 1e9), axis=0, keepdims=True)
            hit = blk_f == first
            sel = sel | hit
            g = jnp.where(hit, -jnp.inf, g)
        sel_ref[0, hd] = jnp.where(sel & past, 1.0, 0.0)


def _proj(x2, g, w, wt, seq_len):
    t = x2.shape[0]
    n_tiles = t // ROW_TILE
    tiles_per_seq = seq_len // ROW_TILE
    bpt = ROW_TILE // MOBA_BLOCK
    spt = ROW_TILE // SWA_BLOCK
    row = lambda n: pl.BlockSpec((ROW_TILE, n), lambda i: (i, 0))
    out_shapes = (
        jax.ShapeDtypeStruct((t, MOBA_W), BF16),
        jax.ShapeDtypeStruct((t, MOBA_W), BF16),
        jax.ShapeDtypeStruct((t // MOBA_BLOCK, MOBA_W, MOBA_BLOCK), BF16),
        jax.ShapeDtypeStruct((t, SWA_QW), BF16),
        jax.ShapeDtypeStruct((t, SWA_K_COLS), BF16),
        jax.ShapeDtypeStruct((t // SWA_BLOCK, SWA_VT_ROWS, SWA_BLOCK), BF16),
        jax.ShapeDtypeStruct((t, MEM_W), BF16),
        jax.ShapeDtypeStruct((t // seq_len, MOBA_HEADS, MAX_BLOCKS, seq_len), F32),
    )
    out_specs = (row(MOBA_W), row(MOBA_W), pl.BlockSpec((bpt, MOBA_W, MOBA_BLOCK), lambda i: (i, 0, 0)),
                 row(SWA_QW), row(SWA_K_COLS), pl.BlockSpec((spt, SWA_VT_ROWS, SWA_BLOCK), lambda i: (i, 0, 0)),
                 row(MEM_W),
                 pl.BlockSpec((1, MOBA_HEADS, MAX_BLOCKS, ROW_TILE),
                              lambda i: (i // tiles_per_seq, 0, 0, i % tiles_per_seq)))
    return pl.pallas_call(
        functools.partial(_proj_kernel, tiles_per_seq=tiles_per_seq),
        grid=(n_tiles,),
        in_specs=[row(D_MODEL), _resident((1, D_MODEL)), _resident((D_MODEL, PROJ_W)),
                  _resident((MOBA_W + SWA_VT_ROWS, D_MODEL))],
        out_specs=out_specs,
        out_shape=out_shapes,
        scratch_shapes=[pltpu.VMEM((MOBA_HEADS * MAX_BLOCKS, MOBA_W), F32)],
        compiler_params=_cparams(1),
        name="proj",
    )(x2, g, w, wt)


def _memkv_kernel(m_ref, g_ref, wk_ref, wvt_ref, k_ref, vt_ref):
    h = _rms(m_ref[0], g_ref[...]).astype(BF16)
    k_ref[0] = _mm(h, wk_ref[...]).astype(BF16)
    vt_ref[0] = _nt(wvt_ref[...], h).astype(BF16)


def _memkv(mem, g, wk, wvt):
    b, m_len, _ = mem.shape
    return pl.pallas_call(
        _memkv_kernel,
        grid=(b,),
        in_specs=[pl.BlockSpec((1, m_len, D_MODEL), lambda i: (i, 0, 0)), _resident((1, D_MODEL)),
                  _resident((D_MODEL, MEM_W)), _resident((MEM_W, D_MODEL))],
        out_specs=(pl.BlockSpec((1, m_len, MEM_W), lambda i: (i, 0, 0)),
                   pl.BlockSpec((1, MEM_W, m_len), lambda i: (i, 0, 0))),
        out_shape=(jax.ShapeDtypeStruct((b, m_len, MEM_W), BF16), jax.ShapeDtypeStruct((b, MEM_W, m_len), BF16)),
        compiler_params=_cparams(1),
        name="memkv",
    )(mem, g, wk, wvt)


def _moba_kernel(q_ref, k_ref, vt_ref, sel_ref, tab_ref, o_ref, mfar_ref, mtail_ref, s_ref):
    c = pl.program_id(2)
    mb = MOBA_BLOCK
    kb = MOBA_KEYS_PER_STEP
    per = kb // mb
    hd = MOBA_HEAD_DIM
    heads = range(MOBA_HEADS_PER_STEP)
    pair_cols = lambda a: slice((a // 2) * LANES, (a // 2 + 1) * LANES)
    lane_q = lax.broadcasted_iota(jnp.int32, (mb, LANES), 1)
    blk = lax.broadcasted_iota(jnp.int32, (MAX_BLOCKS, mb), 0)
    tb = jnp.maximum(c - 1, 0)
    kt = pl.multiple_of(tb * mb, mb)
    left = jnp.where(c == 0, 1, 0)
    right = jnp.where(c == 0, 2, 1)

    ones = jnp.ones((SUM_ROWS, mb), BF16)

    def values(block, a):
        return jnp.concatenate([vt_ref[0, block, pair_cols(a)], ones], axis=0)

    q_pad = []
    state = []
    tail_max = []
    for a in heads:
        half = a % 2
        qf = q_ref[0, :, pair_cols(a)].astype(F32)
        qa = jnp.where((lane_q >= half * hd) & (lane_q < (half + 1) * hd), qf, 0.0).astype(BF16)
        q_pad.append(qa)
        sel = sel_ref[0, a] > 0.5
        mfar_ref[a] = jnp.where(sel & (blk <= c - 2), 0.0, NEG)
        mtail_ref[a] = jnp.where((blk == c) | (sel & (blk == c - 1)), 0.0, NEG)

        s = _nt(k_ref[0, pl.ds(kt, 2 * mb), pair_cols(a)], qa)
        s = jnp.concatenate([s[:mb] + tab_ref[a, left] + mtail_ref[a, pl.ds(tb, 1), :],
                             s[mb:] + tab_ref[a, right] + mtail_ref[a, pl.ds(tb + 1, 1), :]], axis=0)
        s_ref[0, a] = s
        tail_max.append(jnp.max(s, axis=0, keepdims=True))
        state.append((jnp.full((1, mb), -jnp.inf, F32), jnp.zeros((LANES + SUM_ROWS, mb), F32)))

    n_far = jnp.where(c >= 2, ((c - 1) * mb + kb - 1) // kb, 0)
    last_step = k_ref.shape[1] // kb - 1

    def far_scores(j, slot):
        rows = pl.ds(pl.multiple_of(j * kb, kb), kb)
        out = []
        for a in heads:
            s = _nt(k_ref[0, rows, pair_cols(a)], q_pad[a])
            s = jnp.concatenate([s[t * mb:(t + 1) * mb] + mfar_ref[a, pl.ds(j * per + t, 1), :]
                                 for t in range(per)], axis=0)
            s_ref[slot, a] = s
            out.append(jnp.max(s, axis=0, keepdims=True))
        return tuple(out)

    def absorb(stats, slot, s_max, block):
        out = []
        for a in heads:
            m, acc = stats[a]
            m_new = jnp.maximum(m, s_max[a])
            pb = jnp.exp2(s_ref[slot, a] - m_new).astype(BF16)
            acc = jnp.exp2(m - m_new) * acc
            for t in range(per):
                acc = acc + _mm(values(block + t, a), pb[t * mb:(t + 1) * mb])
            out.append((m_new, acc))
        return tuple(out)

    def steps(count, first):
        def trip(i, carry):
            s_max, stats, block = carry
            for u in range(count):
                j = first + count * i + u
                ahead = far_scores(jnp.minimum(j, last_step), (u + 1) % 2)
                stats = absorb(stats, u % 2, s_max, block)
                s_max, block = ahead, j * per
            return s_max, stats, block
        return trip

    def last_step_alone(_, carry):
        s_max, stats, block = carry
        return s_max, absorb(stats, 0, s_max, block), block

    n_steps = n_far + 1
    carry = (tuple(tail_max), tuple(state), tb)
    done = 0
    for count in MOBA_STEPS_PER_TRIP:
        trips = (n_steps - done) // count
        carry = lax.fori_loop(0, trips, steps(count, done), carry)
        done = done + trips * count
    _, stats, _ = lax.fori_loop(0, n_steps - done, last_step_alone, carry)
    row = lax.broadcasted_iota(jnp.int32, (LANES, mb), 0)
    for p in range(MOBA_HEADS_PER_STEP // 2):
        (_, acc_a), (_, acc_b) = stats[2 * p], stats[2 * p + 1]
        o_t = jnp.where(row < hd, acc_a[:LANES] / acc_a[LANES:LANES + 1], acc_b[:LANES] / acc_b[LANES:LANES + 1])
        o_ref[0, :, p * LANES:(p + 1) * LANES] = o_t.T.astype(BF16)


def _moba(qm, km, vt, sel, tab):
    b, s, _ = qm.shape
    nh = MOBA_HEADS_PER_STEP
    width = nh * MOBA_HEAD_DIM
    nblk = s // MOBA_BLOCK
    grid = (b, MOBA_HEADS // nh, nblk)
    return pl.pallas_call(
        _moba_kernel,
        grid=grid,
        in_specs=[
            pl.BlockSpec((1, MOBA_BLOCK, width), lambda bi, hg, c: (bi, c, hg)),
            pl.BlockSpec((1, s, width), lambda bi, hg, c: (bi, 0, hg)),
            pl.BlockSpec((1, nblk, width, MOBA_BLOCK), lambda bi, hg, c: (bi, 0, hg, 0)),
            pl.BlockSpec((1, nh, MAX_BLOCKS, MOBA_BLOCK), lambda bi, hg, c: (bi, hg, 0, c)),
            pl.BlockSpec((nh, 3, MOBA_BLOCK, MOBA_BLOCK), lambda bi, hg, c: (hg, 0, 0, 0)),
        ],
        out_specs=pl.BlockSpec((1, MOBA_BLOCK, width), lambda bi, hg, c: (bi, c, hg)),
        out_shape=jax.ShapeDtypeStruct((b, s, MOBA_W), BF16),
        scratch_shapes=[pltpu.VMEM((nh, MAX_BLOCKS, MOBA_BLOCK), F32), pltpu.VMEM((nh, MAX_BLOCKS, MOBA_BLOCK), F32),
                        pltpu.VMEM((2, nh, MOBA_KEYS_PER_STEP, MOBA_BLOCK), F32)],
        compiler_params=_cparams(3),
        name="moba",
    )(qm, km, vt, sel, tab)


def _local_kernel(qs_ref, ks_ref, kh_ref, vst_ref, vsh_ref, tab_ref, sink_ref, qe_ref, kme_ref, vmt_ref,
                  osw_ref, ome_ref):
    i = pl.program_id(1)
    qb = SWA_BLOCK
    pairs_per_kv = SWA_Q_HEADS // SWA_KV_HEADS // 2
    no_prev = jnp.where(i == 0, NEG, 0.0)
    row = lax.broadcasted_iota(jnp.int32, (LANES, qb), 0)
    for blk in range(ROW_TILE // qb):
        rows = slice(blk * qb, (blk + 1) * qb)
        for hk in range(SWA_KV_HEADS):
            vrows = slice(hk * LANES, (hk + 1) * LANES)
            v_prev = vsh_ref[0, 0, vrows] if blk == 0 else vst_ref[0, blk - 1, vrows]
            vv = jnp.concatenate([v_prev, vst_ref[0, blk, vrows]], axis=1)
            for jj in range(pairs_per_kv):
                j = hk * pairs_per_kv + jj
                qp = qs_ref[0, rows, j * LANES:(j + 1) * LANES]
                outs = []
                for par in range(2):
                    h = 2 * j + par
                    kc = slice((2 * hk + par) * LANES, (2 * hk + par + 1) * LANES)
                    if blk == 0:
                        kk = jnp.concatenate([kh_ref[0, :, kc], ks_ref[0, 0:qb, kc]], axis=0)
                    else:
                        kk = ks_ref[0, (blk - 1) * qb:(blk + 1) * qb, kc]
                    s = _nt(kk, qp) + tab_ref[h]
                    if blk == 0:
                        s = jnp.concatenate([s[:qb] + no_prev, s[qb:]], axis=0)
                    sink = sink_ref[h:h + 1, :]
                    m = jnp.maximum(jnp.max(s, axis=0, keepdims=True), sink)
                    p = jnp.exp2(s - m)
                    den = jnp.sum(p, axis=0, keepdims=True) + jnp.exp2(sink - m)
                    outs.append(_mm(vv, p.astype(BF16)) / den)
                o_t = jnp.where(row < SWA_HEAD_DIM, outs[0], outs[1])
                osw_ref[0, rows, j * LANES:(j + 1) * LANES] = o_t.T.astype(BF16)

    scale = LOG2E / math.sqrt(MEM_HEAD_DIM)
    for hd in range(MEM_HEADS):
        cols = slice(hd * MEM_HEAD_DIM, (hd + 1) * MEM_HEAD_DIM)
        kk = kme_ref[0, :, cols]
        vv = vmt_ref[0, cols, :]
        for blk in range(ROW_TILE // qb):
            rows = slice(blk * qb, (blk + 1) * qb)
            s = _nt(kk, qe_ref[0, rows, cols]) * scale
            m = jnp.max(s, axis=0, keepdims=True)
            p = jnp.exp2(s - m)
            den = jnp.sum(p, axis=0, keepdims=True)
            ome_ref[0, rows, cols] = (_mm(vv, p.astype(BF16)) / den).T.astype(BF16)


def _local(qs, ks, vst, tab, sinks, qe, kme, vmt):
    b, s, _ = qs.shape
    per_tile = ROW_TILE // SWA_BLOCK
    m_len = kme.shape[1]
    tile = lambda n: pl.BlockSpec((1, ROW_TILE, n), lambda bi, i: (bi, i, 0))
    prev_block = lambda bi, i: (bi, jnp.maximum(i * per_tile - 1, 0), 0)
    return pl.pallas_call(
        _local_kernel,
        grid=(b, s // ROW_TILE),
        in_specs=[
            tile(SWA_QW),
            tile(SWA_K_COLS), pl.BlockSpec((1, SWA_BLOCK, SWA_K_COLS), prev_block),
            pl.BlockSpec((1, per_tile, SWA_VT_ROWS, SWA_BLOCK), lambda bi, i: (bi, i, 0, 0)),
            pl.BlockSpec((1, 1, SWA_VT_ROWS, SWA_BLOCK), lambda bi, i: prev_block(bi, i) + (0,)),
            _resident((SWA_Q_HEADS, 2 * SWA_BLOCK, SWA_BLOCK)),
            _resident((SWA_Q_HEADS, LANES)),
            tile(MEM_W),
            pl.BlockSpec((1, m_len, MEM_W), lambda bi, i: (bi, 0, 0)),
            pl.BlockSpec((1, MEM_W, m_len), lambda bi, i: (bi, 0, 0)),
        ],
        out_specs=(tile(SWA_QW), tile(MEM_W)),
        out_shape=(jax.ShapeDtypeStruct((b, s, SWA_QW), BF16), jax.ShapeDtypeStruct((b, s, MEM_W), BF16)),
        compiler_params=_cparams(2),
        name="local",
    )(qs, ks, ks, vst, vst, tab, sinks, qe, kme, vmt)


def _merge_kernel(x_ref, omb_ref, osw_ref, ome_ref, gpre_ref, gpost_ref, wg_ref, wb_ref, wo_ref, out_ref):
    x = x_ref[...]
    h = _rms(x, gpre_ref[...]).astype(BF16)
    branch = (omb_ref[...], osw_ref[...], ome_ref[...])
    half = D_MODEL // 2
    parts = []
    for nc in range(2):
        acc = None
        for br in range(3):
            c0 = br * D_MODEL + nc * half
            gate = jax.nn.sigmoid(_mm(h, wg_ref[:, c0:c0 + half]))
            term = gate * _mm(branch[br], wb_ref[br, :, nc * half:(nc + 1) * half])
            acc = term if acc is None else acc + term
        parts.append(acc.astype(BF16))
    y = _mm(jnp.concatenate(parts, axis=1), wo_ref[...])
    out_ref[...] = x + _rms(y, gpost_ref[...])


def _merge(x2, omb, osw, ome, gpre, gpost, wg, wb, wo):
    t = x2.shape[0]
    row = lambda n: pl.BlockSpec((ROW_TILE, n), lambda i: (i, 0))
    return pl.pallas_call(
        _merge_kernel,
        grid=(t // ROW_TILE,),
        in_specs=[row(D_MODEL), row(MOBA_W), row(SWA_QW), row(MEM_W),
                  _resident((1, D_MODEL)), _resident((1, D_MODEL)),
                  _resident((D_MODEL, 3 * D_MODEL)), _resident((3, MOBA_W, D_MODEL)),
                  _resident((D_MODEL, D_MODEL))],
        out_specs=row(D_MODEL),
        out_shape=jax.ShapeDtypeStruct((t, D_MODEL), F32),
        compiler_params=_cparams(1),
        name="merge",
    )(x2, omb, osw, ome, gpre, gpost, wg, wb, wo)


def _ffn_kernel(x_ref, xh_ref, gpre_ref, gpost_ref, wup_ref, cw_ref, cb_ref, wd_ref, out_ref, acc_ref, act_ref,
                *, tiles_per_seq):
    i = pl.program_id(0)
    x = x_ref[...]
    xe = jnp.concatenate([xh_ref[...], x], axis=0)
    he = _rms(xe, gpre_ref[...])
    row = lax.broadcasted_iota(jnp.int32, (ROW_TILE + HALO, 1), 0)
    seq_start = (i % tiles_per_seq) == 0
    he = jnp.where((row < HALO) & seq_start, 0.0, he).astype(BF16)

    def up(fc):
        c0 = fc * FFN_CHUNK
        return (_mm(he, wup_ref[:, c0:c0 + FFN_CHUNK]),
                _mm(he, wup_ref[:, FFN_HIDDEN + c0:FFN_HIDDEN + c0 + FFN_CHUNK]))

    def conv(u, c0):
        w = cw_ref[:, c0:c0 + FFN_CHUNK]
        out = cb_ref[:, c0:c0 + FFN_CHUNK] + w[0:1] * pltpu.roll(u, 2, 0)
        out = out + w[1:2] * pltpu.roll(u, 1, 0)
        out = out + w[2:3] * u
        return out[HALO:]

    n_chunks = FFN_HIDDEN // FFN_CHUNK
    u_gate, u_val = up(0)
    for fc in range(n_chunks):
        c0 = fc * FFN_CHUNK
        ahead = up(fc + 1) if fc + 1 < n_chunks else None
        group, k = divmod(fc, FFN_GROUP)
        act = jax.nn.gelu(conv(u_gate, c0), approximate=True) * conv(u_val, FFN_HIDDEN + c0)
        act_ref[group % 2, :, k * FFN_CHUNK:(k + 1) * FFN_CHUNK] = act.astype(BF16)
        if k == FFN_GROUP - 1 or ahead is None:
            g0 = group * FFN_GROUP * FFN_CHUNK
            width = (k + 1) * FFN_CHUNK
            part = _mm(act_ref[group % 2, :, :width], wd_ref[g0:g0 + width, :])
            if group == 0:
                acc_ref[...] = part
            else:
                acc_ref[...] += part
        if ahead is not None:
            u_gate, u_val = ahead
    out_ref[...] = x + _rms(acc_ref[...], gpost_ref[...])


def _ffn(x2, gpre, gpost, wup, cw, cb, wd, seq_len):
    t = x2.shape[0]
    row = pl.BlockSpec((ROW_TILE, D_MODEL), lambda i: (i, 0))
    halo = pl.BlockSpec((HALO, D_MODEL), lambda i: (jnp.maximum(i * (ROW_TILE // HALO) - 1, 0), 0))
    return pl.pallas_call(
        functools.partial(_ffn_kernel, tiles_per_seq=seq_len // ROW_TILE),
        grid=(t // ROW_TILE,),
        in_specs=[row, halo, _resident((1, D_MODEL)), _resident((1, D_MODEL)),
                  _resident((D_MODEL, 2 * FFN_HIDDEN)), _resident((CONV_WIDTH, 2 * FFN_HIDDEN)),
                  _resident((1, 2 * FFN_HIDDEN)), _resident((FFN_HIDDEN, D_MODEL))],
        out_specs=row,
        out_shape=jax.ShapeDtypeStruct((t, D_MODEL), F32),
        scratch_shapes=[pltpu.VMEM((ROW_TILE, D_MODEL), F32),
                        pltpu.VMEM((2, ROW_TILE, FFN_GROUP * FFN_CHUNK), BF16)],
        compiler_params=_cparams(1),
        name="ffn",
    )(x2, x2, gpre, gpost, wup, cw, cb, wd)


def _t5_bucket(dist):
    n = jnp.maximum(dist, 0)
    nf = jnp.maximum(n, 1).astype(F32)
    large = MAX_EXACT + (jnp.log(nf / MAX_EXACT) / math.log(REL_MAX_DISTANCE / MAX_EXACT)
                         * (NUM_BUCKETS - MAX_EXACT)).astype(jnp.int32)
    large = jnp.minimum(large, NUM_BUCKETS - 1)
    return jnp.where(n < MAX_EXACT, n, large)


def _bias_tables(rel_bias):
    bm = rel_bias[:, :MOBA_HEADS].T.astype(F32)
    bs = rel_bias[:, MOBA_HEADS:].T.astype(F32)

    def lookup(table, dist):
        bucket = _t5_bucket(dist)[None]
        out = jnp.zeros((table.shape[0],) + dist.shape, F32)
        for bkt in range(NUM_BUCKETS):
            out = jnp.where(bucket == bkt, table[:, bkt][:, None, None], out)
        return out

    kj = jnp.arange(MOBA_BLOCK)[:, None]
    qi = jnp.arange(MOBA_BLOCK)[None, :]
    far = bm[:, NUM_BUCKETS - 1][:, None, None]
    own = jnp.where((qi - kj) >= 0, (lookup(bm, qi - kj) - far) * LOG2E, NEG)
    prev = (lookup(bm, MOBA_BLOCK + qi - kj) - far) * LOG2E
    tab_moba = jnp.stack([prev, own, jnp.full_like(own, NEG)], axis=1)
    kj = jnp.arange(2 * SWA_BLOCK)[:, None]
    qi = jnp.arange(SWA_BLOCK)[None, :]
    dist = SWA_BLOCK + qi - kj
    tab_swa = jnp.where((dist >= 0) & (dist < SWA_WINDOW), lookup(bs, dist) * LOG2E, NEG)
    return tab_moba, tab_swa


def _head_variants(w, heads):
    w3 = w.reshape(w.shape[0], heads, 1, -1)
    z = jnp.zeros_like(w3)
    lo = jnp.concatenate([w3, z], axis=3)
    hi = jnp.concatenate([z, w3], axis=3)
    return jnp.concatenate([lo, hi], axis=2).reshape(w.shape[0], heads * 2 * LANES)


def _proj_weight(w_in):
    o = 0
    parts = {}
    for name, width in (("qm", MOBA_W), ("km", MOBA_W), ("vm", MOBA_W), ("qs", SWA_QW), ("ks", SWA_KVW),
                        ("vs", SWA_KVW), ("qe", MEM_W)):
        parts[name] = w_in[:, o:o + width]
        o += width
    q_scale = LOG2E / math.sqrt(MOBA_HEAD_DIM)
    w = jnp.concatenate([parts["qm"] * q_scale, parts["km"], parts["qs"] * q_scale,
                         _head_variants(parts["ks"], SWA_KV_HEADS), parts["qe"]], axis=1)
    vs3 = parts["vs"].reshape(-1, SWA_KV_HEADS, SWA_HEAD_DIM)
    vs_dup = jnp.concatenate([vs3, vs3], axis=2).reshape(-1, SWA_VT_ROWS)
    w_t = jnp.concatenate([parts["vm"], vs_dup], axis=1).T
    return w.astype(BF16), w_t.astype(BF16), w_in[:, o:].astype(BF16)


def kernel(x, mem, norm_mix_pre, norm_mix_post, norm_ffn_pre, norm_ffn_post, norm_mem, w_in, rel_bias, swa_sinks,
           w_mem_kv, w_branch_moba, w_branch_swa, w_branch_mem, w_out, w_ffn_up, ffn_conv_w, ffn_conv_b,
           w_ffn_down):
    b, s, d = x.shape
    m_len = mem.shape[1]
    assert d == D_MODEL and s % ROW_TILE == 0 and s // MOBA_BLOCK <= 32 and m_len % LANES == 0
    depth = w_in.shape[0]
    tab_moba, tab_swa = _bias_tables(rel_bias)
    x2 = x.reshape(b * s, d).astype(F32)
    vec = lambda v: v.reshape(1, -1).astype(F32)
    nblk = s // MOBA_BLOCK
    r3 = lambda a: a.reshape(b, s, a.shape[-1])
    for l in range(depth):
        w_qkv, w_t, w_gate = _proj_weight(w_in[l])
        qm, km, vt, qs, ks, vst, qe, sel = _proj(x2, vec(norm_mix_pre[l]), w_qkv, w_t, s)
        kme, vmt = _memkv(mem.astype(F32), vec(norm_mem[l]), w_mem_kv[l][:, :MEM_W].astype(BF16),
                          w_mem_kv[l][:, MEM_W:].T.astype(BF16))
        o_mb = _moba(r3(qm), r3(km), vt.reshape(b, nblk, MOBA_W, MOBA_BLOCK), sel, tab_moba)
        sinks = jnp.broadcast_to(swa_sinks[l].astype(F32)[:, None] * LOG2E, (SWA_Q_HEADS, LANES))
        o_sw, o_me = _local(r3(qs), r3(ks), vst.reshape(b, s // SWA_BLOCK, SWA_VT_ROWS, SWA_BLOCK), tab_swa, sinks,
                            r3(qe), kme, vmt)
        w_branch = jnp.stack([w_branch_moba[l], w_branch_swa[l], w_branch_mem[l]]).astype(BF16)
        x2 = _merge(x2, o_mb.reshape(b * s, -1), o_sw.reshape(b * s, -1), o_me.reshape(b * s, -1),
                    vec(norm_mix_pre[l]), vec(norm_mix_post[l]), w_gate, w_branch, w_out[l].astype(BF16))
        x2 = _ffn(x2, vec(norm_ffn_pre[l]), vec(norm_ffn_post[l]), w_ffn_up[l].astype(BF16),
                  ffn_conv_w[l].astype(F32), vec(ffn_conv_b[l]), w_ffn_down[l].astype(BF16), s)
    return x2.reshape(b, s, d).astype(x.dtype)
```

```python
import functools
import math

import jax
import jax.numpy as jnp
from jax import lax
from jax.experimental import pallas as pl
from jax.experimental.pallas import tpu as pltpu

D_MODEL = 1024
MOBA_HEADS = 8
MOBA_HEAD_DIM = 64
MOBA_BLOCK = 256
MOBA_TOPK = 3
SWA_Q_HEADS = 8
SWA_KV_HEADS = 2
SWA_HEAD_DIM = 64
SWA_WINDOW = 128
SWA_BLOCK = 128
MEM_HEADS = 4
MEM_HEAD_DIM = 128
NUM_BUCKETS = 32
MAX_EXACT = NUM_BUCKETS // 2
REL_MAX_DISTANCE = 128
FFN_HIDDEN = 2816
CONV_WIDTH = 3
RMS_EPS = 1e-6

MOBA_W = MOBA_HEADS * MOBA_HEAD_DIM
SWA_QW = SWA_Q_HEADS * SWA_HEAD_DIM
SWA_KVW = SWA_KV_HEADS * SWA_HEAD_DIM
MEM_W = MEM_HEADS * MEM_HEAD_DIM

LANES = 128
NEG = -1e30
VMEM_LIMIT = 56 * 1024 * 1024
ROW_TILE = 512
MOBA_KEYS_PER_STEP = 512
MOBA_STEPS_PER_TRIP = (4, 2)
MOBA_HEADS_PER_STEP = 8
FFN_CHUNK = 256
FFN_GROUP = 4
HALO = 8
SUM_ROWS = 16
MAX_BLOCKS = 32

BF16 = jnp.bfloat16
F32 = jnp.float32

SWA_K_COLS = 2 * SWA_KV_HEADS * LANES
SWA_VT_ROWS = SWA_KV_HEADS * LANES
OFF_QM = 0
OFF_KM = OFF_QM + MOBA_W
OFF_QS = OFF_KM + MOBA_W
OFF_KS = OFF_QS + SWA_QW
OFF_QE = OFF_KS + SWA_K_COLS
PROJ_W = OFF_QE + MEM_W
LOG2E = math.log2(math.e)


def _mm(a, b):
    return jnp.dot(a, b, preferred_element_type=F32)


def _nt(a, b):
    return lax.dot_general(a, b, (((1,), (1,)), ((), ())), preferred_element_type=F32)


def _rms(xf, g):
    r = lax.rsqrt(jnp.mean(xf * xf, axis=-1, keepdims=True) + RMS_EPS)
    return xf * r * g


def _cparams(n_axes):
    return pltpu.CompilerParams(dimension_semantics=("arbitrary",) * n_axes, vmem_limit_bytes=VMEM_LIMIT)


def _resident(shape):
    nd = len(shape)
    return pl.BlockSpec(shape, lambda *_: (0,) * nd, pipeline_mode=pl.Buffered(1))


def _proj_kernel(x_ref, g_ref, w_ref, wt_ref, qm_ref, km_ref, vt_ref, qs_ref, ks_ref, vst_ref, qe_ref, sel_ref,
                 kmean_ref, *, tiles_per_seq):
    i = pl.program_id(0)
    h = _rms(x_ref[...], g_ref[...]).astype(BF16)

    def seg(c0, n):
        return _mm(h, w_ref[:, c0:c0 + n])

    q = seg(OFF_QM, MOBA_W).astype(BF16)
    qm_ref[...] = q
    k = seg(OFF_KM, MOBA_W)
    km_ref[...] = k.astype(BF16)
    qs_ref[...] = seg(OFF_QS, SWA_QW).astype(BF16)
    ks_ref[...] = seg(OFF_KS, SWA_K_COLS).astype(BF16)
    qe_ref[...] = seg(OFF_QE, MEM_W).astype(BF16)
    vt = _nt(wt_ref[...], h).astype(BF16)
    for r in range(ROW_TILE // SWA_BLOCK):
        vst_ref[r] = vt[MOBA_W:, r * SWA_BLOCK:(r + 1) * SWA_BLOCK]

    @pl.when(i % tiles_per_seq == 0)
    def _():
        kmean_ref[...] = jnp.zeros_like(kmean_ref)

    blocks_per_tile = ROW_TILE // MOBA_BLOCK
    c0 = (i % tiles_per_seq) * blocks_per_tile
    lane = lax.broadcasted_iota(jnp.int32, (1, MOBA_W), 1)
    for r in range(blocks_per_tile):
        rows = slice(r * MOBA_BLOCK, (r + 1) * MOBA_BLOCK)
        vt_ref[r] = vt[:MOBA_W, rows]
        mean = jnp.sum(k[rows], axis=0, keepdims=True) * (1.0 / MOBA_BLOCK)
        for hd in range(MOBA_HEADS):
            in_head = (lane >= hd * MOBA_HEAD_DIM) & (lane < (hd + 1) * MOBA_HEAD_DIM)
            kmean_ref[pl.ds(hd * MAX_BLOCKS + c0 + r, 1), :] = jnp.where(in_head, mean, 0.0)

    km = kmean_ref[...]
    km_hi = km.astype(BF16)
    km_lo = (km - km_hi.astype(F32)).astype(BF16)
    gates = _nt(km_hi, q) + _nt(km_lo, q)
    blk = lax.broadcasted_iota(jnp.int32, (MAX_BLOCKS, ROW_TILE), 0)
    blk_f = blk.astype(F32)
    col = lax.broadcasted_iota(jnp.int32, (MAX_BLOCKS, ROW_TILE), 1)
    q_blk = c0 + sum((col >= r * MOBA_BLOCK).astype(jnp.int32) for r in range(1, blocks_per_tile))
    past = blk < q_blk
    for hd in range(MOBA_HEADS):
        g = jnp.where(past, gates[hd * MAX_BLOCKS:(hd + 1) * MAX_BLOCKS], -jnp.inf)
        sel = blk < 0
        for _ in range(MOBA_TOPK):
            top = jnp.max(g, axis=0, keepdims=True)
            first = jnp.min(jnp.where(g == top, blk_f, 1e9), axis=0, keepdims=True)
            hit = blk_f == first
            sel = sel | hit
            g = jnp.where(hit, -jnp.inf, g)
        sel_ref[0, hd] = jnp.where(sel & past, 1.0, 0.0)


def _proj(x2, g, w, wt, seq_len):
    t = x2.shape[0]
    n_tiles = t // ROW_TILE
    tiles_per_seq = seq_len // ROW_TILE
    bpt = ROW_TILE // MOBA_BLOCK
    spt = ROW_TILE // SWA_BLOCK
    row = lambda n: pl.BlockSpec((ROW_TILE, n), lambda i: (i, 0))
    out_shapes = (
        jax.ShapeDtypeStruct((t, MOBA_W), BF16),
        jax.ShapeDtypeStruct((t, MOBA_W), BF16),
        jax.ShapeDtypeStruct((t // MOBA_BLOCK, MOBA_W, MOBA_BLOCK), BF16),
        jax.ShapeDtypeStruct((t, SWA_QW), BF16),
        jax.ShapeDtypeStruct((t, SWA_K_COLS), BF16),
        jax.ShapeDtypeStruct((t // SWA_BLOCK, SWA_VT_ROWS, SWA_BLOCK), BF16),
        jax.ShapeDtypeStruct((t, MEM_W), BF16),
        jax.ShapeDtypeStruct((t // seq_len, MOBA_HEADS, MAX_BLOCKS, seq_len), F32),
    )
    out_specs = (row(MOBA_W), row(MOBA_W), pl.BlockSpec((bpt, MOBA_W, MOBA_BLOCK), lambda i: (i, 0, 0)),
                 row(SWA_QW), row(SWA_K_COLS), pl.BlockSpec((spt, SWA_VT_ROWS, SWA_BLOCK), lambda i: (i, 0, 0)),
                 row(MEM_W),
                 pl.BlockSpec((1, MOBA_HEADS, MAX_BLOCKS, ROW_TILE),
                              lambda i: (i // tiles_per_seq, 0, 0, i % tiles_per_seq)))
    return pl.pallas_call(
        functools.partial(_proj_kernel, tiles_per_seq=tiles_per_seq),
        grid=(n_tiles,),
        in_specs=[row(D_MODEL), _resident((1, D_MODEL)), _resident((D_MODEL, PROJ_W)),
                  _resident((MOBA_W + SWA_VT_ROWS, D_MODEL))],
        out_specs=out_specs,
        out_shape=out_shapes,
        scratch_shapes=[pltpu.VMEM((MOBA_HEADS * MAX_BLOCKS, MOBA_W), F32)],
        compiler_params=_cparams(1),
        name="proj",
    )(x2, g, w, wt)


def _memkv_kernel(m_ref, g_ref, wk_ref, wvt_ref, k_ref, vt_ref):
    h = _rms(m_ref[0], g_ref[...]).astype(BF16)
    k_ref[0] = _mm(h, wk_ref[...]).astype(BF16)
    vt_ref[0] = _nt(wvt_ref[...], h).astype(BF16)


def _memkv(mem, g, wk, wvt):
    b, m_len, _ = mem.shape
    return pl.pallas_call(
        _memkv_kernel,
        grid=(b,),
        in_specs=[pl.BlockSpec((1, m_len, D_MODEL), lambda i: (i, 0, 0)), _resident((1, D_MODEL)),
                  _resident((D_MODEL, MEM_W)), _resident((MEM_W, D_MODEL))],
        out_specs=(pl.BlockSpec((1, m_len, MEM_W), lambda i: (i, 0, 0)),
                   pl.BlockSpec((1, MEM_W, m_len), lambda i: (i, 0, 0))),
        out_shape=(jax.ShapeDtypeStruct((b, m_len, MEM_W), BF16), jax.ShapeDtypeStruct((b, MEM_W, m_len), BF16)),
        compiler_params=_cparams(1),
        name="memkv",
    )(mem, g, wk, wvt)


def _moba_kernel(q_ref, k_ref, vt_ref, sel_ref, tab_ref, o_ref, mfar_ref, mtail_ref, s_ref):
    c = pl.program_id(2)
    mb = MOBA_BLOCK
    kb = MOBA_KEYS_PER_STEP
    per = kb // mb
    hd = MOBA_HEAD_DIM
    heads = range(MOBA_HEADS_PER_STEP)
    pair_cols = lambda a: slice((a // 2) * LANES, (a // 2 + 1) * LANES)
    lane_q = lax.broadcasted_iota(jnp.int32, (mb, LANES), 1)
    blk = lax.broadcasted_iota(jnp.int32, (MAX_BLOCKS, mb), 0)
    tb = jnp.maximum(c - 1, 0)
    kt = pl.multiple_of(tb * mb, mb)
    left = jnp.where(c == 0, 1, 0)
    right = jnp.where(c == 0, 2, 1)

    ones = jnp.ones((SUM_ROWS, mb), BF16)

    def values(block, a):
        return jnp.concatenate([vt_ref[0, block, pair_cols(a)], ones], axis=0)

    q_pad = []
    state = []
    tail_max = []
    for a in heads:
        half = a % 2
        qf = q_ref[0, :, pair_cols(a)].astype(F32)
        qa = jnp.where((lane_q >= half * hd) & (lane_q < (half + 1) * hd), qf, 0.0).astype(BF16)
        q_pad.append(qa)
        sel = sel_ref[0, a] > 0.5
        mfar_ref[a] = jnp.where(sel & (blk <= c - 2), 0.0, NEG)
        mtail_ref[a] = jnp.where((blk == c) | (sel & (blk == c - 1)), 0.0, NEG)

        s = _nt(k_ref[0, pl.ds(kt, 2 * mb), pair_cols(a)], qa)
        s = jnp.concatenate([s[:mb] + tab_ref[a, left] + mtail_ref[a, pl.ds(tb, 1), :],
                             s[mb:] + tab_ref[a, right] + mtail_ref[a, pl.ds(tb + 1, 1), :]], axis=0)
        s_ref[0, a] = s
        tail_max.append(jnp.max(s, axis=0, keepdims=True))
        state.append((jnp.full((1, mb), -jnp.inf, F32), jnp.zeros((LANES + SUM_ROWS, mb), F32)))

    n_far = jnp.where(c >= 2, ((c - 1) * mb + kb - 1) // kb, 0)
    last_step = k_ref.shape[1] // kb - 1

    def far_scores(j, slot):
        rows = pl.ds(pl.multiple_of(j * kb, kb), kb)
        out = []
        for a in heads:
            s = _nt(k_ref[0, rows, pair_cols(a)], q_pad[a])
            s = jnp.concatenate([s[t * mb:(t + 1) * mb] + mfar_ref[a, pl.ds(j * per + t, 1), :]
                                 for t in range(per)], axis=0)
            s_ref[slot, a] = s
            out.append(jnp.max(s, axis=0, keepdims=True))
        return tuple(out)

    def absorb(stats, slot, s_max, block):
        out = []
        for a in heads:
            m, acc = stats[a]
            m_new = jnp.maximum(m, s_max[a])
            pb = jnp.exp2(s_ref[slot, a] - m_new).astype(BF16)
            acc = jnp.exp2(m - m_new) * acc
            for t in range(per):
                acc = acc + _mm(values(block + t, a), pb[t * mb:(t + 1) * mb])
            out.append((m_new, acc))
        return tuple(out)

    def steps(count, first):
        def trip(i, carry):
            s_max, stats, block = carry
            for u in range(count):
                j = first + count * i + u
                ahead = far_scores(jnp.minimum(j, last_step), (u + 1) % 2)
                stats = absorb(stats, u % 2, s_max, block)
                s_max, block = ahead, j * per
            return s_max, stats, block
        return trip

    def last_step_alone(_, carry):
        s_max, stats, block = carry
        return s_max, absorb(stats, 0, s_max, block), block

    n_steps = n_far + 1
    carry = (tuple(tail_max), tuple(state), tb)
    done = 0
    for count in MOBA_STEPS_PER_TRIP:
        trips = (n_steps - done) // count
        carry = lax.fori_loop(0, trips, steps(count, done), carry)
        done = done + trips * count
    _, stats, _ = lax.fori_loop(0, n_steps - done, last_step_alone, carry)
    row = lax.broadcasted_iota(jnp.int32, (LANES, mb), 0)
    for p in range(MOBA_HEADS_PER_STEP // 2):
        (_, acc_a), (_, acc_b) = stats[2 * p], stats[2 * p + 1]
        o_t = jnp.where(row < hd, acc_a[:LANES] / acc_a[LANES:LANES + 1], acc_b[:LANES] / acc_b[LANES:LANES + 1])
        o_ref[0, :, p * LANES:(p + 1) * LANES] = o_t.T.astype(BF16)


def _moba(qm, km, vt, sel, tab):
    b, s, _ = qm.shape
    nh = MOBA_HEADS_PER_STEP
    width = nh * MOBA_HEAD_DIM
    nblk = s // MOBA_BLOCK
    grid = (b, MOBA_HEADS // nh, nblk)
    return pl.pallas_call(
        _moba_kernel,
        grid=grid,
        in_specs=[
            pl.BlockSpec((1, MOBA_BLOCK, width), lambda bi, hg, c: (bi, c, hg)),
            pl.BlockSpec((1, s, width), lambda bi, hg, c: (bi, 0, hg)),
            pl.BlockSpec((1, nblk, width, MOBA_BLOCK), lambda bi, hg, c: (bi, 0, hg, 0)),
            pl.BlockSpec((1, nh, MAX_BLOCKS, MOBA_BLOCK), lambda bi, hg, c: (bi, hg, 0, c)),
            pl.BlockSpec((nh, 3, MOBA_BLOCK, MOBA_BLOCK), lambda bi, hg, c: (hg, 0, 0, 0),
                         pipeline_mode=pl.Buffered(1 if nh == MOBA_HEADS else 2)),
        ],
        out_specs=pl.BlockSpec((1, MOBA_BLOCK, width), lambda bi, hg, c: (bi, c, hg)),
        out_shape=jax.ShapeDtypeStruct((b, s, MOBA_W), BF16),
        scratch_shapes=[pltpu.VMEM((nh, MAX_BLOCKS, MOBA_BLOCK), F32), pltpu.VMEM((nh, MAX_BLOCKS, MOBA_BLOCK), F32),
                        pltpu.VMEM((2, nh, MOBA_KEYS_PER_STEP, MOBA_BLOCK), F32)],
        compiler_params=_cparams(3),
        name="moba",
    )(qm, km, vt, sel, tab)


def _local_kernel(qs_ref, ks_ref, kh_ref, vst_ref, vsh_ref, tab_ref, sink_ref, qe_ref, kme_ref, vmt_ref,
                  osw_ref, ome_ref):
    i = pl.program_id(1)
    qb = SWA_BLOCK
    pairs_per_kv = SWA_Q_HEADS // SWA_KV_HEADS // 2
    no_prev = jnp.where(i == 0, NEG, 0.0)
    row = lax.broadcasted_iota(jnp.int32, (LANES, qb), 0)
    for blk in range(ROW_TILE // qb):
        rows = slice(blk * qb, (blk + 1) * qb)
        for hk in range(SWA_KV_HEADS):
            vrows = slice(hk * LANES, (hk + 1) * LANES)
            v_prev = vsh_ref[0, 0, vrows] if blk == 0 else vst_ref[0, blk - 1, vrows]
            vv = jnp.concatenate([v_prev, vst_ref[0, blk, vrows]], axis=1)
            for jj in range(pairs_per_kv):
                j = hk * pairs_per_kv + jj
                qp = qs_ref[0, rows, j * LANES:(j + 1) * LANES]
                outs = []
                for par in range(2):
                    h = 2 * j + par
                    kc = slice((2 * hk + par) * LANES, (2 * hk + par + 1) * LANES)
                    if blk == 0:
                        kk = jnp.concatenate([kh_ref[0, :, kc], ks_ref[0, 0:qb, kc]], axis=0)
                    else:
                        kk = ks_ref[0, (blk - 1) * qb:(blk + 1) * qb, kc]
                    s = _nt(kk, qp) + tab_ref[h]
                    if blk == 0:
                        s = jnp.concatenate([s[:qb] + no_prev, s[qb:]], axis=0)
                    sink = sink_ref[h:h + 1, :]
                    m = jnp.maximum(jnp.max(s, axis=0, keepdims=True), sink)
                    p = jnp.exp2(s - m)
                    den = jnp.sum(p, axis=0, keepdims=True) + jnp.exp2(sink - m)
                    outs.append(_mm(vv, p.astype(BF16)) / den)
                o_t = jnp.where(row < SWA_HEAD_DIM, outs[0], outs[1])
                osw_ref[0, rows, j * LANES:(j + 1) * LANES] = o_t.T.astype(BF16)

    scale = LOG2E / math.sqrt(MEM_HEAD_DIM)
    for hd in range(MEM_HEADS):
        cols = slice(hd * MEM_HEAD_DIM, (hd + 1) * MEM_HEAD_DIM)
        kk = kme_ref[0, :, cols]
        vv = vmt_ref[0, cols, :]
        for blk in range(ROW_TILE // qb):
            rows = slice(blk * qb, (blk + 1) * qb)
            s = _nt(kk, qe_ref[0, rows, cols]) * scale
            m = jnp.max(s, axis=0, keepdims=True)
            p = jnp.exp2(s - m)
            den = jnp.sum(p, axis=0, keepdims=True)
            ome_ref[0, rows, cols] = (_mm(vv, p.astype(BF16)) / den).T.astype(BF16)


def _local(qs, ks, vst, tab, sinks, qe, kme, vmt):
    b, s, _ = qs.shape
    per_tile = ROW_TILE // SWA_BLOCK
    m_len = kme.shape[1]
    tile = lambda n: pl.BlockSpec((1, ROW_TILE, n), lambda bi, i: (bi, i, 0))
    prev_block = lambda bi, i: (bi, jnp.maximum(i * per_tile - 1, 0), 0)
    return pl.pallas_call(
        _local_kernel,
        grid=(b, s // ROW_TILE),
        in_specs=[
            tile(SWA_QW),
            tile(SWA_K_COLS), pl.BlockSpec((1, SWA_BLOCK, SWA_K_COLS), prev_block),
            pl.BlockSpec((1, per_tile, SWA_VT_ROWS, SWA_BLOCK), lambda bi, i: (bi, i, 0, 0)),
            pl.BlockSpec((1, 1, SWA_VT_ROWS, SWA_BLOCK), lambda bi, i: prev_block(bi, i) + (0,)),
            _resident((SWA_Q_HEADS, 2 * SWA_BLOCK, SWA_BLOCK)),
            _resident((SWA_Q_HEADS, LANES)),
            tile(MEM_W),
            pl.BlockSpec((1, m_len, MEM_W), lambda bi, i: (bi, 0, 0)),
            pl.BlockSpec((1, MEM_W, m_len), lambda bi, i: (bi, 0, 0)),
        ],
        out_specs=(tile(SWA_QW), tile(MEM_W)),
        out_shape=(jax.ShapeDtypeStruct((b, s, SWA_QW), BF16), jax.ShapeDtypeStruct((b, s, MEM_W), BF16)),
        compiler_params=_cparams(2),
        name="local",
    )(qs, ks, ks, vst, vst, tab, sinks, qe, kme, vmt)


def _merge_kernel(x_ref, omb_ref, osw_ref, ome_ref, gpre_ref, gpost_ref, wg_ref, wb_ref, wo_ref, out_ref):
    x = x_ref[...]
    h = _rms(x, gpre_ref[...]).astype(BF16)
    branch = (omb_ref[...], osw_ref[...], ome_ref[...])
    half = D_MODEL // 2
    parts = []
    for nc in range(2):
        acc = None
        for br in range(3):
            c0 = br * D_MODEL + nc * half
            gate = jax.nn.sigmoid(_mm(h, wg_ref[:, c0:c0 + half]))
            term = gate * _mm(branch[br], wb_ref[br, :, nc * half:(nc + 1) * half])
            acc = term if acc is None else acc + term
        parts.append(acc.astype(BF16))
    y = _mm(jnp.concatenate(parts, axis=1), wo_ref[...])
    out_ref[...] = x + _rms(y, gpost_ref[...])


def _merge(x2, omb, osw, ome, gpre, gpost, wg, wb, wo):
    t = x2.shape[0]
    row = lambda n: pl.BlockSpec((ROW_TILE, n), lambda i: (i, 0))
    return pl.pallas_call(
        _merge_kernel,
        grid=(t // ROW_TILE,),
        in_specs=[row(D_MODEL), row(MOBA_W), row(SWA_QW), row(MEM_W),
                  _resident((1, D_MODEL)), _resident((1, D_MODEL)),
                  _resident((D_MODEL, 3 * D_MODEL)), _resident((3, MOBA_W, D_MODEL)),
                  _resident((D_MODEL, D_MODEL))],
        out_specs=row(D_MODEL),
        out_shape=jax.ShapeDtypeStruct((t, D_MODEL), F32),
        compiler_params=_cparams(1),
        name="merge",
    )(x2, omb, osw, ome, gpre, gpost, wg, wb, wo)


def _ffn_kernel(x_ref, xh_ref, gpre_ref, gpost_ref, wup_ref, cw_ref, cb_ref, wd_ref, out_ref, acc_ref, act_ref,
                *, tiles_per_seq):
    i = pl.program_id(0)
    x = x_ref[...]
    xe = jnp.concatenate([xh_ref[...], x], axis=0)
    he = _rms(xe, gpre_ref[...])
    row = lax.broadcasted_iota(jnp.int32, (ROW_TILE + HALO, 1), 0)
    seq_start = (i % tiles_per_seq) == 0
    he = jnp.where((row < HALO) & seq_start, 0.0, he).astype(BF16)

    def up(fc):
        c0 = fc * FFN_CHUNK
        return (_mm(he, wup_ref[:, c0:c0 + FFN_CHUNK]),
                _mm(he, wup_ref[:, FFN_HIDDEN + c0:FFN_HIDDEN + c0 + FFN_CHUNK]))

    def conv(u, c0):
        w = cw_ref[:, c0:c0 + FFN_CHUNK]
        out = cb_ref[:, c0:c0 + FFN_CHUNK] + w[0:1] * pltpu.roll(u, 2, 0)
        out = out + w[1:2] * pltpu.roll(u, 1, 0)
        out = out + w[2:3] * u
        return out[HALO:]

    n_chunks = FFN_HIDDEN // FFN_CHUNK
    u_gate, u_val = up(0)
    for fc in range(n_chunks):
        c0 = fc * FFN_CHUNK
        ahead = up(fc + 1) if fc + 1 < n_chunks else None
        group, k = divmod(fc, FFN_GROUP)
        act = jax.nn.gelu(conv(u_gate, c0), approximate=True) * conv(u_val, FFN_HIDDEN + c0)
        act_ref[group % 2, :, k * FFN_CHUNK:(k + 1) * FFN_CHUNK] = act.astype(BF16)
        if k == FFN_GROUP - 1 or ahead is None:
            g0 = group * FFN_GROUP * FFN_CHUNK
            width = (k + 1) * FFN_CHUNK
            part = _mm(act_ref[group % 2, :, :width], wd_ref[g0:g0 + width, :])
            if group == 0:
                acc_ref[...] = part
            else:
                acc_ref[...] += part
        if ahead is not None:
            u_gate, u_val = ahead
    out_ref[...] = x + _rms(acc_ref[...], gpost_ref[...])


def _ffn(x2, gpre, gpost, wup, cw, cb, wd, seq_len):
    t = x2.shape[0]
    row = pl.BlockSpec((ROW_TILE, D_MODEL), lambda i: (i, 0))
    halo = pl.BlockSpec((HALO, D_MODEL), lambda i: (jnp.maximum(i * (ROW_TILE // HALO) - 1, 0), 0))
    return pl.pallas_call(
        functools.partial(_ffn_kernel, tiles_per_seq=seq_len // ROW_TILE),
        grid=(t // ROW_TILE,),
        in_specs=[row, halo, _resident((1, D_MODEL)), _resident((1, D_MODEL)),
                  _resident((D_MODEL, 2 * FFN_HIDDEN)), _resident((CONV_WIDTH, 2 * FFN_HIDDEN)),
                  _resident((1, 2 * FFN_HIDDEN)), _resident((FFN_HIDDEN, D_MODEL))],
        out_specs=row,
        out_shape=jax.ShapeDtypeStruct((t, D_MODEL), F32),
        scratch_shapes=[pltpu.VMEM((ROW_TILE, D_MODEL), F32),
                        pltpu.VMEM((2, ROW_TILE, FFN_GROUP * FFN_CHUNK), BF16)],
        compiler_params=_cparams(1),
        name="ffn",
    )(x2, x2, gpre, gpost, wup, cw, cb, wd)


def _t5_bucket(dist):
    n = jnp.maximum(dist, 0)
    nf = jnp.maximum(n, 1).astype(F32)
    large = MAX_EXACT + (jnp.log(nf / MAX_EXACT) / math.log(REL_MAX_DISTANCE / MAX_EXACT)
                         * (NUM_BUCKETS - MAX_EXACT)).astype(jnp.int32)
    large = jnp.minimum(large, NUM_BUCKETS - 1)
    return jnp.where(n < MAX_EXACT, n, large)


def _bias_tables(rel_bias):
    bm = rel_bias[:, :MOBA_HEADS].T.astype(F32)
    bs = rel_bias[:, MOBA_HEADS:].T.astype(F32)

    def lookup(table, dist):
        bucket = _t5_bucket(dist)[None]
        out = jnp.zeros((table.shape[0],) + dist.shape, F32)
        for bkt in range(NUM_BUCKETS):
            out = jnp.where(bucket == bkt, table[:, bkt][:, None, None], out)
        return out

    kj = jnp.arange(MOBA_BLOCK)[:, None]
    qi = jnp.arange(MOBA_BLOCK)[None, :]
    far = bm[:, NUM_BUCKETS - 1][:, None, None]
    own = jnp.where((qi - kj) >= 0, (lookup(bm, qi - kj) - far) * LOG2E, NEG)
    prev = (lookup(bm, MOBA_BLOCK + qi - kj) - far) * LOG2E
    tab_moba = jnp.stack([prev, own, jnp.full_like(own, NEG)], axis=1)
    kj = jnp.arange(2 * SWA_BLOCK)[:, None]
    qi = jnp.arange(SWA_BLOCK)[None, :]
    dist = SWA_BLOCK + qi - kj
    tab_swa = jnp.where((dist >= 0) & (dist < SWA_WINDOW), lookup(bs, dist) * LOG2E, NEG)
    return tab_moba, tab_swa


def _head_variants(w, heads):
    w3 = w.reshape(w.shape[0], heads, 1, -1)
    z = jnp.zeros_like(w3)
    lo = jnp.concatenate([w3, z], axis=3)
    hi = jnp.concatenate([z, w3], axis=3)
    return jnp.concatenate([lo, hi], axis=2).reshape(w.shape[0], heads * 2 * LANES)


def _proj_weight(w_in):
    o = 0
    parts = {}
    for name, width in (("qm", MOBA_W), ("km", MOBA_W), ("vm", MOBA_W), ("qs", SWA_QW), ("ks", SWA_KVW),
                        ("vs", SWA_KVW), ("qe", MEM_W)):
        parts[name] = w_in[:, o:o + width]
        o += width
    q_scale = LOG2E / math.sqrt(MOBA_HEAD_DIM)
    w = jnp.concatenate([parts["qm"] * q_scale, parts["km"], parts["qs"] * q_scale,
                         _head_variants(parts["ks"], SWA_KV_HEADS), parts["qe"]], axis=1)
    vs3 = parts["vs"].reshape(-1, SWA_KV_HEADS, SWA_HEAD_DIM)
    vs_dup = jnp.concatenate([vs3, vs3], axis=2).reshape(-1, SWA_VT_ROWS)
    w_t = jnp.concatenate([parts["vm"], vs_dup], axis=1).T
    return w.astype(BF16), w_t.astype(BF16), w_in[:, o:].astype(BF16)


def kernel(x, mem, norm_mix_pre, norm_mix_post, norm_ffn_pre, norm_ffn_post, norm_mem, w_in, rel_bias, swa_sinks,
           w_mem_kv, w_branch_moba, w_branch_swa, w_branch_mem, w_out, w_ffn_up, ffn_conv_w, ffn_conv_b,
           w_ffn_down):
    b, s, d = x.shape
    m_len = mem.shape[1]
    assert d == D_MODEL and s % ROW_TILE == 0 and s // MOBA_BLOCK <= 32 and m_len % LANES == 0
    depth = w_in.shape[0]
    tab_moba, tab_swa = _bias_tables(rel_bias)
    x2 = x.reshape(b * s, d).astype(F32)
    vec = lambda v: v.reshape(1, -1).astype(F32)
    nblk = s // MOBA_BLOCK
    r3 = lambda a: a.reshape(b, s, a.shape[-1])
    for l in range(depth):
        w_qkv, w_t, w_gate = _proj_weight(w_in[l])
        qm, km, vt, qs, ks, vst, qe, sel = _proj(x2, vec(norm_mix_pre[l]), w_qkv, w_t, s)
        kme, vmt = _memkv(mem.astype(F32), vec(norm_mem[l]), w_mem_kv[l][:, :MEM_W].astype(BF16),
                          w_mem_kv[l][:, MEM_W:].T.astype(BF16))
        o_mb = _moba(r3(qm), r3(km), vt.reshape(b, nblk, MOBA_W, MOBA_BLOCK), sel, tab_moba)
        sinks = jnp.broadcast_to(swa_sinks[l].astype(F32)[:, None] * LOG2E, (SWA_Q_HEADS, LANES))
        o_sw, o_me = _local(r3(qs), r3(ks), vst.reshape(b, s // SWA_BLOCK, SWA_VT_ROWS, SWA_BLOCK), tab_swa, sinks,
                            r3(qe), kme, vmt)
        w_branch = jnp.stack([w_branch_moba[l], w_branch_swa[l], w_branch_mem[l]]).astype(BF16)
        x2 = _merge(x2, o_mb.reshape(b * s, -1), o_sw.reshape(b * s, -1), o_me.reshape(b * s, -1),
                    vec(norm_mix_pre[l]), vec(norm_mix_post[l]), w_gate, w_branch, w_out[l].astype(BF16))
        x2 = _ffn(x2, vec(norm_ffn_pre[l]), vec(norm_ffn_post[l]), w_ffn_up[l].astype(BF16),
                  ffn_conv_w[l].astype(F32), vec(ffn_conv_b[l]), w_ffn_down[l].astype(BF16), s)
    return x2.reshape(b, s, d).astype(x.dtype)
```

```python
import functools
import math

import jax
import jax.numpy as jnp
from jax import lax
from jax.experimental import pallas as pl
from jax.experimental.pallas import tpu as pltpu

D_MODEL = 1024
MOBA_HEADS = 8
MOBA_HEAD_DIM = 64
MOBA_BLOCK = 256
MOBA_TOPK = 3
SWA_Q_HEADS = 8
SWA_KV_HEADS = 2
SWA_HEAD_DIM = 64
SWA_WINDOW = 128
SWA_BLOCK = 128
MEM_HEADS = 4
MEM_HEAD_DIM = 128
NUM_BUCKETS = 32
MAX_EXACT = NUM_BUCKETS // 2
REL_MAX_DISTANCE = 128
FFN_HIDDEN = 2816
CONV_WIDTH = 3
RMS_EPS = 1e-6

MOBA_W = MOBA_HEADS * MOBA_HEAD_DIM
SWA_QW = SWA_Q_HEADS * SWA_HEAD_DIM
SWA_KVW = SWA_KV_HEADS * SWA_HEAD_DIM
MEM_W = MEM_HEADS * MEM_HEAD_DIM

LANES = 128
NEG = -1e30
VMEM_LIMIT = 56 * 1024 * 1024
ROW_TILE = 1024
MERGE_TILE = 1024
MOBA_KEYS_PER_STEP = 512
MOBA_STEPS_PER_TRIP = (4, 2)
MOBA_HEADS_PER_STEP = 8
FFN_CHUNK = 256
FFN_GROUP = 4
HALO = 8
SUM_ROWS = 16
MAX_BLOCKS = 32

BF16 = jnp.bfloat16
F32 = jnp.float32

SWA_K_COLS = 2 * SWA_KV_HEADS * LANES
SWA_VT_ROWS = SWA_KV_HEADS * LANES
OFF_QM = 0
OFF_KM = OFF_QM + MOBA_W
OFF_QS = OFF_KM + MOBA_W
OFF_KS = OFF_QS + SWA_QW
OFF_QE = OFF_KS + SWA_K_COLS
PROJ_W = OFF_QE + MEM_W
LOG2E = math.log2(math.e)


def _mm(a, b):
    return jnp.dot(a, b, preferred_element_type=F32)


def _nt(a, b):
    return lax.dot_general(a, b, (((1,), (1,)), ((), ())), preferred_element_type=F32)


def _rms(xf, g):
    r = lax.rsqrt(jnp.mean(xf * xf, axis=-1, keepdims=True) + RMS_EPS)
    return xf * r * g


def _cparams(n_axes):
    return pltpu.CompilerParams(dimension_semantics=("arbitrary",) * n_axes, vmem_limit_bytes=VMEM_LIMIT)


def _resident(shape):
    nd = len(shape)
    return pl.BlockSpec(shape, lambda *_: (0,) * nd, pipeline_mode=pl.Buffered(1))


def _proj_kernel(x_ref, g_ref, w_ref, wt_ref, qm_ref, km_ref, vt_ref, qs_ref, ks_ref, vst_ref, qe_ref, sel_ref,
                 kmean_ref, *, tiles_per_seq):
    i = pl.program_id(0)
    h = _rms(x_ref[...], g_ref[...]).astype(BF16)

    def seg(c0, n):
        return _mm(h, w_ref[:, c0:c0 + n])

    q = seg(OFF_QM, MOBA_W).astype(BF16)
    qm_ref[...] = q
    k = seg(OFF_KM, MOBA_W)
    km_ref[...] = k.astype(BF16)
    qs_ref[...] = seg(OFF_QS, SWA_QW).astype(BF16)
    ks_ref[...] = seg(OFF_KS, SWA_K_COLS).astype(BF16)
    qe_ref[...] = seg(OFF_QE, MEM_W).astype(BF16)
    vt = _nt(wt_ref[...], h).astype(BF16)
    for r in range(ROW_TILE // SWA_BLOCK):
        vst_ref[r] = vt[MOBA_W:, r * SWA_BLOCK:(r + 1) * SWA_BLOCK]

    @pl.when(i % tiles_per_seq == 0)
    def _():
        kmean_ref[...] = jnp.zeros_like(kmean_ref)

    blocks_per_tile = ROW_TILE // MOBA_BLOCK
    c0 = (i % tiles_per_seq) * blocks_per_tile
    lane = lax.broadcasted_iota(jnp.int32, (1, MOBA_W), 1)
    for r in range(blocks_per_tile):
        rows = slice(r * MOBA_BLOCK, (r + 1) * MOBA_BLOCK)
        vt_ref[r] = vt[:MOBA_W, rows]
        mean = jnp.sum(k[rows], axis=0, keepdims=True) * (1.0 / MOBA_BLOCK)
        for hd in range(MOBA_HEADS):
            in_head = (lane >= hd * MOBA_HEAD_DIM) & (lane < (hd + 1) * MOBA_HEAD_DIM)
            kmean_ref[pl.ds(hd * MAX_BLOCKS + c0 + r, 1), :] = jnp.where(in_head, mean, 0.0)

    km = kmean_ref[...]
    km_hi = km.astype(BF16)
    km_lo = (km - km_hi.astype(F32)).astype(BF16)
    gates = _nt(km_hi, q) + _nt(km_lo, q)
    blk = lax.broadcasted_iota(jnp.int32, (MAX_BLOCKS, ROW_TILE), 0)
    blk_f = blk.astype(F32)
    col = lax.broadcasted_iota(jnp.int32, (MAX_BLOCKS, ROW_TILE), 1)
    q_blk = c0 + sum((col >= r * MOBA_BLOCK).astype(jnp.int32) for r in range(1, blocks_per_tile))
    past = blk < q_blk
    for hd in range(MOBA_HEADS):
        g = jnp.where(past, gates[hd * MAX_BLOCKS:(hd + 1) * MAX_BLOCKS], -jnp.inf)
        sel = blk < 0
        for _ in range(MOBA_TOPK):
            top = jnp.max(g, axis=0, keepdims=True)
            first = jnp.min(jnp.where(g == top, blk_f, 1e9), axis=0, keepdims=True)
            hit = blk_f == first
            sel = sel | hit
            g = jnp.where(hit, -jnp.inf, g)
        sel_ref[0, hd] = jnp.where(sel & past, 1.0, 0.0)


def _proj(x2, g, w, wt, seq_len):
    t = x2.shape[0]
    n_tiles = t // ROW_TILE
    tiles_per_seq = seq_len // ROW_TILE
    bpt = ROW_TILE // MOBA_BLOCK
    spt = ROW_TILE // SWA_BLOCK
    row = lambda n: pl.BlockSpec((ROW_TILE, n), lambda i: (i, 0))
    out_shapes = (
        jax.ShapeDtypeStruct((t, MOBA_W), BF16),
        jax.ShapeDtypeStruct((t, MOBA_W), BF16),
        jax.ShapeDtypeStruct((t // MOBA_BLOCK, MOBA_W, MOBA_BLOCK), BF16),
        jax.ShapeDtypeStruct((t, SWA_QW), BF16),
        jax.ShapeDtypeStruct((t, SWA_K_COLS), BF16),
        jax.ShapeDtypeStruct((t // SWA_BLOCK, SWA_VT_ROWS, SWA_BLOCK), BF16),
        jax.ShapeDtypeStruct((t, MEM_W), BF16),
        jax.ShapeDtypeStruct((t // seq_len, MOBA_HEADS, MAX_BLOCKS, seq_len), F32),
    )
    out_specs = (row(MOBA_W), row(MOBA_W), pl.BlockSpec((bpt, MOBA_W, MOBA_BLOCK), lambda i: (i, 0, 0)),
                 row(SWA_QW), row(SWA_K_COLS), pl.BlockSpec((spt, SWA_VT_ROWS, SWA_BLOCK), lambda i: (i, 0, 0)),
                 row(MEM_W),
                 pl.BlockSpec((1, MOBA_HEADS, MAX_BLOCKS, ROW_TILE),
                              lambda i: (i // tiles_per_seq, 0, 0, i % tiles_per_seq)))
    return pl.pallas_call(
        functools.partial(_proj_kernel, tiles_per_seq=tiles_per_seq),
        grid=(n_tiles,),
        in_specs=[row(D_MODEL), _resident((1, D_MODEL)), _resident((D_MODEL, PROJ_W)),
                  _resident((MOBA_W + SWA_VT_ROWS, D_MODEL))],
        out_specs=out_specs,
        out_shape=out_shapes,
        scratch_shapes=[pltpu.VMEM((MOBA_HEADS * MAX_BLOCKS, MOBA_W), F32)],
        compiler_params=_cparams(1),
        name="proj",
    )(x2, g, w, wt)


def _memkv_kernel(m_ref, g_ref, wk_ref, wvt_ref, k_ref, vt_ref):
    h = _rms(m_ref[0], g_ref[...]).astype(BF16)
    k_ref[0] = _mm(h, wk_ref[...]).astype(BF16)
    vt_ref[0] = _nt(wvt_ref[...], h).astype(BF16)


def _memkv(mem, g, wk, wvt):
    b, m_len, _ = mem.shape
    return pl.pallas_call(
        _memkv_kernel,
        grid=(b,),
        in_specs=[pl.BlockSpec((1, m_len, D_MODEL), lambda i: (i, 0, 0)), _resident((1, D_MODEL)),
                  _resident((D_MODEL, MEM_W)), _resident((MEM_W, D_MODEL))],
        out_specs=(pl.BlockSpec((1, m_len, MEM_W), lambda i: (i, 0, 0)),
                   pl.BlockSpec((1, MEM_W, m_len), lambda i: (i, 0, 0))),
        out_shape=(jax.ShapeDtypeStruct((b, m_len, MEM_W), BF16), jax.ShapeDtypeStruct((b, MEM_W, m_len), BF16)),
        compiler_params=_cparams(1),
        name="memkv",
    )(mem, g, wk, wvt)


def _moba_kernel(q_ref, k_ref, vt_ref, sel_ref, tab_ref, o_ref, mfar_ref, mtail_ref, s_ref):
    c = pl.program_id(2)
    mb = MOBA_BLOCK
    kb = MOBA_KEYS_PER_STEP
    per = kb // mb
    hd = MOBA_HEAD_DIM
    heads = range(MOBA_HEADS_PER_STEP)
    pair_cols = lambda a: slice((a // 2) * LANES, (a // 2 + 1) * LANES)
    lane_q = lax.broadcasted_iota(jnp.int32, (mb, LANES), 1)
    blk = lax.broadcasted_iota(jnp.int32, (MAX_BLOCKS, mb), 0)
    tb = jnp.maximum(c - 1, 0)
    kt = pl.multiple_of(tb * mb, mb)
    left = jnp.where(c == 0, 1, 0)
    right = jnp.where(c == 0, 2, 1)

    ones = jnp.ones((SUM_ROWS, mb), BF16)

    def values(block, a):
        return jnp.concatenate([vt_ref[0, block, pair_cols(a)], ones], axis=0)

    q_pad = []
    state = []
    tail_max = []
    for a in heads:
        half = a % 2
        qf = q_ref[0, :, pair_cols(a)].astype(F32)
        qa = jnp.where((lane_q >= half * hd) & (lane_q < (half + 1) * hd), qf, 0.0).astype(BF16)
        q_pad.append(qa)
        sel = sel_ref[0, a] > 0.5
        mfar_ref[a] = jnp.where(sel & (blk <= c - 2), 0.0, NEG)
        mtail_ref[a] = jnp.where((blk == c) | (sel & (blk == c - 1)), 0.0, NEG)

        s = _nt(k_ref[0, pl.ds(kt, 2 * mb), pair_cols(a)], qa)
        s = jnp.concatenate([s[:mb] + tab_ref[a, left] + mtail_ref[a, pl.ds(tb, 1), :],
                             s[mb:] + tab_ref[a, right] + mtail_ref[a, pl.ds(tb + 1, 1), :]], axis=0)
        s_ref[0, a] = s
        tail_max.append(jnp.max(s, axis=0, keepdims=True))
        state.append((jnp.full((1, mb), -jnp.inf, F32), jnp.zeros((LANES + SUM_ROWS, mb), F32)))

    n_far = jnp.where(c >= 2, ((c - 1) * mb + kb - 1) // kb, 0)
    last_step = k_ref.shape[1] // kb - 1

    def far_scores(j, slot):
        rows = pl.ds(pl.multiple_of(j * kb, kb), kb)
        out = []
        for a in heads:
            s = _nt(k_ref[0, rows, pair_cols(a)], q_pad[a])
            s = jnp.concatenate([s[t * mb:(t + 1) * mb] + mfar_ref[a, pl.ds(j * per + t, 1), :]
                                 for t in range(per)], axis=0)
            s_ref[slot, a] = s
            out.append(jnp.max(s, axis=0, keepdims=True))
        return tuple(out)

    def absorb(stats, slot, s_max, block):
        out = []
        for a in heads:
            m, acc = stats[a]
            m_new = jnp.maximum(m, s_max[a])
            pb = jnp.exp2(s_ref[slot, a] - m_new).astype(BF16)
            acc = jnp.exp2(m - m_new) * acc
            for t in range(per):
                acc = acc + _mm(values(block + t, a), pb[t * mb:(t + 1) * mb])
            out.append((m_new, acc))
        return tuple(out)

    def steps(count, first):
        def trip(i, carry):
            s_max, stats, block = carry
            for u in range(count):
                j = first + count * i + u
                ahead = far_scores(jnp.minimum(j, last_step), (u + 1) % 2)
                stats = absorb(stats, u % 2, s_max, block)
                s_max, block = ahead, j * per
            return s_max, stats, block
        return trip

    def last_step_alone(_, carry):
        s_max, stats, block = carry
        return s_max, absorb(stats, 0, s_max, block), block

    n_steps = n_far + 1
    carry = (tuple(tail_max), tuple(state), tb)
    done = 0
    for count in MOBA_STEPS_PER_TRIP:
        trips = (n_steps - done) // count
        carry = lax.fori_loop(0, trips, steps(count, done), carry)
        done = done + trips * count
    _, stats, _ = lax.fori_loop(0, n_steps - done, last_step_alone, carry)
    row = lax.broadcasted_iota(jnp.int32, (LANES, mb), 0)
    for p in range(MOBA_HEADS_PER_STEP // 2):
        (_, acc_a), (_, acc_b) = stats[2 * p], stats[2 * p + 1]
        o_t = jnp.where(row < hd, acc_a[:LANES] / acc_a[LANES:LANES + 1], acc_b[:LANES] / acc_b[LANES:LANES + 1])
        o_ref[0, :, p * LANES:(p + 1) * LANES] = o_t.T.astype(BF16)


def _moba(qm, km, vt, sel, tab):
    b, s, _ = qm.shape
    nh = MOBA_HEADS_PER_STEP
    width = nh * MOBA_HEAD_DIM
    nblk = s // MOBA_BLOCK
    grid = (b, MOBA_HEADS // nh, nblk)
    return pl.pallas_call(
        _moba_kernel,
        grid=grid,
        in_specs=[
            pl.BlockSpec((1, MOBA_BLOCK, width), lambda bi, hg, c: (bi, c, hg)),
            pl.BlockSpec((1, s, width), lambda bi, hg, c: (bi, 0, hg)),
            pl.BlockSpec((1, nblk, width, MOBA_BLOCK), lambda bi, hg, c: (bi, 0, hg, 0)),
            pl.BlockSpec((1, nh, MAX_BLOCKS, MOBA_BLOCK), lambda bi, hg, c: (bi, hg, 0, c)),
            pl.BlockSpec((nh, 3, MOBA_BLOCK, MOBA_BLOCK), lambda bi, hg, c: (hg, 0, 0, 0),
                         pipeline_mode=pl.Buffered(1 if nh == MOBA_HEADS else 2)),
        ],
        out_specs=pl.BlockSpec((1, MOBA_BLOCK, width), lambda bi, hg, c: (bi, c, hg)),
        out_shape=jax.ShapeDtypeStruct((b, s, MOBA_W), BF16),
        scratch_shapes=[pltpu.VMEM((nh, MAX_BLOCKS, MOBA_BLOCK), F32), pltpu.VMEM((nh, MAX_BLOCKS, MOBA_BLOCK), F32),
                        pltpu.VMEM((2, nh, MOBA_KEYS_PER_STEP, MOBA_BLOCK), F32)],
        compiler_params=_cparams(3),
        name="moba",
    )(qm, km, vt, sel, tab)


def _local_kernel(qs_ref, ks_ref, kh_ref, vst_ref, vsh_ref, tab_ref, sink_ref, qe_ref, kme_ref, vmt_ref,
                  osw_ref, ome_ref):
    i = pl.program_id(1)
    qb = SWA_BLOCK
    pairs_per_kv = SWA_Q_HEADS // SWA_KV_HEADS // 2
    no_prev = jnp.where(i == 0, NEG, 0.0)
    row = lax.broadcasted_iota(jnp.int32, (LANES, qb), 0)
    for blk in range(ROW_TILE // qb):
        rows = slice(blk * qb, (blk + 1) * qb)
        for hk in range(SWA_KV_HEADS):
            vrows = slice(hk * LANES, (hk + 1) * LANES)
            v_prev = vsh_ref[0, 0, vrows] if blk == 0 else vst_ref[0, blk - 1, vrows]
            vv = jnp.concatenate([v_prev, vst_ref[0, blk, vrows]], axis=1)
            for jj in range(pairs_per_kv):
                j = hk * pairs_per_kv + jj
                qp = qs_ref[0, rows, j * LANES:(j + 1) * LANES]
                outs = []
                for par in range(2):
                    h = 2 * j + par
                    kc = slice((2 * hk + par) * LANES, (2 * hk + par + 1) * LANES)
                    if blk == 0:
                        kk = jnp.concatenate([kh_ref[0, :, kc], ks_ref[0, 0:qb, kc]], axis=0)
                    else:
                        kk = ks_ref[0, (blk - 1) * qb:(blk + 1) * qb, kc]
                    s = _nt(kk, qp) + tab_ref[h]
                    if blk == 0:
                        s = jnp.concatenate([s[:qb] + no_prev, s[qb:]], axis=0)
                    sink = sink_ref[h:h + 1, :]
                    m = jnp.maximum(jnp.max(s, axis=0, keepdims=True), sink)
                    p = jnp.exp2(s - m)
                    den = jnp.sum(p, axis=0, keepdims=True) + jnp.exp2(sink - m)
                    outs.append(_mm(vv, p.astype(BF16)) / den)
                o_t = jnp.where(row < SWA_HEAD_DIM, outs[0], outs[1])
                osw_ref[0, rows, j * LANES:(j + 1) * LANES] = o_t.T.astype(BF16)

    scale = LOG2E / math.sqrt(MEM_HEAD_DIM)
    for hd in range(MEM_HEADS):
        cols = slice(hd * MEM_HEAD_DIM, (hd + 1) * MEM_HEAD_DIM)
        kk = kme_ref[0, :, cols]
        vv = vmt_ref[0, cols, :]
        for blk in range(ROW_TILE // qb):
            rows = slice(blk * qb, (blk + 1) * qb)
            s = _nt(kk, qe_ref[0, rows, cols]) * scale
            m = jnp.max(s, axis=0, keepdims=True)
            p = jnp.exp2(s - m)
            den = jnp.sum(p, axis=0, keepdims=True)
            ome_ref[0, rows, cols] = (_mm(vv, p.astype(BF16)) / den).T.astype(BF16)


def _local(qs, ks, vst, tab, sinks, qe, kme, vmt):
    b, s, _ = qs.shape
    per_tile = ROW_TILE // SWA_BLOCK
    m_len = kme.shape[1]
    tile = lambda n: pl.BlockSpec((1, ROW_TILE, n), lambda bi, i: (bi, i, 0))
    prev_block = lambda bi, i: (bi, jnp.maximum(i * per_tile - 1, 0), 0)
    return pl.pallas_call(
        _local_kernel,
        grid=(b, s // ROW_TILE),
        in_specs=[
            tile(SWA_QW),
            tile(SWA_K_COLS), pl.BlockSpec((1, SWA_BLOCK, SWA_K_COLS), prev_block),
            pl.BlockSpec((1, per_tile, SWA_VT_ROWS, SWA_BLOCK), lambda bi, i: (bi, i, 0, 0)),
            pl.BlockSpec((1, 1, SWA_VT_ROWS, SWA_BLOCK), lambda bi, i: prev_block(bi, i) + (0,)),
            _resident((SWA_Q_HEADS, 2 * SWA_BLOCK, SWA_BLOCK)),
            _resident((SWA_Q_HEADS, LANES)),
            tile(MEM_W),
            pl.BlockSpec((1, m_len, MEM_W), lambda bi, i: (bi, 0, 0)),
            pl.BlockSpec((1, MEM_W, m_len), lambda bi, i: (bi, 0, 0)),
        ],
        out_specs=(tile(SWA_QW), tile(MEM_W)),
        out_shape=(jax.ShapeDtypeStruct((b, s, SWA_QW), BF16), jax.ShapeDtypeStruct((b, s, MEM_W), BF16)),
        compiler_params=_cparams(2),
        name="local",
    )(qs, ks, ks, vst, vst, tab, sinks, qe, kme, vmt)


def _merge_kernel(x_ref, omb_ref, osw_ref, ome_ref, gpre_ref, gpost_ref, wg_ref, wb_ref, wo_ref, out_ref):
    x = x_ref[...]
    h = _rms(x, gpre_ref[...]).astype(BF16)
    branch = (omb_ref[...], osw_ref[...], ome_ref[...])
    half = D_MODEL // 2
    parts = []
    for nc in range(2):
        acc = None
        for br in range(3):
            c0 = br * D_MODEL + nc * half
            gate = jax.nn.sigmoid(_mm(h, wg_ref[:, c0:c0 + half]))
            term = gate * _mm(branch[br], wb_ref[br, :, nc * half:(nc + 1) * half])
            acc = term if acc is None else acc + term
        parts.append(acc.astype(BF16))
    y = _mm(jnp.concatenate(parts, axis=1), wo_ref[...])
    out_ref[...] = x + _rms(y, gpost_ref[...])


def _merge(x2, omb, osw, ome, gpre, gpost, wg, wb, wo):
    t = x2.shape[0]
    row = lambda n: pl.BlockSpec((MERGE_TILE, n), lambda i: (i, 0))
    return pl.pallas_call(
        _merge_kernel,
        grid=(t // MERGE_TILE,),
        in_specs=[row(D_MODEL), row(MOBA_W), row(SWA_QW), row(MEM_W),
                  _resident((1, D_MODEL)), _resident((1, D_MODEL)),
                  _resident((D_MODEL, 3 * D_MODEL)), _resident((3, MOBA_W, D_MODEL)),
                  _resident((D_MODEL, D_MODEL))],
        out_specs=row(D_MODEL),
        out_shape=jax.ShapeDtypeStruct((t, D_MODEL), F32),
        compiler_params=_cparams(1),
        name="merge",
    )(x2, omb, osw, ome, gpre, gpost, wg, wb, wo)


def _ffn_kernel(x_ref, xh_ref, gpre_ref, gpost_ref, wup_ref, cw_ref, cb_ref, wd_ref, out_ref, acc_ref, act_ref,
                *, tiles_per_seq):
    i = pl.program_id(0)
    x = x_ref[...]
    xe = jnp.concatenate([xh_ref[...], x], axis=0)
    he = _rms(xe, gpre_ref[...])
    row = lax.broadcasted_iota(jnp.int32, (ROW_TILE + HALO, 1), 0)
    seq_start = (i % tiles_per_seq) == 0
    he = jnp.where((row < HALO) & seq_start, 0.0, he).astype(BF16)

    def up(fc):
        c0 = fc * FFN_CHUNK
        return (_mm(he, wup_ref[:, c0:c0 + FFN_CHUNK]),
                _mm(he, wup_ref[:, FFN_HIDDEN + c0:FFN_HIDDEN + c0 + FFN_CHUNK]))

    def conv(u, c0):
        w = cw_ref[:, c0:c0 + FFN_CHUNK]
        out = cb_ref[:, c0:c0 + FFN_CHUNK] + w[0:1] * pltpu.roll(u, 2, 0)
        out = out + w[1:2] * pltpu.roll(u, 1, 0)
        out = out + w[2:3] * u
        return out[HALO:]

    n_chunks = FFN_HIDDEN // FFN_CHUNK
    u_gate, u_val = up(0)
    for fc in range(n_chunks):
        c0 = fc * FFN_CHUNK
        ahead = up(fc + 1) if fc + 1 < n_chunks else None
        group, k = divmod(fc, FFN_GROUP)
        act = jax.nn.gelu(conv(u_gate, c0), approximate=True) * conv(u_val, FFN_HIDDEN + c0)
        act_ref[group % 2, :, k * FFN_CHUNK:(k + 1) * FFN_CHUNK] = act.astype(BF16)
        if k == FFN_GROUP - 1 or ahead is None:
            g0 = group * FFN_GROUP * FFN_CHUNK
            width = (k + 1) * FFN_CHUNK
            part = _mm(act_ref[group % 2, :, :width], wd_ref[g0:g0 + width, :])
            if group == 0:
                acc_ref[...] = part
            else:
                acc_ref[...] += part
        if ahead is not None:
            u_gate, u_val = ahead
    out_ref[...] = x + _rms(acc_ref[...], gpost_ref[...])


def _ffn(x2, gpre, gpost, wup, cw, cb, wd, seq_len):
    t = x2.shape[0]
    row = pl.BlockSpec((ROW_TILE, D_MODEL), lambda i: (i, 0))
    halo = pl.BlockSpec((HALO, D_MODEL), lambda i: (jnp.maximum(i * (ROW_TILE // HALO) - 1, 0), 0))
    return pl.pallas_call(
        functools.partial(_ffn_kernel, tiles_per_seq=seq_len // ROW_TILE),
        grid=(t // ROW_TILE,),
        in_specs=[row, halo, _resident((1, D_MODEL)), _resident((1, D_MODEL)),
                  _resident((D_MODEL, 2 * FFN_HIDDEN)), _resident((CONV_WIDTH, 2 * FFN_HIDDEN)),
                  _resident((1, 2 * FFN_HIDDEN)), _resident((FFN_HIDDEN, D_MODEL))],
        out_specs=row,
        out_shape=jax.ShapeDtypeStruct((t, D_MODEL), F32),
        scratch_shapes=[pltpu.VMEM((ROW_TILE, D_MODEL), F32),
                        pltpu.VMEM((2, ROW_TILE, FFN_GROUP * FFN_CHUNK), BF16)],
        compiler_params=_cparams(1),
        name="ffn",
    )(x2, x2, gpre, gpost, wup, cw, cb, wd)


def _t5_bucket(dist):
    n = jnp.maximum(dist, 0)
    nf = jnp.maximum(n, 1).astype(F32)
    large = MAX_EXACT + (jnp.log(nf / MAX_EXACT) / math.log(REL_MAX_DISTANCE / MAX_EXACT)
                         * (NUM_BUCKETS - MAX_EXACT)).astype(jnp.int32)
    large = jnp.minimum(large, NUM_BUCKETS - 1)
    return jnp.where(n < MAX_EXACT, n, large)


def _bias_tables(rel_bias):
    bm = rel_bias[:, :MOBA_HEADS].T.astype(F32)
    bs = rel_bias[:, MOBA_HEADS:].T.astype(F32)

    def lookup(table, dist):
        bucket = _t5_bucket(dist)[None]
        out = jnp.zeros((table.shape[0],) + dist.shape, F32)
        for bkt in range(NUM_BUCKETS):
            out = jnp.where(bucket == bkt, table[:, bkt][:, None, None], out)
        return out

    kj = jnp.arange(MOBA_BLOCK)[:, None]
    qi = jnp.arange(MOBA_BLOCK)[None, :]
    far = bm[:, NUM_BUCKETS - 1][:, None, None]
    own = jnp.where((qi - kj) >= 0, (lookup(bm, qi - kj) - far) * LOG2E, NEG)
    prev = (lookup(bm, MOBA_BLOCK + qi - kj) - far) * LOG2E
    tab_moba = jnp.stack([prev, own, jnp.full_like(own, NEG)], axis=1)
    kj = jnp.arange(2 * SWA_BLOCK)[:, None]
    qi = jnp.arange(SWA_BLOCK)[None, :]
    dist = SWA_BLOCK + qi - kj
    tab_swa = jnp.where((dist >= 0) & (dist < SWA_WINDOW), lookup(bs, dist) * LOG2E, NEG)
    return tab_moba, tab_swa


def _head_variants(w, heads):
    w3 = w.reshape(w.shape[0], heads, 1, -1)
    z = jnp.zeros_like(w3)
    lo = jnp.concatenate([w3, z], axis=3)
    hi = jnp.concatenate([z, w3], axis=3)
    return jnp.concatenate([lo, hi], axis=2).reshape(w.shape[0], heads * 2 * LANES)


def _proj_weight(w_in):
    o = 0
    parts = {}
    for name, width in (("qm", MOBA_W), ("km", MOBA_W), ("vm", MOBA_W), ("qs", SWA_QW), ("ks", SWA_KVW),
                        ("vs", SWA_KVW), ("qe", MEM_W)):
        parts[name] = w_in[:, o:o + width]
        o += width
    q_scale = LOG2E / math.sqrt(MOBA_HEAD_DIM)
    w = jnp.concatenate([parts["qm"] * q_scale, parts["km"], parts["qs"] * q_scale,
                         _head_variants(parts["ks"], SWA_KV_HEADS), parts["qe"]], axis=1)
    vs3 = parts["vs"].reshape(-1, SWA_KV_HEADS, SWA_HEAD_DIM)
    vs_dup = jnp.concatenate([vs3, vs3], axis=2).reshape(-1, SWA_VT_ROWS)
    w_t = jnp.concatenate([parts["vm"], vs_dup], axis=1).T
    return w.astype(BF16), w_t.astype(BF16), w_in[:, o:].astype(BF16)


def kernel(x, mem, norm_mix_pre, norm_mix_post, norm_ffn_pre, norm_ffn_post, norm_mem, w_in, rel_bias, swa_sinks,
           w_mem_kv, w_branch_moba, w_branch_swa, w_branch_mem, w_out, w_ffn_up, ffn_conv_w, ffn_conv_b,
           w_ffn_down):
    b, s, d = x.shape
    m_len = mem.shape[1]
    assert d == D_MODEL and s % ROW_TILE == 0 and s // MOBA_BLOCK <= 32 and m_len % LANES == 0
    depth = w_in.shape[0]
    tab_moba, tab_swa = _bias_tables(rel_bias)
    x2 = x.reshape(b * s, d).astype(F32)
    vec = lambda v: v.reshape(1, -1).astype(F32)
    nblk = s // MOBA_BLOCK
    r3 = lambda a: a.reshape(b, s, a.shape[-1])
    for l in range(depth):
        w_qkv, w_t, w_gate = _proj_weight(w_in[l])
        qm, km, vt, qs, ks, vst, qe, sel = _proj(x2, vec(norm_mix_pre[l]), w_qkv, w_t, s)
        kme, vmt = _memkv(mem.astype(F32), vec(norm_mem[l]), w_mem_kv[l][:, :MEM_W].astype(BF16),
                          w_mem_kv[l][:, MEM_W:].T.astype(BF16))
        o_mb = _moba(r3(qm), r3(km), vt.reshape(b, nblk, MOBA_W, MOBA_BLOCK), sel, tab_moba)
        sinks = jnp.broadcast_to(swa_sinks[l].astype(F32)[:, None] * LOG2E, (SWA_Q_HEADS, LANES))
        o_sw, o_me = _local(r3(qs), r3(ks), vst.reshape(b, s // SWA_BLOCK, SWA_VT_ROWS, SWA_BLOCK), tab_swa, sinks,
                            r3(qe), kme, vmt)
        w_branch = jnp.stack([w_branch_moba[l], w_branch_swa[l], w_branch_mem[l]]).astype(BF16)
        x2 = _merge(x2, o_mb.reshape(b * s, -1), o_sw.reshape(b * s, -1), o_me.reshape(b * s, -1),
                    vec(norm_mix_pre[l]), vec(norm_mix_post[l]), w_gate, w_branch, w_out[l].astype(BF16))
        x2 = _ffn(x2, vec(norm_ffn_pre[l]), vec(norm_ffn_post[l]), w_ffn_up[l].astype(BF16),
                  ffn_conv_w[l].astype(F32), vec(ffn_conv_b[l]), w_ffn_down[l].astype(BF16), s)
    return x2.reshape(b, s, d).astype(x.dtype)
```

```python
import functools
import math

import jax
import jax.numpy as jnp
from jax import lax
from jax.experimental import pallas as pl
from jax.experimental.pallas import tpu as pltpu

D_MODEL = 1024
MOBA_HEADS = 8
MOBA_HEAD_DIM = 64
MOBA_BLOCK = 256
MOBA_TOPK = 3
SWA_Q_HEADS = 8
SWA_KV_HEADS = 2
SWA_HEAD_DIM = 64
SWA_WINDOW = 128
SWA_BLOCK = 128
MEM_HEADS = 4
MEM_HEAD_DIM = 128
NUM_BUCKETS = 32
MAX_EXACT = NUM_BUCKETS // 2
REL_MAX_DISTANCE = 128
FFN_HIDDEN = 2816
CONV_WIDTH = 3
RMS_EPS = 1e-6

MOBA_W = MOBA_HEADS * MOBA_HEAD_DIM
SWA_QW = SWA_Q_HEADS * SWA_HEAD_DIM
SWA_KVW = SWA_KV_HEADS * SWA_HEAD_DIM
MEM_W = MEM_HEADS * MEM_HEAD_DIM

LANES = 128
NEG = -1e30
VMEM_LIMIT = 56 * 1024 * 1024
ROW_TILE = 1024
MERGE_TILE = 1024
MOBA_KEYS_PER_STEP = 512
MOBA_STEPS_PER_TRIP = (4, 2)
MOBA_HEADS_PER_STEP = 8
FFN_CHUNK = 256
FFN_GROUP = 4
HALO = 8
SUM_ROWS = 16
MAX_BLOCKS = 32

BF16 = jnp.bfloat16
F32 = jnp.float32

SWA_K_COLS = 2 * SWA_KV_HEADS * LANES
SWA_VT_ROWS = SWA_KV_HEADS * LANES
OFF_QM = 0
OFF_KM = OFF_QM + MOBA_W
OFF_QS = OFF_KM + MOBA_W
OFF_KS = OFF_QS + SWA_QW
OFF_QE = OFF_KS + SWA_K_COLS
PROJ_W = OFF_QE + MEM_W
LOG2E = math.log2(math.e)


def _mm(a, b):
    return jnp.dot(a, b, preferred_element_type=F32)


def _nt(a, b):
    return lax.dot_general(a, b, (((1,), (1,)), ((), ())), preferred_element_type=F32)


def _rms(xf, g):
    r = lax.rsqrt(jnp.mean(xf * xf, axis=-1, keepdims=True) + RMS_EPS)
    return xf * r * g


def _cparams(n_axes):
    return pltpu.CompilerParams(dimension_semantics=("arbitrary",) * n_axes, vmem_limit_bytes=VMEM_LIMIT)


def _resident(shape):
    nd = len(shape)
    return pl.BlockSpec(shape, lambda *_: (0,) * nd, pipeline_mode=pl.Buffered(1))


def _proj_kernel(x_ref, g_ref, w_ref, wt_ref, qm_ref, km_ref, vt_ref, qs_ref, ks_ref, vst_ref, qe_ref, sel_ref,
                 kmean_ref, *, tiles_per_seq):
    i = pl.program_id(0)
    h = _rms(x_ref[...], g_ref[...]).astype(BF16)

    def seg(c0, n):
        return _mm(h, w_ref[:, c0:c0 + n])

    q = seg(OFF_QM, MOBA_W).astype(BF16)
    qm_ref[...] = q
    k = seg(OFF_KM, MOBA_W)
    km_ref[...] = k.astype(BF16)
    qs_ref[...] = seg(OFF_QS, SWA_QW).astype(BF16)
    ks_ref[...] = seg(OFF_KS, SWA_K_COLS).astype(BF16)
    qe_ref[...] = seg(OFF_QE, MEM_W).astype(BF16)
    vt = _nt(wt_ref[...], h).astype(BF16)
    for r in range(ROW_TILE // SWA_BLOCK):
        vst_ref[r] = vt[MOBA_W:, r * SWA_BLOCK:(r + 1) * SWA_BLOCK]

    @pl.when(i % tiles_per_seq == 0)
    def _():
        kmean_ref[...] = jnp.zeros_like(kmean_ref)

    blocks_per_tile = ROW_TILE // MOBA_BLOCK
    c0 = (i % tiles_per_seq) * blocks_per_tile
    lane = lax.broadcasted_iota(jnp.int32, (1, MOBA_W), 1)
    for r in range(blocks_per_tile):
        rows = slice(r * MOBA_BLOCK, (r + 1) * MOBA_BLOCK)
        vt_ref[r] = vt[:MOBA_W, rows]
        mean = jnp.sum(k[rows], axis=0, keepdims=True) * (1.0 / MOBA_BLOCK)
        for hd in range(MOBA_HEADS):
            in_head = (lane >= hd * MOBA_HEAD_DIM) & (lane < (hd + 1) * MOBA_HEAD_DIM)
            kmean_ref[pl.ds(hd * MAX_BLOCKS + c0 + r, 1), :] = jnp.where(in_head, mean, 0.0)

    km = kmean_ref[...]
    km_hi = km.astype(BF16)
    km_lo = (km - km_hi.astype(F32)).astype(BF16)
    gates = _nt(km_hi, q) + _nt(km_lo, q)
    blk = lax.broadcasted_iota(jnp.int32, (MAX_BLOCKS, ROW_TILE), 0)
    blk_f = blk.astype(F32)
    col = lax.broadcasted_iota(jnp.int32, (MAX_BLOCKS, ROW_TILE), 1)
    q_blk = c0 + sum((col >= r * MOBA_BLOCK).astype(jnp.int32) for r in range(1, blocks_per_tile))
    past = blk < q_blk
    for hd in range(MOBA_HEADS):
        g = jnp.where(past, gates[hd * MAX_BLOCKS:(hd + 1) * MAX_BLOCKS], -jnp.inf)
        sel = blk < 0
        for _ in range(MOBA_TOPK):
            top = jnp.max(g, axis=0, keepdims=True)
            first = jnp.min(jnp.where(g == top, blk_f, 1e9), axis=0, keepdims=True)
            hit = blk_f == first
            sel = sel | hit
            g = jnp.where(hit, -jnp.inf, g)
        sel_ref[0, hd] = jnp.where(sel & past, 1.0, 0.0)


def _proj(x2, g, w, wt, seq_len):
    t = x2.shape[0]
    n_tiles = t // ROW_TILE
    tiles_per_seq = seq_len // ROW_TILE
    bpt = ROW_TILE // MOBA_BLOCK
    spt = ROW_TILE // SWA_BLOCK
    row = lambda n: pl.BlockSpec((ROW_TILE, n), lambda i: (i, 0))
    out_shapes = (
        jax.ShapeDtypeStruct((t, MOBA_W), BF16),
        jax.ShapeDtypeStruct((t, MOBA_W), BF16),
        jax.ShapeDtypeStruct((t // MOBA_BLOCK, MOBA_W, MOBA_BLOCK), BF16),
        jax.ShapeDtypeStruct((t, SWA_QW), BF16),
        jax.ShapeDtypeStruct((t, SWA_K_COLS), BF16),
        jax.ShapeDtypeStruct((t // SWA_BLOCK, SWA_VT_ROWS, SWA_BLOCK), BF16),
        jax.ShapeDtypeStruct((t, MEM_W), BF16),
        jax.ShapeDtypeStruct((t // seq_len, MOBA_HEADS, MAX_BLOCKS, seq_len), F32),
    )
    out_specs = (row(MOBA_W), row(MOBA_W), pl.BlockSpec((bpt, MOBA_W, MOBA_BLOCK), lambda i: (i, 0, 0)),
                 row(SWA_QW), row(SWA_K_COLS), pl.BlockSpec((spt, SWA_VT_ROWS, SWA_BLOCK), lambda i: (i, 0, 0)),
                 row(MEM_W),
                 pl.BlockSpec((1, MOBA_HEADS, MAX_BLOCKS, ROW_TILE),
                              lambda i: (i // tiles_per_seq, 0, 0, i % tiles_per_seq)))
    return pl.pallas_call(
        functools.partial(_proj_kernel, tiles_per_seq=tiles_per_seq),
        grid=(n_tiles,),
        in_specs=[row(D_MODEL), _resident((1, D_MODEL)), _resident((D_MODEL, PROJ_W)),
                  _resident((MOBA_W + SWA_VT_ROWS, D_MODEL))],
        out_specs=out_specs,
        out_shape=out_shapes,
        scratch_shapes=[pltpu.VMEM((MOBA_HEADS * MAX_BLOCKS, MOBA_W), F32)],
        compiler_params=_cparams(1),
        name="proj",
    )(x2, g, w, wt)


def _memkv_kernel(m_ref, g_ref, wk_ref, wvt_ref, k_ref, vt_ref):
    h = _rms(m_ref[0], g_ref[...]).astype(BF16)
    k_ref[0] = _mm(h, wk_ref[...]).astype(BF16)
    vt_ref[0] = _nt(wvt_ref[...], h).astype(BF16)


def _memkv(mem, g, wk, wvt):
    b, m_len, _ = mem.shape
    return pl.pallas_call(
        _memkv_kernel,
        grid=(b,),
        in_specs=[pl.BlockSpec((1, m_len, D_MODEL), lambda i: (i, 0, 0)), _resident((1, D_MODEL)),
                  _resident((D_MODEL, MEM_W)), _resident((MEM_W, D_MODEL))],
        out_specs=(pl.BlockSpec((1, m_len, MEM_W), lambda i: (i, 0, 0)),
                   pl.BlockSpec((1, MEM_W, m_len), lambda i: (i, 0, 0))),
        out_shape=(jax.ShapeDtypeStruct((b, m_len, MEM_W), BF16), jax.ShapeDtypeStruct((b, MEM_W, m_len), BF16)),
        compiler_params=_cparams(1),
        name="memkv",
    )(mem, g, wk, wvt)


def _moba_kernel(q_ref, k_ref, vt_ref, sel_ref, tab_ref, o_ref, mfar_ref, mtail_ref, s_ref):
    c = pl.program_id(2)
    mb = MOBA_BLOCK
    kb = MOBA_KEYS_PER_STEP
    per = kb // mb
    hd = MOBA_HEAD_DIM
    heads = range(MOBA_HEADS_PER_STEP)
    pair_cols = lambda a: slice((a // 2) * LANES, (a // 2 + 1) * LANES)
    lane_q = lax.broadcasted_iota(jnp.int32, (mb, LANES), 1)
    blk = lax.broadcasted_iota(jnp.int32, (MAX_BLOCKS, mb), 0)
    tb = jnp.maximum(c - 1, 0)
    kt = pl.multiple_of(tb * mb, mb)
    left = jnp.where(c == 0, 1, 0)
    right = jnp.where(c == 0, 2, 1)

    ones = jnp.ones((SUM_ROWS, mb), BF16)

    def values(block, a):
        return jnp.concatenate([vt_ref[0, block, pair_cols(a)], ones], axis=0)

    q_pad = []
    state = []
    tail_max = []
    for a in heads:
        half = a % 2
        qp = q_ref[0, :, pair_cols(a)]
        qa = jnp.where((lane_q >= half * hd) & (lane_q < (half + 1) * hd), qp, jnp.zeros_like(qp))
        q_pad.append(qa)
        sel = sel_ref[0, a] > 0.5
        mfar_ref[a] = jnp.where(sel & (blk <= c - 2), 0.0, NEG)
        mtail_ref[a] = jnp.where((blk == c) | (sel & (blk == c - 1)), 0.0, NEG)

        s = _nt(k_ref[0, pl.ds(kt, 2 * mb), pair_cols(a)], qa)
        s = jnp.concatenate([s[:mb] + tab_ref[a, left] + mtail_ref[a, pl.ds(tb, 1), :],
                             s[mb:] + tab_ref[a, right]], axis=0)
        s_ref[0, a] = s
        tail_max.append(jnp.max(s, axis=0, keepdims=True))
        state.append((jnp.full((1, mb), -jnp.inf, F32), jnp.zeros((LANES + SUM_ROWS, mb), F32)))

    n_far = jnp.where(c >= 2, ((c - 1) * mb + kb - 1) // kb, 0)
    last_step = k_ref.shape[1] // kb - 1

    def far_scores(j, slot):
        rows = pl.ds(pl.multiple_of(j * kb, kb), kb)
        out = []
        for a in heads:
            s = _nt(k_ref[0, rows, pair_cols(a)], q_pad[a])
            s = jnp.concatenate([s[t * mb:(t + 1) * mb] + mfar_ref[a, pl.ds(j * per + t, 1), :]
                                 for t in range(per)], axis=0)
            s_ref[slot, a] = s
            out.append(jnp.max(s, axis=0, keepdims=True))
        return tuple(out)

    def absorb(stats, slot, s_max, block):
        out = []
        for a in heads:
            m, acc = stats[a]
            m_new = jnp.maximum(m, s_max[a])
            pb = jnp.exp2(s_ref[slot, a] - m_new).astype(BF16)
            acc = jnp.exp2(m - m_new) * acc
            for t in range(per):
                acc = acc + _mm(values(block + t, a), pb[t * mb:(t + 1) * mb])
            out.append((m_new, acc))
        return tuple(out)

    def steps(count, first):
        def trip(i, carry):
            s_max, stats, block = carry
            for u in range(count):
                j = first + count * i + u
                ahead = far_scores(jnp.minimum(j, last_step), (u + 1) % 2)
                stats = absorb(stats, u % 2, s_max, block)
                s_max, block = ahead, j * per
            return s_max, stats, block
        return trip

    def last_step_alone(_, carry):
        s_max, stats, block = carry
        return s_max, absorb(stats, 0, s_max, block), block

    n_steps = n_far + 1
    carry = (tuple(tail_max), tuple(state), tb)
    done = 0
    for count in MOBA_STEPS_PER_TRIP:
        trips = (n_steps - done) // count
        carry = lax.fori_loop(0, trips, steps(count, done), carry)
        done = done + trips * count
    _, stats, _ = lax.fori_loop(0, n_steps - done, last_step_alone, carry)
    row = lax.broadcasted_iota(jnp.int32, (LANES, mb), 0)
    for p in range(MOBA_HEADS_PER_STEP // 2):
        (_, acc_a), (_, acc_b) = stats[2 * p], stats[2 * p + 1]
        o_t = jnp.where(row < hd, acc_a[:LANES] / acc_a[LANES:LANES + 1], acc_b[:LANES] / acc_b[LANES:LANES + 1])
        o_ref[0, :, p * LANES:(p + 1) * LANES] = o_t.T.astype(BF16)


def _moba(qm, km, vt, sel, tab):
    b, s, _ = qm.shape
    nh = MOBA_HEADS_PER_STEP
    width = nh * MOBA_HEAD_DIM
    nblk = s // MOBA_BLOCK
    grid = (b, MOBA_HEADS // nh, nblk)
    return pl.pallas_call(
        _moba_kernel,
        grid=grid,
        in_specs=[
            pl.BlockSpec((1, MOBA_BLOCK, width), lambda bi, hg, c: (bi, c, hg)),
            pl.BlockSpec((1, s, width), lambda bi, hg, c: (bi, 0, hg)),
            pl.BlockSpec((1, nblk, width, MOBA_BLOCK), lambda bi, hg, c: (bi, 0, hg, 0)),
            pl.BlockSpec((1, nh, MAX_BLOCKS, MOBA_BLOCK), lambda bi, hg, c: (bi, hg, 0, c)),
            pl.BlockSpec((nh, 3, MOBA_BLOCK, MOBA_BLOCK), lambda bi, hg, c: (hg, 0, 0, 0),
                         pipeline_mode=pl.Buffered(1 if nh == MOBA_HEADS else 2)),
        ],
        out_specs=pl.BlockSpec((1, MOBA_BLOCK, width), lambda bi, hg, c: (bi, c, hg)),
        out_shape=jax.ShapeDtypeStruct((b, s, MOBA_W), BF16),
        scratch_shapes=[pltpu.VMEM((nh, MAX_BLOCKS, MOBA_BLOCK), F32), pltpu.VMEM((nh, MAX_BLOCKS, MOBA_BLOCK), F32),
                        pltpu.VMEM((2, nh, MOBA_KEYS_PER_STEP, MOBA_BLOCK), F32)],
        compiler_params=_cparams(3),
        name="moba",
    )(qm, km, vt, sel, tab)


def _local_kernel(qs_ref, ks_ref, kh_ref, vst_ref, vsh_ref, tab_ref, sink_ref, qe_ref, kme_ref, vmt_ref,
                  osw_ref, ome_ref):
    i = pl.program_id(1)
    qb = SWA_BLOCK
    pairs_per_kv = SWA_Q_HEADS // SWA_KV_HEADS // 2
    no_prev = jnp.where(i == 0, NEG, 0.0)
    row = lax.broadcasted_iota(jnp.int32, (LANES, qb), 0)
    pending = {}

    def swa_unit(blk, hk, j, par):
        rows = slice(blk * qb, (blk + 1) * qb)
        vrows = slice(hk * LANES, (hk + 1) * LANES)
        kc = slice((2 * hk + par) * LANES, (2 * hk + par + 1) * LANES)
        h = 2 * j + par

        def scores():
            if blk == 0:
                kk = jnp.concatenate([kh_ref[0, :, kc], ks_ref[0, 0:qb, kc]], axis=0)
            else:
                kk = ks_ref[0, (blk - 1) * qb:(blk + 1) * qb, kc]
            s = _nt(kk, qs_ref[0, rows, j * LANES:(j + 1) * LANES]) + tab_ref[h]
            if blk == 0:
                s = jnp.concatenate([s[:qb] + no_prev, s[qb:]], axis=0)
            return s

        def finish(s):
            v_prev = vsh_ref[0, 0, vrows] if blk == 0 else vst_ref[0, blk - 1, vrows]
            vv = jnp.concatenate([v_prev, vst_ref[0, blk, vrows]], axis=1)
            sink = sink_ref[h:h + 1, :]
            m = jnp.maximum(jnp.max(s, axis=0, keepdims=True), sink)
            p = jnp.exp2(s - m)
            den = jnp.sum(p, axis=0, keepdims=True) + jnp.exp2(sink - m)
            out = _mm(vv, p.astype(BF16)) / den
            if par == 0:
                pending[(blk, j)] = out
            else:
                o_t = jnp.where(row < SWA_HEAD_DIM, pending.pop((blk, j)), out)
                osw_ref[0, rows, j * LANES:(j + 1) * LANES] = o_t.T.astype(BF16)

        return scores, finish

    scale = LOG2E / math.sqrt(MEM_HEAD_DIM)
    mq = qb

    def mem_unit(blk, hd):
        rows = slice(blk * mq, (blk + 1) * mq)
        cols = slice(hd * MEM_HEAD_DIM, (hd + 1) * MEM_HEAD_DIM)

        def scores():
            return _nt(kme_ref[0, :, cols], qe_ref[0, rows, cols]) * scale

        def finish(s):
            m = jnp.max(s, axis=0, keepdims=True)
            p = jnp.exp2(s - m)
            den = jnp.sum(p, axis=0, keepdims=True)
            ome_ref[0, rows, cols] = (_mm(vmt_ref[0, cols, :], p.astype(BF16)) / den).T.astype(BF16)

        return scores, finish

    units = [swa_unit(blk, hk, hk * pairs_per_kv + jj, par) for blk in range(ROW_TILE // qb)
             for hk in range(SWA_KV_HEADS) for jj in range(pairs_per_kv) for par in range(2)]
    units += [mem_unit(blk, hd) for hd in range(MEM_HEADS) for blk in range(ROW_TILE // mq)]
    s_next = units[0][0]()
    for u, (_, finish) in enumerate(units):
        s = s_next
        if u + 1 < len(units):
            s_next = units[u + 1][0]()
        finish(s)


def _local(qs, ks, vst, tab, sinks, qe, kme, vmt):
    b, s, _ = qs.shape
    per_tile = ROW_TILE // SWA_BLOCK
    m_len = kme.shape[1]
    tile = lambda n: pl.BlockSpec((1, ROW_TILE, n), lambda bi, i: (bi, i, 0))
    prev_block = lambda bi, i: (bi, jnp.maximum(i * per_tile - 1, 0), 0)
    return pl.pallas_call(
        _local_kernel,
        grid=(b, s // ROW_TILE),
        in_specs=[
            tile(SWA_QW),
            tile(SWA_K_COLS), pl.BlockSpec((1, SWA_BLOCK, SWA_K_COLS), prev_block),
            pl.BlockSpec((1, per_tile, SWA_VT_ROWS, SWA_BLOCK), lambda bi, i: (bi, i, 0, 0)),
            pl.BlockSpec((1, 1, SWA_VT_ROWS, SWA_BLOCK), lambda bi, i: prev_block(bi, i) + (0,)),
            _resident((SWA_Q_HEADS, 2 * SWA_BLOCK, SWA_BLOCK)),
            _resident((SWA_Q_HEADS, LANES)),
            tile(MEM_W),
            pl.BlockSpec((1, m_len, MEM_W), lambda bi, i: (bi, 0, 0)),
            pl.BlockSpec((1, MEM_W, m_len), lambda bi, i: (bi, 0, 0)),
        ],
        out_specs=(tile(SWA_QW), tile(MEM_W)),
        out_shape=(jax.ShapeDtypeStruct((b, s, SWA_QW), BF16), jax.ShapeDtypeStruct((b, s, MEM_W), BF16)),
        compiler_params=_cparams(2),
        name="local",
    )(qs, ks, ks, vst, vst, tab, sinks, qe, kme, vmt)


def _merge_kernel(x_ref, omb_ref, osw_ref, ome_ref, gpre_ref, gpost_ref, wg_ref, wb_ref, wo_ref, out_ref):
    x = x_ref[...]
    h = _rms(x, gpre_ref[...]).astype(BF16)
    branch = (omb_ref[...], osw_ref[...], ome_ref[...])
    half = D_MODEL // 2
    parts = []
    for nc in range(2):
        acc = None
        for br in range(3):
            c0 = br * D_MODEL + nc * half
            gate = jax.nn.sigmoid(_mm(h, wg_ref[:, c0:c0 + half]))
            term = gate * _mm(branch[br], wb_ref[br, :, nc * half:(nc + 1) * half])
            acc = term if acc is None else acc + term
        parts.append(acc.astype(BF16))
    y = _mm(jnp.concatenate(parts, axis=1), wo_ref[...])
    out_ref[...] = x + _rms(y, gpost_ref[...])


def _merge(x2, omb, osw, ome, gpre, gpost, wg, wb, wo):
    t = x2.shape[0]
    row = lambda n: pl.BlockSpec((MERGE_TILE, n), lambda i: (i, 0))
    return pl.pallas_call(
        _merge_kernel,
        grid=(t // MERGE_TILE,),
        in_specs=[row(D_MODEL), row(MOBA_W), row(SWA_QW), row(MEM_W),
                  _resident((1, D_MODEL)), _resident((1, D_MODEL)),
                  _resident((D_MODEL, 3 * D_MODEL)), _resident((3, MOBA_W, D_MODEL)),
                  _resident((D_MODEL, D_MODEL))],
        out_specs=row(D_MODEL),
        out_shape=jax.ShapeDtypeStruct((t, D_MODEL), F32),
        compiler_params=_cparams(1),
        name="merge",
    )(x2, omb, osw, ome, gpre, gpost, wg, wb, wo)


def _ffn_kernel(x_ref, xh_ref, gpre_ref, gpost_ref, wup_ref, cw_ref, cb_ref, wd_ref, out_ref, acc_ref, act_ref,
                *, tiles_per_seq):
    i = pl.program_id(0)
    x = x_ref[...]
    xe = jnp.concatenate([xh_ref[...], x], axis=0)
    he = _rms(xe, gpre_ref[...])
    row = lax.broadcasted_iota(jnp.int32, (ROW_TILE + HALO, 1), 0)
    seq_start = (i % tiles_per_seq) == 0
    he = jnp.where((row < HALO) & seq_start, 0.0, he).astype(BF16)

    def up(fc):
        c0 = fc * FFN_CHUNK
        return (_mm(he, wup_ref[:, c0:c0 + FFN_CHUNK]),
                _mm(he, wup_ref[:, FFN_HIDDEN + c0:FFN_HIDDEN + c0 + FFN_CHUNK]))

    def conv(u, c0):
        w = cw_ref[:, c0:c0 + FFN_CHUNK]
        out = cb_ref[:, c0:c0 + FFN_CHUNK] + w[0:1] * pltpu.roll(u, 2, 0)
        out = out + w[1:2] * pltpu.roll(u, 1, 0)
        out = out + w[2:3] * u
        return out[HALO:]

    n_chunks = FFN_HIDDEN // FFN_CHUNK
    u_gate, u_val = up(0)
    for fc in range(n_chunks):
        c0 = fc * FFN_CHUNK
        ahead = up(fc + 1) if fc + 1 < n_chunks else None
        group, k = divmod(fc, FFN_GROUP)
        act = jax.nn.gelu(conv(u_gate, c0), approximate=True) * conv(u_val, FFN_HIDDEN + c0)
        act_ref[group % 2, :, k * FFN_CHUNK:(k + 1) * FFN_CHUNK] = act.astype(BF16)
        if k == FFN_GROUP - 1 or ahead is None:
            g0 = group * FFN_GROUP * FFN_CHUNK
            width = (k + 1) * FFN_CHUNK
            part = _mm(act_ref[group % 2, :, :width], wd_ref[g0:g0 + width, :])
            if group == 0:
                acc_ref[...] = part
            else:
                acc_ref[...] += part
        if ahead is not None:
            u_gate, u_val = ahead
    out_ref[...] = x + _rms(acc_ref[...], gpost_ref[...])


def _ffn(x2, gpre, gpost, wup, cw, cb, wd, seq_len):
    t = x2.shape[0]
    row = pl.BlockSpec((ROW_TILE, D_MODEL), lambda i: (i, 0))
    halo = pl.BlockSpec((HALO, D_MODEL), lambda i: (jnp.maximum(i * (ROW_TILE // HALO) - 1, 0), 0))
    return pl.pallas_call(
        functools.partial(_ffn_kernel, tiles_per_seq=seq_len // ROW_TILE),
        grid=(t // ROW_TILE,),
        in_specs=[row, halo, _resident((1, D_MODEL)), _resident((1, D_MODEL)),
                  _resident((D_MODEL, 2 * FFN_HIDDEN)), _resident((CONV_WIDTH, 2 * FFN_HIDDEN)),
                  _resident((1, 2 * FFN_HIDDEN)), _resident((FFN_HIDDEN, D_MODEL))],
        out_specs=row,
        out_shape=jax.ShapeDtypeStruct((t, D_MODEL), F32),
        scratch_shapes=[pltpu.VMEM((ROW_TILE, D_MODEL), F32),
                        pltpu.VMEM((2, ROW_TILE, FFN_GROUP * FFN_CHUNK), BF16)],
        compiler_params=_cparams(1),
        name="ffn",
    )(x2, x2, gpre, gpost, wup, cw, cb, wd)


def _t5_bucket(dist):
    n = jnp.maximum(dist, 0)
    nf = jnp.maximum(n, 1).astype(F32)
    large = MAX_EXACT + (jnp.log(nf / MAX_EXACT) / math.log(REL_MAX_DISTANCE / MAX_EXACT)
                         * (NUM_BUCKETS - MAX_EXACT)).astype(jnp.int32)
    large = jnp.minimum(large, NUM_BUCKETS - 1)
    return jnp.where(n < MAX_EXACT, n, large)


def _bias_tables(rel_bias):
    bm = rel_bias[:, :MOBA_HEADS].T.astype(F32)
    bs = rel_bias[:, MOBA_HEADS:].T.astype(F32)

    def lookup(table, dist):
        bucket = _t5_bucket(dist)[None]
        out = jnp.zeros((table.shape[0],) + dist.shape, F32)
        for bkt in range(NUM_BUCKETS):
            out = jnp.where(bucket == bkt, table[:, bkt][:, None, None], out)
        return out

    kj = jnp.arange(MOBA_BLOCK)[:, None]
    qi = jnp.arange(MOBA_BLOCK)[None, :]
    far = bm[:, NUM_BUCKETS - 1][:, None, None]
    own = jnp.where((qi - kj) >= 0, (lookup(bm, qi - kj) - far) * LOG2E, NEG)
    prev = (lookup(bm, MOBA_BLOCK + qi - kj) - far) * LOG2E
    tab_moba = jnp.stack([prev, own, jnp.full_like(own, NEG)], axis=1)
    kj = jnp.arange(2 * SWA_BLOCK)[:, None]
    qi = jnp.arange(SWA_BLOCK)[None, :]
    dist = SWA_BLOCK + qi - kj
    tab_swa = jnp.where((dist >= 0) & (dist < SWA_WINDOW), lookup(bs, dist) * LOG2E, NEG)
    return tab_moba, tab_swa


def _head_variants(w, heads):
    w3 = w.reshape(w.shape[0], heads, 1, -1)
    z = jnp.zeros_like(w3)
    lo = jnp.concatenate([w3, z], axis=3)
    hi = jnp.concatenate([z, w3], axis=3)
    return jnp.concatenate([lo, hi], axis=2).reshape(w.shape[0], heads * 2 * LANES)


def _proj_weight(w_in):
    o = 0
    parts = {}
    for name, width in (("qm", MOBA_W), ("km", MOBA_W), ("vm", MOBA_W), ("qs", SWA_QW), ("ks", SWA_KVW),
                        ("vs", SWA_KVW), ("qe", MEM_W)):
        parts[name] = w_in[:, o:o + width]
        o += width
    q_scale = LOG2E / math.sqrt(MOBA_HEAD_DIM)
    w = jnp.concatenate([parts["qm"] * q_scale, parts["km"], parts["qs"] * q_scale,
                         _head_variants(parts["ks"], SWA_KV_HEADS), parts["qe"]], axis=1)
    vs3 = parts["vs"].reshape(-1, SWA_KV_HEADS, SWA_HEAD_DIM)
    vs_dup = jnp.concatenate([vs3, vs3], axis=2).reshape(-1, SWA_VT_ROWS)
    w_t = jnp.concatenate([parts["vm"], vs_dup], axis=1).T
    return w.astype(BF16), w_t.astype(BF16), w_in[:, o:].astype(BF16)


def kernel(x, mem, norm_mix_pre, norm_mix_post, norm_ffn_pre, norm_ffn_post, norm_mem, w_in, rel_bias, swa_sinks,
           w_mem_kv, w_branch_moba, w_branch_swa, w_branch_mem, w_out, w_ffn_up, ffn_conv_w, ffn_conv_b,
           w_ffn_down):
    b, s, d = x.shape
    m_len = mem.shape[1]
    assert d == D_MODEL and s % ROW_TILE == 0 and s // MOBA_BLOCK <= 32 and m_len % LANES == 0
    depth = w_in.shape[0]
    tab_moba, tab_swa = _bias_tables(rel_bias)
    x2 = x.reshape(b * s, d).astype(F32)
    vec = lambda v: v.reshape(1, -1).astype(F32)
    nblk = s // MOBA_BLOCK
    r3 = lambda a: a.reshape(b, s, a.shape[-1])
    for l in range(depth):
        w_qkv, w_t, w_gate = _proj_weight(w_in[l])
        qm, km, vt, qs, ks, vst, qe, sel = _proj(x2, vec(norm_mix_pre[l]), w_qkv, w_t, s)
        kme, vmt = _memkv(mem.astype(F32), vec(norm_mem[l]), w_mem_kv[l][:, :MEM_W].astype(BF16),
                          w_mem_kv[l][:, MEM_W:].T.astype(BF16))
        o_mb = _moba(r3(qm), r3(km), vt.reshape(b, nblk, MOBA_W, MOBA_BLOCK), sel, tab_moba)
        sinks = jnp.broadcast_to(swa_sinks[l].astype(F32)[:, None] * LOG2E, (SWA_Q_HEADS, LANES))
        o_sw, o_me = _local(r3(qs), r3(ks), vst.reshape(b, s // SWA_BLOCK, SWA_VT_ROWS, SWA_BLOCK), tab_swa, sinks,
                            r3(qe), kme, vmt)
        w_branch = jnp.stack([w_branch_moba[l], w_branch_swa[l], w_branch_mem[l]]).astype(BF16)
        x2 = _merge(x2, o_mb.reshape(b * s, -1), o_sw.reshape(b * s, -1), o_me.reshape(b * s, -1),
                    vec(norm_mix_pre[l]), vec(norm_mix_post[l]), w_gate, w_branch, w_out[l].astype(BF16))
        x2 = _ffn(x2, vec(norm_ffn_pre[l]), vec(norm_ffn_post[l]), w_ffn_up[l].astype(BF16),
                  ffn_conv_w[l].astype(F32), vec(ffn_conv_b[l]), w_ffn_down[l].astype(BF16), s)
    return x2.reshape(b, s, d).astype(x.dtype)
```

```python
import functools
import math

import jax
import jax.numpy as jnp
from jax import lax
from jax.experimental import pallas as pl
from jax.experimental.pallas import tpu as pltpu

D_MODEL = 1024
MOBA_HEADS = 8
MOBA_HEAD_DIM = 64
MOBA_BLOCK = 256
MOBA_TOPK = 3
SWA_Q_HEADS = 8
SWA_KV_HEADS = 2
SWA_HEAD_DIM = 64
SWA_WINDOW = 128
SWA_BLOCK = 128
MEM_HEADS = 4
MEM_HEAD_DIM = 128
NUM_BUCKETS = 32
MAX_EXACT = NUM_BUCKETS // 2
REL_MAX_DISTANCE = 128
FFN_HIDDEN = 2816
CONV_WIDTH = 3
RMS_EPS = 1e-6

MOBA_W = MOBA_HEADS * MOBA_HEAD_DIM
SWA_QW = SWA_Q_HEADS * SWA_HEAD_DIM
SWA_KVW = SWA_KV_HEADS * SWA_HEAD_DIM
MEM_W = MEM_HEADS * MEM_HEAD_DIM

LANES = 128
NEG = -1e30
VMEM_LIMIT = 56 * 1024 * 1024
ROW_TILE = 1024
MERGE_TILE = 1024
MOBA_KEYS_PER_STEP = 512
MOBA_STEPS_PER_TRIP = (4, 2)
MOBA_HEADS_PER_STEP = 8
FFN_CHUNK = 256
FFN_GROUP = 4
HALO = 8
SUM_ROWS = 16
MAX_BLOCKS = 32
ZERO_ROWS = 8

BF16 = jnp.bfloat16
F32 = jnp.float32

SWA_K_COLS = 2 * SWA_KV_HEADS * LANES
SWA_VT_ROWS = SWA_KV_HEADS * LANES
OFF_QM = 0
OFF_KM = OFF_QM + MOBA_W
OFF_QS = OFF_KM + MOBA_W
OFF_KS = OFF_QS + SWA_QW
OFF_QE = OFF_KS + SWA_K_COLS
PROJ_W = OFF_QE + MEM_W
LOG2E = math.log2(math.e)


def _mm(a, b):
    return jnp.dot(a, b, preferred_element_type=F32)


def _nt(a, b):
    return lax.dot_general(a, b, (((1,), (1,)), ((), ())), preferred_element_type=F32)


def _rms(xf, g):
    r = lax.rsqrt(jnp.mean(xf * xf, axis=-1, keepdims=True) + RMS_EPS)
    return xf * r * g


def _cparams(n_axes):
    return pltpu.CompilerParams(dimension_semantics=("arbitrary",) * n_axes, vmem_limit_bytes=VMEM_LIMIT)


def _resident(shape):
    nd = len(shape)
    return pl.BlockSpec(shape, lambda *_: (0,) * nd, pipeline_mode=pl.Buffered(1))


def _proj_kernel(x_ref, g_ref, w_ref, wt_ref, qm_ref, km_ref, vt_ref, qs_ref, ks_ref, vst_ref, qe_ref, sel_ref,
                 kmean_ref, *, tiles_per_seq):
    i = pl.program_id(0)
    h = _rms(x_ref[...], g_ref[...]).astype(BF16)

    def seg(c0, n):
        return _mm(h, w_ref[:, c0:c0 + n])

    q = seg(OFF_QM, MOBA_W).astype(BF16)
    qm_ref[...] = q
    k = seg(OFF_KM, MOBA_W)
    km_ref[...] = k.astype(BF16)
    qs_ref[...] = seg(OFF_QS, SWA_QW).astype(BF16)
    ks_ref[...] = seg(OFF_KS, SWA_K_COLS).astype(BF16)
    qe_ref[...] = seg(OFF_QE, MEM_W).astype(BF16)
    vt = _nt(wt_ref[...], h).astype(BF16)
    for r in range(ROW_TILE // SWA_BLOCK):
        vst_ref[r] = vt[MOBA_W:, r * SWA_BLOCK:(r + 1) * SWA_BLOCK]

    @pl.when(i % tiles_per_seq == 0)
    def _():
        kmean_ref[...] = jnp.zeros_like(kmean_ref)

    blocks_per_tile = ROW_TILE // MOBA_BLOCK
    c0 = (i % tiles_per_seq) * blocks_per_tile
    lane = lax.broadcasted_iota(jnp.int32, (1, MOBA_W), 1)
    for r in range(blocks_per_tile):
        rows = slice(r * MOBA_BLOCK, (r + 1) * MOBA_BLOCK)
        vt_ref[r] = vt[:MOBA_W, rows]
        mean = jnp.sum(k[rows], axis=0, keepdims=True) * (1.0 / MOBA_BLOCK)
        for hd in range(MOBA_HEADS):
            in_head = (lane >= hd * MOBA_HEAD_DIM) & (lane < (hd + 1) * MOBA_HEAD_DIM)
            kmean_ref[pl.ds(hd * MAX_BLOCKS + c0 + r, 1), :] = jnp.where(in_head, mean, 0.0)

    km = kmean_ref[...]
    km_hi = km.astype(BF16)
    km_lo = (km - km_hi.astype(F32)).astype(BF16)
    gates = _nt(km_hi, q) + _nt(km_lo, q)
    blk = lax.broadcasted_iota(jnp.int32, (MAX_BLOCKS, ROW_TILE), 0)
    blk_f = blk.astype(F32)
    col = lax.broadcasted_iota(jnp.int32, (MAX_BLOCKS, ROW_TILE), 1)
    q_blk = c0 + sum((col >= r * MOBA_BLOCK).astype(jnp.int32) for r in range(1, blocks_per_tile))
    past = blk < q_blk
    for hd in range(MOBA_HEADS):
        g = jnp.where(past, gates[hd * MAX_BLOCKS:(hd + 1) * MAX_BLOCKS], -jnp.inf)
        sel = blk < 0
        for _ in range(MOBA_TOPK):
            top = jnp.max(g, axis=0, keepdims=True)
            first = jnp.min(jnp.where(g == top, blk_f, 1e9), axis=0, keepdims=True)
            hit = blk_f == first
            sel = sel | hit
            g = jnp.where(hit, -jnp.inf, g)
        sel_ref[0, hd] = jnp.where(sel & past, 1.0, 0.0)


def _proj(x2, g, w, wt, seq_len):
    t = x2.shape[0]
    n_tiles = t // ROW_TILE
    tiles_per_seq = seq_len // ROW_TILE
    bpt = ROW_TILE // MOBA_BLOCK
    spt = ROW_TILE // SWA_BLOCK
    row = lambda n: pl.BlockSpec((ROW_TILE, n), lambda i: (i, 0))
    out_shapes = (
        jax.ShapeDtypeStruct((t, MOBA_W), BF16),
        jax.ShapeDtypeStruct((t, MOBA_W), BF16),
        jax.ShapeDtypeStruct((t // MOBA_BLOCK, MOBA_W, MOBA_BLOCK), BF16),
        jax.ShapeDtypeStruct((t, SWA_QW), BF16),
        jax.ShapeDtypeStruct((t, SWA_K_COLS), BF16),
        jax.ShapeDtypeStruct((t // SWA_BLOCK, SWA_VT_ROWS, SWA_BLOCK), BF16),
        jax.ShapeDtypeStruct((t, MEM_W), BF16),
        jax.ShapeDtypeStruct((t // seq_len, MOBA_HEADS, MAX_BLOCKS, seq_len), F32),
    )
    out_specs = (row(MOBA_W), row(MOBA_W), pl.BlockSpec((bpt, MOBA_W, MOBA_BLOCK), lambda i: (i, 0, 0)),
                 row(SWA_QW), row(SWA_K_COLS), pl.BlockSpec((spt, SWA_VT_ROWS, SWA_BLOCK), lambda i: (i, 0, 0)),
                 row(MEM_W),
                 pl.BlockSpec((1, MOBA_HEADS, MAX_BLOCKS, ROW_TILE),
                              lambda i: (i // tiles_per_seq, 0, 0, i % tiles_per_seq)))
    return pl.pallas_call(
        functools.partial(_proj_kernel, tiles_per_seq=tiles_per_seq),
        grid=(n_tiles,),
        in_specs=[row(D_MODEL), _resident((1, D_MODEL)), _resident((D_MODEL, PROJ_W)),
                  _resident((MOBA_W + SWA_VT_ROWS, D_MODEL))],
        out_specs=out_specs,
        out_shape=out_shapes,
        scratch_shapes=[pltpu.VMEM((MOBA_HEADS * MAX_BLOCKS, MOBA_W), F32)],
        compiler_params=_cparams(1),
        name="proj",
    )(x2, g, w, wt)


def _memkv_kernel(m_ref, g_ref, wk_ref, wvt_ref, k_ref, vt_ref):
    h = _rms(m_ref[0], g_ref[...]).astype(BF16)
    k_ref[0] = _mm(h, wk_ref[...]).astype(BF16)
    vt_ref[0] = _nt(wvt_ref[...], h).astype(BF16)


def _memkv(mem, g, wk, wvt):
    b, m_len, _ = mem.shape
    return pl.pallas_call(
        _memkv_kernel,
        grid=(b,),
        in_specs=[pl.BlockSpec((1, m_len, D_MODEL), lambda i: (i, 0, 0)), _resident((1, D_MODEL)),
                  _resident((D_MODEL, MEM_W)), _resident((MEM_W, D_MODEL))],
        out_specs=(pl.BlockSpec((1, m_len, MEM_W), lambda i: (i, 0, 0)),
                   pl.BlockSpec((1, MEM_W, m_len), lambda i: (i, 0, 0))),
        out_shape=(jax.ShapeDtypeStruct((b, m_len, MEM_W), BF16), jax.ShapeDtypeStruct((b, MEM_W, m_len), BF16)),
        compiler_params=_cparams(1),
        name="memkv",
    )(mem, g, wk, wvt)


def _moba_kernel(q_ref, k_ref, vt_ref, sel_ref, tab_ref, o_ref, mfar_ref, mtail_ref, s_ref):
    c = pl.program_id(2)
    mb = MOBA_BLOCK
    kb = MOBA_KEYS_PER_STEP
    per = kb // mb
    hd = MOBA_HEAD_DIM
    heads = range(MOBA_HEADS_PER_STEP)
    pair_cols = lambda a: slice((a // 2) * LANES, (a // 2 + 1) * LANES)
    lane_q = lax.broadcasted_iota(jnp.int32, (mb, LANES), 1)
    blk = lax.broadcasted_iota(jnp.int32, (MAX_BLOCKS, mb), 0)
    tb = jnp.maximum(c - 1, 0)
    kt = pl.multiple_of(tb * mb, mb)
    left = jnp.where(c == 0, 1, 0)
    right = jnp.where(c == 0, 2, 1)

    ones = jnp.ones((SUM_ROWS, mb), BF16)

    def values(block, a):
        return jnp.concatenate([vt_ref[0, block, pair_cols(a)], ones], axis=0)

    q_pad = []
    state = []
    tail_max = []
    for a in heads:
        half = a % 2
        qp = q_ref[0, :, pair_cols(a)]
        qa = jnp.where((lane_q >= half * hd) & (lane_q < (half + 1) * hd), qp, jnp.zeros_like(qp))
        q_pad.append(qa)
        sel = sel_ref[0, a] > 0.5
        mfar_ref[a, :MAX_BLOCKS] = jnp.where(sel & (blk <= c - 2), 0.0, NEG)
        mfar_ref[a, MAX_BLOCKS:] = jnp.zeros((ZERO_ROWS, mb), F32)
        mtail_ref[a] = jnp.where((blk == c) | (sel & (blk == c - 1)), 0.0, NEG)

        s = _nt(k_ref[0, pl.ds(kt, 2 * mb), pair_cols(a)], qa)
        s = jnp.concatenate([s[:mb] + tab_ref[a, left] + mtail_ref[a, pl.ds(tb, 1), :],
                             s[mb:] + tab_ref[a, right]], axis=0)
        s_ref[0, a] = s
        tail_max.append(jnp.max(s, axis=0, keepdims=True))
        state.append((jnp.full((1, mb), -jnp.inf, F32), jnp.zeros((LANES + SUM_ROWS, mb), F32)))

    n_far = jnp.where(c >= 2, ((c - 1) * mb + kb - 1) // kb, 0)
    last_step = k_ref.shape[1] // kb - 1

    def far_scores(j, slot):
        rows = pl.ds(pl.multiple_of(j * kb, kb), kb)
        out = []
        for a in heads:
            s = _nt(k_ref[0, rows, pair_cols(a)], q_pad[a])
            s_ref[slot, a] = s
            block_max = [jnp.max(s[t * mb:(t + 1) * mb], axis=0, keepdims=True)
                         + mfar_ref[a, pl.ds(j * per + t, 1), :] for t in range(per)]
            out.append(functools.reduce(jnp.maximum, block_max))
        return tuple(out)

    def absorb(stats, slot, s_max, block, mask_row):
        out = []
        for a in heads:
            m, acc = stats[a]
            m_new = jnp.maximum(m, s_max[a])
            acc = jnp.exp2(m - m_new) * acc
            for t in range(per):
                shift = mfar_ref[a, pl.ds(mask_row + t, 1), :] - m_new
                pb = jnp.exp2(s_ref[slot, a, t * mb:(t + 1) * mb] + shift).astype(BF16)
                acc = acc + _mm(values(block + t, a), pb)
            out.append((m_new, acc))
        return tuple(out)

    def steps(count, first):
        def trip(i, carry):
            s_max, stats, block, mask_row = carry
            for u in range(count):
                j = first + count * i + u
                ahead = far_scores(jnp.minimum(j, last_step), (u + 1) % 2)
                stats = absorb(stats, u % 2, s_max, block, mask_row)
                s_max, block, mask_row = ahead, j * per, j * per
            return s_max, stats, block, mask_row
        return trip

    def last_step_alone(_, carry):
        s_max, stats, block, mask_row = carry
        return s_max, absorb(stats, 0, s_max, block, mask_row), block, mask_row

    n_steps = n_far + 1
    carry = (tuple(tail_max), tuple(state), tb, jnp.int32(MAX_BLOCKS))
    done = 0
    for count in MOBA_STEPS_PER_TRIP:
        trips = (n_steps - done) // count
        carry = lax.fori_loop(0, trips, steps(count, done), carry)
        done = done + trips * count
    _, stats, _, _ = lax.fori_loop(0, n_steps - done, last_step_alone, carry)
    row = lax.broadcasted_iota(jnp.int32, (LANES, mb), 0)
    for p in range(MOBA_HEADS_PER_STEP // 2):
        (_, acc_a), (_, acc_b) = stats[2 * p], stats[2 * p + 1]
        o_t = jnp.where(row < hd, acc_a[:LANES] / acc_a[LANES:LANES + 1], acc_b[:LANES] / acc_b[LANES:LANES + 1])
        o_ref[0, :, p * LANES:(p + 1) * LANES] = o_t.T.astype(BF16)


def _moba(qm, km, vt, sel, tab):
    b, s, _ = qm.shape
    nh = MOBA_HEADS_PER_STEP
    width = nh * MOBA_HEAD_DIM
    nblk = s // MOBA_BLOCK
    grid = (b, MOBA_HEADS // nh, nblk)
    return pl.pallas_call(
        _moba_kernel,
        grid=grid,
        in_specs=[
            pl.BlockSpec((1, MOBA_BLOCK, width), lambda bi, hg, c: (bi, c, hg)),
            pl.BlockSpec((1, s, width), lambda bi, hg, c: (bi, 0, hg)),
            pl.BlockSpec((1, nblk, width, MOBA_BLOCK), lambda bi, hg, c: (bi, 0, hg, 0)),
            pl.BlockSpec((1, nh, MAX_BLOCKS, MOBA_BLOCK), lambda bi, hg, c: (bi, hg, 0, c)),
            pl.BlockSpec((nh, 3, MOBA_BLOCK, MOBA_BLOCK), lambda bi, hg, c: (hg, 0, 0, 0),
                         pipeline_mode=pl.Buffered(1 if nh == MOBA_HEADS else 2)),
        ],
        out_specs=pl.BlockSpec((1, MOBA_BLOCK, width), lambda bi, hg, c: (bi, c, hg)),
        out_shape=jax.ShapeDtypeStruct((b, s, MOBA_W), BF16),
        scratch_shapes=[pltpu.VMEM((nh, MAX_BLOCKS + ZERO_ROWS, MOBA_BLOCK), F32),
                        pltpu.VMEM((nh, MAX_BLOCKS, MOBA_BLOCK), F32),
                        pltpu.VMEM((2, nh, MOBA_KEYS_PER_STEP, MOBA_BLOCK), F32)],
        compiler_params=_cparams(3),
        name="moba",
    )(qm, km, vt, sel, tab)


def _local_kernel(qs_ref, ks_ref, kh_ref, vst_ref, vsh_ref, tab_ref, sink_ref, qe_ref, kme_ref, vmt_ref,
                  osw_ref, ome_ref):
    i = pl.program_id(1)
    qb = SWA_BLOCK
    pairs_per_kv = SWA_Q_HEADS // SWA_KV_HEADS // 2
    no_prev = jnp.where(i == 0, NEG, 0.0)
    row = lax.broadcasted_iota(jnp.int32, (LANES, qb), 0)
    pending = {}

    def swa_unit(blk, hk, j, par):
        rows = slice(blk * qb, (blk + 1) * qb)
        vrows = slice(hk * LANES, (hk + 1) * LANES)
        kc = slice((2 * hk + par) * LANES, (2 * hk + par + 1) * LANES)
        h = 2 * j + par

        def scores():
            if blk == 0:
                kk = jnp.concatenate([kh_ref[0, :, kc], ks_ref[0, 0:qb, kc]], axis=0)
            else:
                kk = ks_ref[0, (blk - 1) * qb:(blk + 1) * qb, kc]
            s = _nt(kk, qs_ref[0, rows, j * LANES:(j + 1) * LANES]) + tab_ref[h]
            if blk == 0:
                s = jnp.concatenate([s[:qb] + no_prev, s[qb:]], axis=0)
            return s

        def finish(s):
            v_prev = vsh_ref[0, 0, vrows] if blk == 0 else vst_ref[0, blk - 1, vrows]
            vv = jnp.concatenate([v_prev, vst_ref[0, blk, vrows]], axis=1)
            sink = sink_ref[h:h + 1, :]
            m = jnp.maximum(jnp.max(s, axis=0, keepdims=True), sink)
            p = jnp.exp2(s - m)
            den = jnp.sum(p, axis=0, keepdims=True) + jnp.exp2(sink - m)
            out = _mm(vv, p.astype(BF16)) / den
            if par == 0:
                pending[(blk, j)] = out
            else:
                o_t = jnp.where(row < SWA_HEAD_DIM, pending.pop((blk, j)), out)
                osw_ref[0, rows, j * LANES:(j + 1) * LANES] = o_t.T.astype(BF16)

        return scores, finish

    scale = LOG2E / math.sqrt(MEM_HEAD_DIM)
    mq = qb

    def mem_unit(blk, hd):
        rows = slice(blk * mq, (blk + 1) * mq)
        cols = slice(hd * MEM_HEAD_DIM, (hd + 1) * MEM_HEAD_DIM)

        def scores():
            return _nt(kme_ref[0, :, cols], qe_ref[0, rows, cols]) * scale

        def finish(s):
            m = jnp.max(s, axis=0, keepdims=True)
            p = jnp.exp2(s - m)
            den = jnp.sum(p, axis=0, keepdims=True)
            ome_ref[0, rows, cols] = (_mm(vmt_ref[0, cols, :], p.astype(BF16)) / den).T.astype(BF16)

        return scores, finish

    units = [swa_unit(blk, hk, hk * pairs_per_kv + jj, par) for blk in range(ROW_TILE // qb)
             for hk in range(SWA_KV_HEADS) for jj in range(pairs_per_kv) for par in range(2)]
    units += [mem_unit(blk, hd) for hd in range(MEM_HEADS) for blk in range(ROW_TILE // mq)]
    s_next = units[0][0]()
    for u, (_, finish) in enumerate(units):
        s = s_next
        if u + 1 < len(units):
            s_next = units[u + 1][0]()
        finish(s)


def _local(qs, ks, vst, tab, sinks, qe, kme, vmt):
    b, s, _ = qs.shape
    per_tile = ROW_TILE // SWA_BLOCK
    m_len = kme.shape[1]
    tile = lambda n: pl.BlockSpec((1, ROW_TILE, n), lambda bi, i: (bi, i, 0))
    prev_block = lambda bi, i: (bi, jnp.maximum(i * per_tile - 1, 0), 0)
    return pl.pallas_call(
        _local_kernel,
        grid=(b, s // ROW_TILE),
        in_specs=[
            tile(SWA_QW),
            tile(SWA_K_COLS), pl.BlockSpec((1, SWA_BLOCK, SWA_K_COLS), prev_block),
            pl.BlockSpec((1, per_tile, SWA_VT_ROWS, SWA_BLOCK), lambda bi, i: (bi, i, 0, 0)),
            pl.BlockSpec((1, 1, SWA_VT_ROWS, SWA_BLOCK), lambda bi, i: prev_block(bi, i) + (0,)),
            _resident((SWA_Q_HEADS, 2 * SWA_BLOCK, SWA_BLOCK)),
            _resident((SWA_Q_HEADS, LANES)),
            tile(MEM_W),
            pl.BlockSpec((1, m_len, MEM_W), lambda bi, i: (bi, 0, 0)),
            pl.BlockSpec((1, MEM_W, m_len), lambda bi, i: (bi, 0, 0)),
        ],
        out_specs=(tile(SWA_QW), tile(MEM_W)),
        out_shape=(jax.ShapeDtypeStruct((b, s, SWA_QW), BF16), jax.ShapeDtypeStruct((b, s, MEM_W), BF16)),
        compiler_params=_cparams(2),
        name="local",
    )(qs, ks, ks, vst, vst, tab, sinks, qe, kme, vmt)


def _merge_kernel(x_ref, omb_ref, osw_ref, ome_ref, gpre_ref, gpost_ref, wg_ref, wb_ref, wo_ref, out_ref):
    x = x_ref[...]
    h = _rms(x, gpre_ref[...]).astype(BF16)
    branch = (omb_ref[...], osw_ref[...], ome_ref[...])
    half = D_MODEL // 2
    parts = []
    for nc in range(2):
        acc = None
        for br in range(3):
            c0 = br * D_MODEL + nc * half
            gate = jax.nn.sigmoid(_mm(h, wg_ref[:, c0:c0 + half]))
            term = gate * _mm(branch[br], wb_ref[br, :, nc * half:(nc + 1) * half])
            acc = term if acc is None else acc + term
        parts.append(acc.astype(BF16))
    y = _mm(jnp.concatenate(parts, axis=1), wo_ref[...])
    out_ref[...] = x + _rms(y, gpost_ref[...])


def _merge(x2, omb, osw, ome, gpre, gpost, wg, wb, wo):
    t = x2.shape[0]
    row = lambda n: pl.BlockSpec((MERGE_TILE, n), lambda i: (i, 0))
    return pl.pallas_call(
        _merge_kernel,
        grid=(t // MERGE_TILE,),
        in_specs=[row(D_MODEL), row(MOBA_W), row(SWA_QW), row(MEM_W),
                  _resident((1, D_MODEL)), _resident((1, D_MODEL)),
                  _resident((D_MODEL, 3 * D_MODEL)), _resident((3, MOBA_W, D_MODEL)),
                  _resident((D_MODEL, D_MODEL))],
        out_specs=row(D_MODEL),
        out_shape=jax.ShapeDtypeStruct((t, D_MODEL), F32),
        compiler_params=_cparams(1),
        name="merge",
    )(x2, omb, osw, ome, gpre, gpost, wg, wb, wo)


def _ffn_kernel(x_ref, xh_ref, gpre_ref, gpost_ref, wup_ref, cw_ref, cb_ref, wd_ref, out_ref, acc_ref, act_ref,
                *, tiles_per_seq):
    i = pl.program_id(0)
    x = x_ref[...]
    xe = jnp.concatenate([xh_ref[...], x], axis=0)
    he = _rms(xe, gpre_ref[...])
    row = lax.broadcasted_iota(jnp.int32, (ROW_TILE + HALO, 1), 0)
    seq_start = (i % tiles_per_seq) == 0
    he = jnp.where((row < HALO) & seq_start, 0.0, he).astype(BF16)

    def up(fc):
        c0 = fc * FFN_CHUNK
        return (_mm(he, wup_ref[:, c0:c0 + FFN_CHUNK]),
                _mm(he, wup_ref[:, FFN_HIDDEN + c0:FFN_HIDDEN + c0 + FFN_CHUNK]))

    def conv(u, c0):
        w = cw_ref[:, c0:c0 + FFN_CHUNK]
        out = cb_ref[:, c0:c0 + FFN_CHUNK] + w[0:1] * pltpu.roll(u, 2, 0)
        out = out + w[1:2] * pltpu.roll(u, 1, 0)
        out = out + w[2:3] * u
        return out[HALO:]

    n_chunks = FFN_HIDDEN // FFN_CHUNK
    u_gate, u_val = up(0)
    for fc in range(n_chunks):
        c0 = fc * FFN_CHUNK
        ahead = up(fc + 1) if fc + 1 < n_chunks else None
        group, k = divmod(fc, FFN_GROUP)
        act = jax.nn.gelu(conv(u_gate, c0), approximate=True) * conv(u_val, FFN_HIDDEN + c0)
        act_ref[group % 2, :, k * FFN_CHUNK:(k + 1) * FFN_CHUNK] = act.astype(BF16)
        if k == FFN_GROUP - 1 or ahead is None:
            g0 = group * FFN_GROUP * FFN_CHUNK
            width = (k + 1) * FFN_CHUNK
            part = _mm(act_ref[group % 2, :, :width], wd_ref[g0:g0 + width, :])
            if group == 0:
                acc_ref[...] = part
            else:
                acc_ref[...] += part
        if ahead is not None:
            u_gate, u_val = ahead
    out_ref[...] = x + _rms(acc_ref[...], gpost_ref[...])


def _ffn(x2, gpre, gpost, wup, cw, cb, wd, seq_len):
    t = x2.shape[0]
    row = pl.BlockSpec((ROW_TILE, D_MODEL), lambda i: (i, 0))
    halo = pl.BlockSpec((HALO, D_MODEL), lambda i: (jnp.maximum(i * (ROW_TILE // HALO) - 1, 0), 0))
    return pl.pallas_call(
        functools.partial(_ffn_kernel, tiles_per_seq=seq_len // ROW_TILE),
        grid=(t // ROW_TILE,),
        in_specs=[row, halo, _resident((1, D_MODEL)), _resident((1, D_MODEL)),
                  _resident((D_MODEL, 2 * FFN_HIDDEN)), _resident((CONV_WIDTH, 2 * FFN_HIDDEN)),
                  _resident((1, 2 * FFN_HIDDEN)), _resident((FFN_HIDDEN, D_MODEL))],
        out_specs=row,
        out_shape=jax.ShapeDtypeStruct((t, D_MODEL), F32),
        scratch_shapes=[pltpu.VMEM((ROW_TILE, D_MODEL), F32),
                        pltpu.VMEM((2, ROW_TILE, FFN_GROUP * FFN_CHUNK), BF16)],
        compiler_params=_cparams(1),
        name="ffn",
    )(x2, x2, gpre, gpost, wup, cw, cb, wd)


def _t5_bucket(dist):
    n = jnp.maximum(dist, 0)
    nf = jnp.maximum(n, 1).astype(F32)
    large = MAX_EXACT + (jnp.log(nf / MAX_EXACT) / math.log(REL_MAX_DISTANCE / MAX_EXACT)
                         * (NUM_BUCKETS - MAX_EXACT)).astype(jnp.int32)
    large = jnp.minimum(large, NUM_BUCKETS - 1)
    return jnp.where(n < MAX_EXACT, n, large)


def _bias_tables(rel_bias):
    bm = rel_bias[:, :MOBA_HEADS].T.astype(F32)
    bs = rel_bias[:, MOBA_HEADS:].T.astype(F32)

    def lookup(table, dist):
        bucket = _t5_bucket(dist)[None]
        out = jnp.zeros((table.shape[0],) + dist.shape, F32)
        for bkt in range(NUM_BUCKETS):
            out = jnp.where(bucket == bkt, table[:, bkt][:, None, None], out)
        return out

    kj = jnp.arange(MOBA_BLOCK)[:, None]
    qi = jnp.arange(MOBA_BLOCK)[None, :]
    far = bm[:, NUM_BUCKETS - 1][:, None, None]
    own = jnp.where((qi - kj) >= 0, (lookup(bm, qi - kj) - far) * LOG2E, NEG)
    prev = (lookup(bm, MOBA_BLOCK + qi - kj) - far) * LOG2E
    tab_moba = jnp.stack([prev, own, jnp.full_like(own, NEG)], axis=1)
    kj = jnp.arange(2 * SWA_BLOCK)[:, None]
    qi = jnp.arange(SWA_BLOCK)[None, :]
    dist = SWA_BLOCK + qi - kj
    tab_swa = jnp.where((dist >= 0) & (dist < SWA_WINDOW), lookup(bs, dist) * LOG2E, NEG)
    return tab_moba, tab_swa


def _head_variants(w, heads):
    w3 = w.reshape(w.shape[0], heads, 1, -1)
    z = jnp.zeros_like(w3)
    lo = jnp.concatenate([w3, z], axis=3)
    hi = jnp.concatenate([z, w3], axis=3)
    return jnp.concatenate([lo, hi], axis=2).reshape(w.shape[0], heads * 2 * LANES)


def _proj_weight(w_in):
    o = 0
    parts = {}
    for name, width in (("qm", MOBA_W), ("km", MOBA_W), ("vm", MOBA_W), ("qs", SWA_QW), ("ks", SWA_KVW),
                        ("vs", SWA_KVW), ("qe", MEM_W)):
        parts[name] = w_in[:, o:o + width]
        o += width
    q_scale = LOG2E / math.sqrt(MOBA_HEAD_DIM)
    w = jnp.concatenate([parts["qm"] * q_scale, parts["km"], parts["qs"] * q_scale,
                         _head_variants(parts["ks"], SWA_KV_HEADS), parts["qe"]], axis=1)
    vs3 = parts["vs"].reshape(-1, SWA_KV_HEADS, SWA_HEAD_DIM)
    vs_dup = jnp.concatenate([vs3, vs3], axis=2).reshape(-1, SWA_VT_ROWS)
    w_t = jnp.concatenate([parts["vm"], vs_dup], axis=1).T
    return w.astype(BF16), w_t.astype(BF16), w_in[:, o:].astype(BF16)


def kernel(x, mem, norm_mix_pre, norm_mix_post, norm_ffn_pre, norm_ffn_post, norm_mem, w_in, rel_bias, swa_sinks,
           w_mem_kv, w_branch_moba, w_branch_swa, w_branch_mem, w_out, w_ffn_up, ffn_conv_w, ffn_conv_b,
           w_ffn_down):
    b, s, d = x.shape
    m_len = mem.shape[1]
    assert d == D_MODEL and s % ROW_TILE == 0 and s // MOBA_BLOCK <= 32 and m_len % LANES == 0
    depth = w_in.shape[0]
    tab_moba, tab_swa = _bias_tables(rel_bias)
    x2 = x.reshape(b * s, d).astype(F32)
    vec = lambda v: v.reshape(1, -1).astype(F32)
    nblk = s // MOBA_BLOCK
    r3 = lambda a: a.reshape(b, s, a.shape[-1])
    for l in range(depth):
        w_qkv, w_t, w_gate = _proj_weight(w_in[l])
        qm, km, vt, qs, ks, vst, qe, sel = _proj(x2, vec(norm_mix_pre[l]), w_qkv, w_t, s)
        kme, vmt = _memkv(mem.astype(F32), vec(norm_mem[l]), w_mem_kv[l][:, :MEM_W].astype(BF16),
                          w_mem_kv[l][:, MEM_W:].T.astype(BF16))
        o_mb = _moba(r3(qm), r3(km), vt.reshape(b, nblk, MOBA_W, MOBA_BLOCK), sel, tab_moba)
        sinks = jnp.broadcast_to(swa_sinks[l].astype(F32)[:, None] * LOG2E, (SWA_Q_HEADS, LANES))
        o_sw, o_me = _local(r3(qs), r3(ks), vst.reshape(b, s // SWA_BLOCK, SWA_VT_ROWS, SWA_BLOCK), tab_swa, sinks,
                            r3(qe), kme, vmt)
        w_branch = jnp.stack([w_branch_moba[l], w_branch_swa[l], w_branch_mem[l]]).astype(BF16)
        x2 = _merge(x2, o_mb.reshape(b * s, -1), o_sw.reshape(b * s, -1), o_me.reshape(b * s, -1),
                    vec(norm_mix_pre[l]), vec(norm_mix_post[l]), w_gate, w_branch, w_out[l].astype(BF16))
        x2 = _ffn(x2, vec(norm_ffn_pre[l]), vec(norm_ffn_post[l]), w_ffn_up[l].astype(BF16),
                  ffn_conv_w[l].astype(F32), vec(ffn_conv_b[l]), w_ffn_down[l].astype(BF16), s)
    return x2.reshape(b, s, d).astype(x.dtype)
```

```python
import functools
import math

import jax
import jax.numpy as jnp
from jax import lax
from jax.experimental import pallas as pl
from jax.experimental.pallas import tpu as pltpu

D_MODEL = 1024
MOBA_HEADS = 8
MOBA_HEAD_DIM = 64
MOBA_BLOCK = 256
MOBA_TOPK = 3
SWA_Q_HEADS = 8
SWA_KV_HEADS = 2
SWA_HEAD_DIM = 64
SWA_WINDOW = 128
SWA_BLOCK = 128
MEM_HEADS = 4
MEM_HEAD_DIM = 128
NUM_BUCKETS = 32
MAX_EXACT = NUM_BUCKETS // 2
REL_MAX_DISTANCE = 128
FFN_HIDDEN = 2816
CONV_WIDTH = 3
RMS_EPS = 1e-6

MOBA_W = MOBA_HEADS * MOBA_HEAD_DIM
SWA_QW = SWA_Q_HEADS * SWA_HEAD_DIM
SWA_KVW = SWA_KV_HEADS * SWA_HEAD_DIM
MEM_W = MEM_HEADS * MEM_HEAD_DIM

LANES = 128
NEG = -1e30
VMEM_LIMIT = 56 * 1024 * 1024
ROW_TILE = 1024
MERGE_TILE = 1024
MOBA_KEYS_PER_STEP = 512
MOBA_STEPS_PER_TRIP = (4, 2)
MOBA_HEADS_PER_STEP = 8
FFN_CHUNK = 256
FFN_GROUP = 4
HALO = 8
SUM_ROWS = 16
MAX_BLOCKS = 32
ZERO_ROWS = 8

BF16 = jnp.bfloat16
F32 = jnp.float32

SWA_K_COLS = 2 * SWA_KV_HEADS * LANES
SWA_VT_ROWS = SWA_KV_HEADS * LANES
OFF_QM = 0
OFF_KM = OFF_QM + MOBA_W
OFF_QS = OFF_KM + MOBA_W
OFF_KS = OFF_QS + SWA_QW
OFF_QE = OFF_KS + SWA_K_COLS
PROJ_W = OFF_QE + MEM_W
LOG2E = math.log2(math.e)


def _mm(a, b):
    return jnp.dot(a, b, preferred_element_type=F32)


def _nt(a, b):
    return lax.dot_general(a, b, (((1,), (1,)), ((), ())), preferred_element_type=F32)


def _rms(xf, g):
    r = lax.rsqrt(jnp.mean(xf * xf, axis=-1, keepdims=True) + RMS_EPS)
    return xf * r * g


def _gelu_times(g, v):
    a = math.sqrt(2.0 / math.pi)
    minus_2z_log2 = g * ((-2.0 * LOG2E * a) + (-2.0 * LOG2E * a * 0.044715) * (g * g))
    return (g * v) / (1.0 + jnp.exp2(minus_2z_log2))


def _cparams(n_axes):
    return pltpu.CompilerParams(dimension_semantics=("arbitrary",) * n_axes, vmem_limit_bytes=VMEM_LIMIT)


def _resident(shape):
    nd = len(shape)
    return pl.BlockSpec(shape, lambda *_: (0,) * nd, pipeline_mode=pl.Buffered(1))


def _proj_kernel(x_ref, g_ref, w_ref, wt_ref, qm_ref, km_ref, vt_ref, qs_ref, ks_ref, vst_ref, qe_ref, sel_ref,
                 kmean_ref, *, tiles_per_seq):
    i = pl.program_id(0)
    h = _rms(x_ref[...], g_ref[...]).astype(BF16)

    def seg(c0, n):
        return _mm(h, w_ref[:, c0:c0 + n])

    q = seg(OFF_QM, MOBA_W).astype(BF16)
    qm_ref[...] = q
    k = seg(OFF_KM, MOBA_W)
    km_ref[...] = k.astype(BF16)
    qs_ref[...] = seg(OFF_QS, SWA_QW).astype(BF16)
    ks_ref[...] = seg(OFF_KS, SWA_K_COLS).astype(BF16)
    qe_ref[...] = seg(OFF_QE, MEM_W).astype(BF16)
    vt = _nt(wt_ref[...], h).astype(BF16)
    for r in range(ROW_TILE // SWA_BLOCK):
        vst_ref[r] = vt[MOBA_W:, r * SWA_BLOCK:(r + 1) * SWA_BLOCK]

    @pl.when(i % tiles_per_seq == 0)
    def _():
        kmean_ref[...] = jnp.zeros_like(kmean_ref)

    blocks_per_tile = ROW_TILE // MOBA_BLOCK
    c0 = (i % tiles_per_seq) * blocks_per_tile
    lane = lax.broadcasted_iota(jnp.int32, (1, MOBA_W), 1)
    for r in range(blocks_per_tile):
        rows = slice(r * MOBA_BLOCK, (r + 1) * MOBA_BLOCK)
        vt_ref[r] = vt[:MOBA_W, rows]
        mean = jnp.sum(k[rows], axis=0, keepdims=True) * (1.0 / MOBA_BLOCK)
        for hd in range(MOBA_HEADS):
            in_head = (lane >= hd * MOBA_HEAD_DIM) & (lane < (hd + 1) * MOBA_HEAD_DIM)
            kmean_ref[pl.ds(hd * MAX_BLOCKS + c0 + r, 1), :] = jnp.where(in_head, mean, 0.0)

    km = kmean_ref[...]
    km_hi = km.astype(BF16)
    km_lo = (km - km_hi.astype(F32)).astype(BF16)
    gates = _nt(km_hi, q) + _nt(km_lo, q)
    blk = lax.broadcasted_iota(jnp.int32, (MAX_BLOCKS, ROW_TILE), 0)
    blk_f = blk.astype(F32)
    col = lax.broadcasted_iota(jnp.int32, (MAX_BLOCKS, ROW_TILE), 1)
    q_blk = c0 + sum((col >= r * MOBA_BLOCK).astype(jnp.int32) for r in range(1, blocks_per_tile))
    past = blk < q_blk
    for hd in range(MOBA_HEADS):
        g = jnp.where(past, gates[hd * MAX_BLOCKS:(hd + 1) * MAX_BLOCKS], -jnp.inf)
        sel = blk < 0
        for _ in range(MOBA_TOPK):
            top = jnp.max(g, axis=0, keepdims=True)
            first = jnp.min(jnp.where(g == top, blk_f, 1e9), axis=0, keepdims=True)
            hit = blk_f == first
            sel = sel | hit
            g = jnp.where(hit, -jnp.inf, g)
        sel_ref[0, hd] = jnp.where(sel & past, 1.0, 0.0)


def _proj(x2, g, w, wt, seq_len):
    t = x2.shape[0]
    n_tiles = t // ROW_TILE
    tiles_per_seq = seq_len // ROW_TILE
    bpt = ROW_TILE // MOBA_BLOCK
    spt = ROW_TILE // SWA_BLOCK
    row = lambda n: pl.BlockSpec((ROW_TILE, n), lambda i: (i, 0))
    out_shapes = (
        jax.ShapeDtypeStruct((t, MOBA_W), BF16),
        jax.ShapeDtypeStruct((t, MOBA_W), BF16),
        jax.ShapeDtypeStruct((t // MOBA_BLOCK, MOBA_W, MOBA_BLOCK), BF16),
        jax.ShapeDtypeStruct((t, SWA_QW), BF16),
        jax.ShapeDtypeStruct((t, SWA_K_COLS), BF16),
        jax.ShapeDtypeStruct((t // SWA_BLOCK, SWA_VT_ROWS, SWA_BLOCK), BF16),
        jax.ShapeDtypeStruct((t, MEM_W), BF16),
        jax.ShapeDtypeStruct((t // seq_len, MOBA_HEADS, MAX_BLOCKS, seq_len), F32),
    )
    out_specs = (row(MOBA_W), row(MOBA_W), pl.BlockSpec((bpt, MOBA_W, MOBA_BLOCK), lambda i: (i, 0, 0)),
                 row(SWA_QW), row(SWA_K_COLS), pl.BlockSpec((spt, SWA_VT_ROWS, SWA_BLOCK), lambda i: (i, 0, 0)),
                 row(MEM_W),
                 pl.BlockSpec((1, MOBA_HEADS, MAX_BLOCKS, ROW_TILE),
                              lambda i: (i // tiles_per_seq, 0, 0, i % tiles_per_seq)))
    return pl.pallas_call(
        functools.partial(_proj_kernel, tiles_per_seq=tiles_per_seq),
        grid=(n_tiles,),
        in_specs=[row(D_MODEL), _resident((1, D_MODEL)), _resident((D_MODEL, PROJ_W)),
                  _resident((MOBA_W + SWA_VT_ROWS, D_MODEL))],
        out_specs=out_specs,
        out_shape=out_shapes,
        scratch_shapes=[pltpu.VMEM((MOBA_HEADS * MAX_BLOCKS, MOBA_W), F32)],
        compiler_params=_cparams(1),
        name="proj",
    )(x2, g, w, wt)


def _memkv_kernel(m_ref, g_ref, wk_ref, wvt_ref, k_ref, vt_ref):
    h = _rms(m_ref[0], g_ref[...]).astype(BF16)
    k_ref[0] = _mm(h, wk_ref[...]).astype(BF16)
    vt_ref[0] = _nt(wvt_ref[...], h).astype(BF16)


def _memkv(mem, g, wk, wvt):
    b, m_len, _ = mem.shape
    return pl.pallas_call(
        _memkv_kernel,
        grid=(b,),
        in_specs=[pl.BlockSpec((1, m_len, D_MODEL), lambda i: (i, 0, 0)), _resident((1, D_MODEL)),
                  _resident((D_MODEL, MEM_W)), _resident((MEM_W, D_MODEL))],
        out_specs=(pl.BlockSpec((1, m_len, MEM_W), lambda i: (i, 0, 0)),
                   pl.BlockSpec((1, MEM_W, m_len), lambda i: (i, 0, 0))),
        out_shape=(jax.ShapeDtypeStruct((b, m_len, MEM_W), BF16), jax.ShapeDtypeStruct((b, MEM_W, m_len), BF16)),
        compiler_params=_cparams(1),
        name="memkv",
    )(mem, g, wk, wvt)


def _moba_kernel(q_ref, k_ref, vt_ref, sel_ref, tab_ref, o_ref, mfar_ref, mtail_ref, s_ref):
    c = pl.program_id(2)
    mb = MOBA_BLOCK
    kb = MOBA_KEYS_PER_STEP
    per = kb // mb
    hd = MOBA_HEAD_DIM
    heads = range(MOBA_HEADS_PER_STEP)
    pair_cols = lambda a: slice((a // 2) * LANES, (a // 2 + 1) * LANES)
    lane_q = lax.broadcasted_iota(jnp.int32, (mb, LANES), 1)
    blk = lax.broadcasted_iota(jnp.int32, (MAX_BLOCKS, mb), 0)
    tb = jnp.maximum(c - 1, 0)
    kt = pl.multiple_of(tb * mb, mb)
    left = jnp.where(c == 0, 1, 0)
    right = jnp.where(c == 0, 2, 1)

    ones = jnp.ones((SUM_ROWS, mb), BF16)

    def values(block, a):
        return jnp.concatenate([vt_ref[0, block, pair_cols(a)], ones], axis=0)

    q_pad = []
    state = []
    tail_max = []
    for a in heads:
        half = a % 2
        qp = q_ref[0, :, pair_cols(a)]
        qa = jnp.where((lane_q >= half * hd) & (lane_q < (half + 1) * hd), qp, jnp.zeros_like(qp))
        q_pad.append(qa)
        sel = sel_ref[0, a] > 0.5
        mfar_ref[a, :MAX_BLOCKS] = jnp.where(sel & (blk <= c - 2), 0.0, NEG)
        mfar_ref[a, MAX_BLOCKS:] = jnp.zeros((ZERO_ROWS, mb), F32)
        mtail_ref[a] = jnp.where((blk == c) | (sel & (blk == c - 1)), 0.0, NEG)

        s = _nt(k_ref[0, pl.ds(kt, 2 * mb), pair_cols(a)], qa)
        s = jnp.concatenate([s[:mb] + tab_ref[a, left] + mtail_ref[a, pl.ds(tb, 1), :],
                             s[mb:] + tab_ref[a, right]], axis=0)
        s_ref[0, a] = s
        tail_max.append(jnp.max(s, axis=0, keepdims=True))
        state.append((jnp.full((1, mb), -jnp.inf, F32), jnp.zeros((LANES + SUM_ROWS, mb), F32)))

    n_far = jnp.where(c >= 2, ((c - 1) * mb + kb - 1) // kb, 0)
    last_step = k_ref.shape[1] // kb - 1

    def far_scores(j, slot):
        rows = pl.ds(pl.multiple_of(j * kb, kb), kb)
        out = []
        for a in heads:
            s = _nt(k_ref[0, rows, pair_cols(a)], q_pad[a])
            s_ref[slot, a] = s
            block_max = [jnp.max(s[t * mb:(t + 1) * mb], axis=0, keepdims=True)
                         + mfar_ref[a, pl.ds(j * per + t, 1), :] for t in range(per)]
            out.append(functools.reduce(jnp.maximum, block_max))
        return tuple(out)

    def absorb(stats, slot, s_max, block, mask_row):
        out = []
        for a in heads:
            m, acc = stats[a]
            m_new = jnp.maximum(m, s_max[a])
            acc = jnp.exp2(m - m_new) * acc
            for t in range(per):
                shift = mfar_ref[a, pl.ds(mask_row + t, 1), :] - m_new
                pb = jnp.exp2(s_ref[slot, a, t * mb:(t + 1) * mb] + shift).astype(BF16)
                acc = acc + _mm(values(block + t, a), pb)
            out.append((m_new, acc))
        return tuple(out)

    def steps(count, first):
        def trip(i, carry):
            s_max, stats, block, mask_row = carry
            for u in range(count):
                j = first + count * i + u
                ahead = far_scores(jnp.minimum(j, last_step), (u + 1) % 2)
                stats = absorb(stats, u % 2, s_max, block, mask_row)
                s_max, block, mask_row = ahead, j * per, j * per
            return s_max, stats, block, mask_row
        return trip

    def last_step_alone(_, carry):
        s_max, stats, block, mask_row = carry
        return s_max, absorb(stats, 0, s_max, block, mask_row), block, mask_row

    n_steps = n_far + 1
    carry = (tuple(tail_max), tuple(state), tb, jnp.int32(MAX_BLOCKS))
    done = 0
    for count in MOBA_STEPS_PER_TRIP:
        trips = (n_steps - done) // count
        carry = lax.fori_loop(0, trips, steps(count, done), carry)
        done = done + trips * count
    _, stats, _, _ = lax.fori_loop(0, n_steps - done, last_step_alone, carry)
    row = lax.broadcasted_iota(jnp.int32, (LANES, mb), 0)
    for p in range(MOBA_HEADS_PER_STEP // 2):
        (_, acc_a), (_, acc_b) = stats[2 * p], stats[2 * p + 1]
        o_t = jnp.where(row < hd, acc_a[:LANES] / acc_a[LANES:LANES + 1], acc_b[:LANES] / acc_b[LANES:LANES + 1])
        o_ref[0, :, p * LANES:(p + 1) * LANES] = o_t.T.astype(BF16)


def _moba(qm, km, vt, sel, tab):
    b, s, _ = qm.shape
    nh = MOBA_HEADS_PER_STEP
    width = nh * MOBA_HEAD_DIM
    nblk = s // MOBA_BLOCK
    grid = (b, MOBA_HEADS // nh, nblk)
    return pl.pallas_call(
        _moba_kernel,
        grid=grid,
        in_specs=[
            pl.BlockSpec((1, MOBA_BLOCK, width), lambda bi, hg, c: (bi, c, hg)),
            pl.BlockSpec((1, s, width), lambda bi, hg, c: (bi, 0, hg)),
            pl.BlockSpec((1, nblk, width, MOBA_BLOCK), lambda bi, hg, c: (bi, 0, hg, 0)),
            pl.BlockSpec((1, nh, MAX_BLOCKS, MOBA_BLOCK), lambda bi, hg, c: (bi, hg, 0, c)),
            pl.BlockSpec((nh, 3, MOBA_BLOCK, MOBA_BLOCK), lambda bi, hg, c: (hg, 0, 0, 0),
                         pipeline_mode=pl.Buffered(1 if nh == MOBA_HEADS else 2)),
        ],
        out_specs=pl.BlockSpec((1, MOBA_BLOCK, width), lambda bi, hg, c: (bi, c, hg)),
        out_shape=jax.ShapeDtypeStruct((b, s, MOBA_W), BF16),
        scratch_shapes=[pltpu.VMEM((nh, MAX_BLOCKS + ZERO_ROWS, MOBA_BLOCK), F32),
                        pltpu.VMEM((nh, MAX_BLOCKS, MOBA_BLOCK), F32),
                        pltpu.VMEM((2, nh, MOBA_KEYS_PER_STEP, MOBA_BLOCK), F32)],
        compiler_params=_cparams(3),
        name="moba",
    )(qm, km, vt, sel, tab)


def _local_kernel(qs_ref, ks_ref, kh_ref, vst_ref, vsh_ref, tab_ref, sink_ref, qe_ref, kme_ref, vmt_ref,
                  osw_ref, ome_ref):
    i = pl.program_id(1)
    qb = SWA_BLOCK
    pairs_per_kv = SWA_Q_HEADS // SWA_KV_HEADS // 2
    no_prev = jnp.where(i == 0, NEG, 0.0)
    row = lax.broadcasted_iota(jnp.int32, (LANES, qb), 0)
    pending = {}

    def swa_unit(blk, hk, j, par):
        rows = slice(blk * qb, (blk + 1) * qb)
        vrows = slice(hk * LANES, (hk + 1) * LANES)
        kc = slice((2 * hk + par) * LANES, (2 * hk + par + 1) * LANES)
        h = 2 * j + par

        def scores():
            if blk == 0:
                kk = jnp.concatenate([kh_ref[0, :, kc], ks_ref[0, 0:qb, kc]], axis=0)
            else:
                kk = ks_ref[0, (blk - 1) * qb:(blk + 1) * qb, kc]
            s = _nt(kk, qs_ref[0, rows, j * LANES:(j + 1) * LANES]) + tab_ref[h]
            if blk == 0:
                s = jnp.concatenate([s[:qb] + no_prev, s[qb:]], axis=0)
            return s

        def finish(s):
            v_prev = vsh_ref[0, 0, vrows] if blk == 0 else vst_ref[0, blk - 1, vrows]
            vv = jnp.concatenate([v_prev, vst_ref[0, blk, vrows]], axis=1)
            sink = sink_ref[h:h + 1, :]
            m = jnp.maximum(jnp.max(s, axis=0, keepdims=True), sink)
            p = jnp.exp2(s - m)
            den = jnp.sum(p, axis=0, keepdims=True) + jnp.exp2(sink - m)
            out = _mm(vv, p.astype(BF16)) / den
            if par == 0:
                pending[(blk, j)] = out
            else:
                o_t = jnp.where(row < SWA_HEAD_DIM, pending.pop((blk, j)), out)
                osw_ref[0, rows, j * LANES:(j + 1) * LANES] = o_t.T.astype(BF16)

        return scores, finish

    scale = LOG2E / math.sqrt(MEM_HEAD_DIM)
    mq = qb

    def mem_unit(blk, hd):
        rows = slice(blk * mq, (blk + 1) * mq)
        cols = slice(hd * MEM_HEAD_DIM, (hd + 1) * MEM_HEAD_DIM)

        def scores():
            return _nt(kme_ref[0, :, cols], qe_ref[0, rows, cols]) * scale

        def finish(s):
            m = jnp.max(s, axis=0, keepdims=True)
            p = jnp.exp2(s - m)
            den = jnp.sum(p, axis=0, keepdims=True)
            ome_ref[0, rows, cols] = (_mm(vmt_ref[0, cols, :], p.astype(BF16)) / den).T.astype(BF16)

        return scores, finish

    units = [swa_unit(blk, hk, hk * pairs_per_kv + jj, par) for blk in range(ROW_TILE // qb)
             for hk in range(SWA_KV_HEADS) for jj in range(pairs_per_kv) for par in range(2)]
    units += [mem_unit(blk, hd) for hd in range(MEM_HEADS) for blk in range(ROW_TILE // mq)]
    s_next = units[0][0]()
    for u, (_, finish) in enumerate(units):
        s = s_next
        if u + 1 < len(units):
            s_next = units[u + 1][0]()
        finish(s)


def _local(qs, ks, vst, tab, sinks, qe, kme, vmt):
    b, s, _ = qs.shape
    per_tile = ROW_TILE // SWA_BLOCK
    m_len = kme.shape[1]
    tile = lambda n: pl.BlockSpec((1, ROW_TILE, n), lambda bi, i: (bi, i, 0))
    prev_block = lambda bi, i: (bi, jnp.maximum(i * per_tile - 1, 0), 0)
    return pl.pallas_call(
        _local_kernel,
        grid=(b, s // ROW_TILE),
        in_specs=[
            tile(SWA_QW),
            tile(SWA_K_COLS), pl.BlockSpec((1, SWA_BLOCK, SWA_K_COLS), prev_block),
            pl.BlockSpec((1, per_tile, SWA_VT_ROWS, SWA_BLOCK), lambda bi, i: (bi, i, 0, 0)),
            pl.BlockSpec((1, 1, SWA_VT_ROWS, SWA_BLOCK), lambda bi, i: prev_block(bi, i) + (0,)),
            _resident((SWA_Q_HEADS, 2 * SWA_BLOCK, SWA_BLOCK)),
            _resident((SWA_Q_HEADS, LANES)),
            tile(MEM_W),
            pl.BlockSpec((1, m_len, MEM_W), lambda bi, i: (bi, 0, 0)),
            pl.BlockSpec((1, MEM_W, m_len), lambda bi, i: (bi, 0, 0)),
        ],
        out_specs=(tile(SWA_QW), tile(MEM_W)),
        out_shape=(jax.ShapeDtypeStruct((b, s, SWA_QW), BF16), jax.ShapeDtypeStruct((b, s, MEM_W), BF16)),
        compiler_params=_cparams(2),
        name="local",
    )(qs, ks, ks, vst, vst, tab, sinks, qe, kme, vmt)


def _merge_kernel(x_ref, omb_ref, osw_ref, ome_ref, gpre_ref, gpost_ref, wg_ref, wb_ref, wo_ref, out_ref):
    x = x_ref[...]
    h = _rms(x, gpre_ref[...]).astype(BF16)
    branch = (omb_ref[...], osw_ref[...], ome_ref[...])
    half = D_MODEL // 2
    parts = []
    for nc in range(2):
        acc = None
        for br in range(3):
            c0 = br * D_MODEL + nc * half
            gate = jax.nn.sigmoid(_mm(h, wg_ref[:, c0:c0 + half]))
            term = gate * _mm(branch[br], wb_ref[br, :, nc * half:(nc + 1) * half])
            acc = term if acc is None else acc + term
        parts.append(acc.astype(BF16))
    y = _mm(jnp.concatenate(parts, axis=1), wo_ref[...])
    out_ref[...] = x + _rms(y, gpost_ref[...])


def _merge(x2, omb, osw, ome, gpre, gpost, wg, wb, wo):
    t = x2.shape[0]
    row = lambda n: pl.BlockSpec((MERGE_TILE, n), lambda i: (i, 0))
    return pl.pallas_call(
        _merge_kernel,
        grid=(t // MERGE_TILE,),
        in_specs=[row(D_MODEL), row(MOBA_W), row(SWA_QW), row(MEM_W),
                  _resident((1, D_MODEL)), _resident((1, D_MODEL)),
                  _resident((D_MODEL, 3 * D_MODEL)), _resident((3, MOBA_W, D_MODEL)),
                  _resident((D_MODEL, D_MODEL))],
        out_specs=row(D_MODEL),
        out_shape=jax.ShapeDtypeStruct((t, D_MODEL), F32),
        compiler_params=_cparams(1),
        name="merge",
    )(x2, omb, osw, ome, gpre, gpost, wg, wb, wo)


def _ffn_kernel(x_ref, xh_ref, gpre_ref, gpost_ref, wup_ref, cw_ref, cb_ref, wd_ref, out_ref, acc_ref, act_ref,
                *, tiles_per_seq):
    i = pl.program_id(0)
    x = x_ref[...]
    xe = jnp.concatenate([xh_ref[...], x], axis=0)
    he = _rms(xe, gpre_ref[...])
    row = lax.broadcasted_iota(jnp.int32, (ROW_TILE + HALO, 1), 0)
    seq_start = (i % tiles_per_seq) == 0
    he = jnp.where((row < HALO) & seq_start, 0.0, he).astype(BF16)

    def up(fc):
        c0 = fc * FFN_CHUNK
        return (_mm(he, wup_ref[:, c0:c0 + FFN_CHUNK]),
                _mm(he, wup_ref[:, FFN_HIDDEN + c0:FFN_HIDDEN + c0 + FFN_CHUNK]))

    def conv(u, c0):
        w = cw_ref[:, c0:c0 + FFN_CHUNK]
        out = cb_ref[:, c0:c0 + FFN_CHUNK] + w[0:1] * pltpu.roll(u, 2, 0)
        out = out + w[1:2] * pltpu.roll(u, 1, 0)
        out = out + w[2:3] * u
        return out[HALO:]

    n_chunks = FFN_HIDDEN // FFN_CHUNK
    u_gate, u_val = up(0)
    for fc in range(n_chunks):
        c0 = fc * FFN_CHUNK
        ahead = up(fc + 1) if fc + 1 < n_chunks else None
        group, k = divmod(fc, FFN_GROUP)
        act = _gelu_times(conv(u_gate, c0), conv(u_val, FFN_HIDDEN + c0))
        act_ref[group % 2, :, k * FFN_CHUNK:(k + 1) * FFN_CHUNK] = act.astype(BF16)
        if k == FFN_GROUP - 1 or ahead is None:
            g0 = group * FFN_GROUP * FFN_CHUNK
            width = (k + 1) * FFN_CHUNK
            part = _mm(act_ref[group % 2, :, :width], wd_ref[g0:g0 + width, :])
            if group == 0:
                acc_ref[...] = part
            else:
                acc_ref[...] += part
        if ahead is not None:
            u_gate, u_val = ahead
    out_ref[...] = x + _rms(acc_ref[...], gpost_ref[...])


def _ffn(x2, gpre, gpost, wup, cw, cb, wd, seq_len):
    t = x2.shape[0]
    row = pl.BlockSpec((ROW_TILE, D_MODEL), lambda i: (i, 0))
    halo = pl.BlockSpec((HALO, D_MODEL), lambda i: (jnp.maximum(i * (ROW_TILE // HALO) - 1, 0), 0))
    return pl.pallas_call(
        functools.partial(_ffn_kernel, tiles_per_seq=seq_len // ROW_TILE),
        grid=(t // ROW_TILE,),
        in_specs=[row, halo, _resident((1, D_MODEL)), _resident((1, D_MODEL)),
                  _resident((D_MODEL, 2 * FFN_HIDDEN)), _resident((CONV_WIDTH, 2 * FFN_HIDDEN)),
                  _resident((1, 2 * FFN_HIDDEN)), _resident((FFN_HIDDEN, D_MODEL))],
        out_specs=row,
        out_shape=jax.ShapeDtypeStruct((t, D_MODEL), F32),
        scratch_shapes=[pltpu.VMEM((ROW_TILE, D_MODEL), F32),
                        pltpu.VMEM((2, ROW_TILE, FFN_GROUP * FFN_CHUNK), BF16)],
        compiler_params=_cparams(1),
        name="ffn",
    )(x2, x2, gpre, gpost, wup, cw, cb, wd)


def _t5_bucket(dist):
    n = jnp.maximum(dist, 0)
    nf = jnp.maximum(n, 1).astype(F32)
    large = MAX_EXACT + (jnp.log(nf / MAX_EXACT) / math.log(REL_MAX_DISTANCE / MAX_EXACT)
                         * (NUM_BUCKETS - MAX_EXACT)).astype(jnp.int32)
    large = jnp.minimum(large, NUM_BUCKETS - 1)
    return jnp.where(n < MAX_EXACT, n, large)


def _bias_tables(rel_bias):
    bm = rel_bias[:, :MOBA_HEADS].T.astype(F32)
    bs = rel_bias[:, MOBA_HEADS:].T.astype(F32)

    def lookup(table, dist):
        bucket = _t5_bucket(dist)[None]
        out = jnp.zeros((table.shape[0],) + dist.shape, F32)
        for bkt in range(NUM_BUCKETS):
            out = jnp.where(bucket == bkt, table[:, bkt][:, None, None], out)
        return out

    kj = jnp.arange(MOBA_BLOCK)[:, None]
    qi = jnp.arange(MOBA_BLOCK)[None, :]
    far = bm[:, NUM_BUCKETS - 1][:, None, None]
    own = jnp.where((qi - kj) >= 0, (lookup(bm, qi - kj) - far) * LOG2E, NEG)
    prev = (lookup(bm, MOBA_BLOCK + qi - kj) - far) * LOG2E
    tab_moba = jnp.stack([prev, own, jnp.full_like(own, NEG)], axis=1)
    kj = jnp.arange(2 * SWA_BLOCK)[:, None]
    qi = jnp.arange(SWA_BLOCK)[None, :]
    dist = SWA_BLOCK + qi - kj
    tab_swa = jnp.where((dist >= 0) & (dist < SWA_WINDOW), lookup(bs, dist) * LOG2E, NEG)
    return tab_moba, tab_swa


def _head_variants(w, heads):
    w3 = w.reshape(w.shape[0], heads, 1, -1)
    z = jnp.zeros_like(w3)
    lo = jnp.concatenate([w3, z], axis=3)
    hi = jnp.concatenate([z, w3], axis=3)
    return jnp.concatenate([lo, hi], axis=2).reshape(w.shape[0], heads * 2 * LANES)


def _proj_weight(w_in):
    o = 0
    parts = {}
    for name, width in (("qm", MOBA_W), ("km", MOBA_W), ("vm", MOBA_W), ("qs", SWA_QW), ("ks", SWA_KVW),
                        ("vs", SWA_KVW), ("qe", MEM_W)):
        parts[name] = w_in[:, o:o + width]
        o += width
    q_scale = LOG2E / math.sqrt(MOBA_HEAD_DIM)
    w = jnp.concatenate([parts["qm"] * q_scale, parts["km"], parts["qs"] * q_scale,
                         _head_variants(parts["ks"], SWA_KV_HEADS), parts["qe"]], axis=1)
    vs3 = parts["vs"].reshape(-1, SWA_KV_HEADS, SWA_HEAD_DIM)
    vs_dup = jnp.concatenate([vs3, vs3], axis=2).reshape(-1, SWA_VT_ROWS)
    w_t = jnp.concatenate([parts["vm"], vs_dup], axis=1).T
    return w.astype(BF16), w_t.astype(BF16), w_in[:, o:].astype(BF16)


def kernel(x, mem, norm_mix_pre, norm_mix_post, norm_ffn_pre, norm_ffn_post, norm_mem, w_in, rel_bias, swa_sinks,
           w_mem_kv, w_branch_moba, w_branch_swa, w_branch_mem, w_out, w_ffn_up, ffn_conv_w, ffn_conv_b,
           w_ffn_down):
    b, s, d = x.shape
    m_len = mem.shape[1]
    assert d == D_MODEL and s % ROW_TILE == 0 and s // MOBA_BLOCK <= 32 and m_len % LANES == 0
    depth = w_in.shape[0]
    tab_moba, tab_swa = _bias_tables(rel_bias)
    x2 = x.reshape(b * s, d).astype(F32)
    vec = lambda v: v.reshape(1, -1).astype(F32)
    nblk = s // MOBA_BLOCK
    r3 = lambda a: a.reshape(b, s, a.shape[-1])
    for l in range(depth):
        w_qkv, w_t, w_gate = _proj_weight(w_in[l])
        qm, km, vt, qs, ks, vst, qe, sel = _proj(x2, vec(norm_mix_pre[l]), w_qkv, w_t, s)
        kme, vmt = _memkv(mem.astype(F32), vec(norm_mem[l]), w_mem_kv[l][:, :MEM_W].astype(BF16),
                          w_mem_kv[l][:, MEM_W:].T.astype(BF16))
        o_mb = _moba(r3(qm), r3(km), vt.reshape(b, nblk, MOBA_W, MOBA_BLOCK), sel, tab_moba)
        sinks = jnp.broadcast_to(swa_sinks[l].astype(F32)[:, None] * LOG2E, (SWA_Q_HEADS, LANES))
        o_sw, o_me = _local(r3(qs), r3(ks), vst.reshape(b, s // SWA_BLOCK, SWA_VT_ROWS, SWA_BLOCK), tab_swa, sinks,
                            r3(qe), kme, vmt)
        w_branch = jnp.stack([w_branch_moba[l], w_branch_swa[l], w_branch_mem[l]]).astype(BF16)
        x2 = _merge(x2, o_mb.reshape(b * s, -1), o_sw.reshape(b * s, -1), o_me.reshape(b * s, -1),
                    vec(norm_mix_pre[l]), vec(norm_mix_post[l]), w_gate, w_branch, w_out[l].astype(BF16))
        x2 = _ffn(x2, vec(norm_ffn_pre[l]), vec(norm_ffn_post[l]), w_ffn_up[l].astype(BF16),
                  ffn_conv_w[l].astype(F32), vec(ffn_conv_b[l]), w_ffn_down[l].astype(BF16), s)
    return x2.reshape(b, s, d).astype(x.dtype)
```

```python
import functools
import math

import jax
import jax.numpy as jnp
from jax import lax
from jax.experimental import pallas as pl
from jax.experimental.pallas import tpu as pltpu

D_MODEL = 1024
MOBA_HEADS = 8
MOBA_HEAD_DIM = 64
MOBA_BLOCK = 256
MOBA_TOPK = 3
SWA_Q_HEADS = 8
SWA_KV_HEADS = 2
SWA_HEAD_DIM = 64
SWA_WINDOW = 128
SWA_BLOCK = 128
MEM_HEADS = 4
MEM_HEAD_DIM = 128
NUM_BUCKETS = 32
MAX_EXACT = NUM_BUCKETS // 2
REL_MAX_DISTANCE = 128
FFN_HIDDEN = 2816
CONV_WIDTH = 3
RMS_EPS = 1e-6

MOBA_W = MOBA_HEADS * MOBA_HEAD_DIM
SWA_QW = SWA_Q_HEADS * SWA_HEAD_DIM
SWA_KVW = SWA_KV_HEADS * SWA_HEAD_DIM
MEM_W = MEM_HEADS * MEM_HEAD_DIM

LANES = 128
NEG = -1e30
VMEM_LIMIT = 56 * 1024 * 1024
ROW_TILE = 1024
MERGE_TILE = 1024
MOBA_KEYS_PER_STEP = 512
MOBA_STEPS_PER_TRIP = (4, 2)
MOBA_HEADS_PER_STEP = 8
FFN_CHUNK = 256
FFN_GROUP = 4
HALO = 8
SUM_ROWS = 16
MAX_BLOCKS = 32
ZERO_ROWS = 8

BF16 = jnp.bfloat16
F32 = jnp.float32

SWA_K_COLS = 2 * SWA_KV_HEADS * LANES
SWA_VT_ROWS = SWA_KV_HEADS * LANES
OFF_QM = 0
OFF_KM = OFF_QM + MOBA_W
OFF_QS = OFF_KM + MOBA_W
OFF_KS = OFF_QS + SWA_QW
OFF_QE = OFF_KS + SWA_K_COLS
PROJ_W = OFF_QE + MEM_W
LOG2E = math.log2(math.e)


def _mm(a, b):
    return jnp.dot(a, b, preferred_element_type=F32)


def _nt(a, b):
    return lax.dot_general(a, b, (((1,), (1,)), ((), ())), preferred_element_type=F32)


def _rms(xf, g):
    r = lax.rsqrt(jnp.mean(xf * xf, axis=-1, keepdims=True) + RMS_EPS)
    return xf * r * g


def _gelu_times(g, v):
    a = math.sqrt(2.0 / math.pi)
    minus_2z_log2 = g * ((-2.0 * LOG2E * a) + (-2.0 * LOG2E * a * 0.044715) * (g * g))
    return (g * v) / (1.0 + jnp.exp2(minus_2z_log2))


def _cparams(n_axes):
    return pltpu.CompilerParams(dimension_semantics=("arbitrary",) * n_axes, vmem_limit_bytes=VMEM_LIMIT)


def _resident(shape):
    nd = len(shape)
    return pl.BlockSpec(shape, lambda *_: (0,) * nd, pipeline_mode=pl.Buffered(1))


def _proj_kernel(x_ref, g_ref, w_ref, wt_ref, qm_ref, km_ref, vt_ref, qs_ref, ks_ref, vst_ref, qe_ref, sel_ref,
                 kmean_ref, *, tiles_per_seq):
    i = pl.program_id(0)
    h = _rms(x_ref[...], g_ref[...]).astype(BF16)

    def seg(c0, n):
        return _mm(h, w_ref[:, c0:c0 + n])

    q = seg(OFF_QM, MOBA_W).astype(BF16)
    qm_ref[...] = q
    k = seg(OFF_KM, MOBA_W)
    km_ref[...] = k.astype(BF16)
    qs_ref[...] = seg(OFF_QS, SWA_QW).astype(BF16)
    ks_ref[...] = seg(OFF_KS, SWA_K_COLS).astype(BF16)
    qe_ref[...] = seg(OFF_QE, MEM_W).astype(BF16)
    vt = _nt(wt_ref[...], h).astype(BF16)
    for r in range(ROW_TILE // SWA_BLOCK):
        vst_ref[r] = vt[MOBA_W:, r * SWA_BLOCK:(r + 1) * SWA_BLOCK]

    @pl.when(i % tiles_per_seq == 0)
    def _():
        kmean_ref[...] = jnp.zeros_like(kmean_ref)

    blocks_per_tile = ROW_TILE // MOBA_BLOCK
    c0 = (i % tiles_per_seq) * blocks_per_tile
    lane = lax.broadcasted_iota(jnp.int32, (1, MOBA_W), 1)
    for r in range(blocks_per_tile):
        rows = slice(r * MOBA_BLOCK, (r + 1) * MOBA_BLOCK)
        vt_ref[r] = vt[:MOBA_W, rows]
        mean = jnp.sum(k[rows], axis=0, keepdims=True) * (1.0 / MOBA_BLOCK)
        for hd in range(MOBA_HEADS):
            in_head = (lane >= hd * MOBA_HEAD_DIM) & (lane < (hd + 1) * MOBA_HEAD_DIM)
            kmean_ref[pl.ds(hd * MAX_BLOCKS + c0 + r, 1), :] = jnp.where(in_head, mean, 0.0)

    km = kmean_ref[...]
    km_hi = km.astype(BF16)
    km_lo = (km - km_hi.astype(F32)).astype(BF16)
    gates = _nt(km_hi, q) + _nt(km_lo, q)
    blk = lax.broadcasted_iota(jnp.int32, (MAX_BLOCKS, ROW_TILE), 0)
    blk_f = blk.astype(F32)
    col = lax.broadcasted_iota(jnp.int32, (MAX_BLOCKS, ROW_TILE), 1)
    q_blk = c0 + sum((col >= r * MOBA_BLOCK).astype(jnp.int32) for r in range(1, blocks_per_tile))
    past = blk < q_blk
    for hd in range(MOBA_HEADS):
        g = jnp.where(past, gates[hd * MAX_BLOCKS:(hd + 1) * MAX_BLOCKS], -jnp.inf)
        sel = blk < 0
        for _ in range(MOBA_TOPK):
            top = jnp.max(g, axis=0, keepdims=True)
            first = jnp.min(jnp.where(g == top, blk_f, 1e9), axis=0, keepdims=True)
            hit = blk_f == first
            sel = sel | hit
            g = jnp.where(hit, -jnp.inf, g)
        sel_ref[0, hd] = jnp.where(sel & past, 1.0, 0.0)


def _proj(x2, g, w, wt, seq_len):
    t = x2.shape[0]
    n_tiles = t // ROW_TILE
    tiles_per_seq = seq_len // ROW_TILE
    bpt = ROW_TILE // MOBA_BLOCK
    spt = ROW_TILE // SWA_BLOCK
    row = lambda n: pl.BlockSpec((ROW_TILE, n), lambda i: (i, 0))
    out_shapes = (
        jax.ShapeDtypeStruct((t, MOBA_W), BF16),
        jax.ShapeDtypeStruct((t, MOBA_W), BF16),
        jax.ShapeDtypeStruct((t // MOBA_BLOCK, MOBA_W, MOBA_BLOCK), BF16),
        jax.ShapeDtypeStruct((t, SWA_QW), BF16),
        jax.ShapeDtypeStruct((t, SWA_K_COLS), BF16),
        jax.ShapeDtypeStruct((t // SWA_BLOCK, SWA_VT_ROWS, SWA_BLOCK), BF16),
        jax.ShapeDtypeStruct((t, MEM_W), BF16),
        jax.ShapeDtypeStruct((t // seq_len, MOBA_HEADS, MAX_BLOCKS, seq_len), F32),
    )
    out_specs = (row(MOBA_W), row(MOBA_W), pl.BlockSpec((bpt, MOBA_W, MOBA_BLOCK), lambda i: (i, 0, 0)),
                 row(SWA_QW), row(SWA_K_COLS), pl.BlockSpec((spt, SWA_VT_ROWS, SWA_BLOCK), lambda i: (i, 0, 0)),
                 row(MEM_W),
                 pl.BlockSpec((1, MOBA_HEADS, MAX_BLOCKS, ROW_TILE),
                              lambda i: (i // tiles_per_seq, 0, 0, i % tiles_per_seq)))
    return pl.pallas_call(
        functools.partial(_proj_kernel, tiles_per_seq=tiles_per_seq),
        grid=(n_tiles,),
        in_specs=[row(D_MODEL), _resident((1, D_MODEL)), _resident((D_MODEL, PROJ_W)),
                  _resident((MOBA_W + SWA_VT_ROWS, D_MODEL))],
        out_specs=out_specs,
        out_shape=out_shapes,
        scratch_shapes=[pltpu.VMEM((MOBA_HEADS * MAX_BLOCKS, MOBA_W), F32)],
        compiler_params=_cparams(1),
        name="proj",
    )(x2, g, w, wt)


def _memkv_kernel(m_ref, g_ref, wk_ref, wvt_ref, k_ref, vt_ref):
    h = _rms(m_ref[0], g_ref[...]).astype(BF16)
    k_ref[0] = _mm(h, wk_ref[...]).astype(BF16)
    vt_ref[0] = _nt(wvt_ref[...], h).astype(BF16)


def _memkv(mem, g, wk, wvt):
    b, m_len, _ = mem.shape
    return pl.pallas_call(
        _memkv_kernel,
        grid=(b,),
        in_specs=[pl.BlockSpec((1, m_len, D_MODEL), lambda i: (i, 0, 0)), _resident((1, D_MODEL)),
                  _resident((D_MODEL, MEM_W)), _resident((MEM_W, D_MODEL))],
        out_specs=(pl.BlockSpec((1, m_len, MEM_W), lambda i: (i, 0, 0)),
                   pl.BlockSpec((1, MEM_W, m_len), lambda i: (i, 0, 0))),
        out_shape=(jax.ShapeDtypeStruct((b, m_len, MEM_W), BF16), jax.ShapeDtypeStruct((b, MEM_W, m_len), BF16)),
        compiler_params=_cparams(1),
        name="memkv",
    )(mem, g, wk, wvt)


def _moba_kernel(q_ref, k_ref, vt_ref, sel_ref, tab_ref, o_ref, mfar_ref, mtail_ref, s_ref):
    c = pl.program_id(2)
    mb = MOBA_BLOCK
    kb = MOBA_KEYS_PER_STEP
    per = kb // mb
    hd = MOBA_HEAD_DIM
    heads = range(MOBA_HEADS_PER_STEP)
    pair_cols = lambda a: slice((a // 2) * LANES, (a // 2 + 1) * LANES)
    lane_q = lax.broadcasted_iota(jnp.int32, (mb, LANES), 1)
    blk = lax.broadcasted_iota(jnp.int32, (MAX_BLOCKS, mb), 0)
    tb = jnp.maximum(c - 1, 0)
    kt = pl.multiple_of(tb * mb, mb)
    left = jnp.where(c == 0, 1, 0)
    right = jnp.where(c == 0, 2, 1)

    ones = jnp.ones((SUM_ROWS, mb), BF16)

    def values(block, a):
        return jnp.concatenate([vt_ref[0, block, pair_cols(a)], ones], axis=0)

    q_pad = []
    state = []
    tail_max = []
    for a in heads:
        half = a % 2
        qp = q_ref[0, :, pair_cols(a)]
        qa = jnp.where((lane_q >= half * hd) & (lane_q < (half + 1) * hd), qp, jnp.zeros_like(qp))
        q_pad.append(qa)
        sel = sel_ref[0, a] > 0.5
        mfar_ref[a, :MAX_BLOCKS] = jnp.where(sel & (blk <= c - 2), 0.0, NEG)
        mfar_ref[a, MAX_BLOCKS:] = jnp.zeros((ZERO_ROWS, mb), F32)
        mtail_ref[a] = jnp.where((blk == c) | (sel & (blk == c - 1)), 0.0, NEG)

        s = _nt(k_ref[0, pl.ds(kt, 2 * mb), pair_cols(a)], qa)
        s = jnp.concatenate([s[:mb] + tab_ref[a, left] + mtail_ref[a, pl.ds(tb, 1), :],
                             s[mb:] + tab_ref[a, right]], axis=0)
        s_ref[0, a] = s
        tail_max.append(jnp.max(s, axis=0, keepdims=True))
        state.append((jnp.full((1, mb), -jnp.inf, F32), jnp.zeros((LANES + SUM_ROWS, mb), F32)))

    n_far = jnp.where(c >= 2, ((c - 1) * mb + kb - 1) // kb, 0)
    last_step = k_ref.shape[1] // kb - 1

    def far_scores(j, slot):
        rows = pl.ds(pl.multiple_of(j * kb, kb), kb)
        out = []
        for a in heads:
            s = _nt(k_ref[0, rows, pair_cols(a)], q_pad[a])
            s_ref[slot, a] = s
            block_max = [jnp.max(s[t * mb:(t + 1) * mb], axis=0, keepdims=True)
                         + mfar_ref[a, pl.ds(j * per + t, 1), :] for t in range(per)]
            out.append(functools.reduce(jnp.maximum, block_max))
        return tuple(out)

    def absorb(stats, slot, s_max, block, mask_row):
        out = []
        for a in heads:
            m, acc = stats[a]
            m_new = jnp.maximum(m, s_max[a])
            acc = jnp.exp2(m - m_new) * acc
            for t in range(per):
                shift = mfar_ref[a, pl.ds(mask_row + t, 1), :] - m_new
                pb = jnp.exp2(s_ref[slot, a, t * mb:(t + 1) * mb] + shift).astype(BF16)
                acc = acc + _mm(values(block + t, a), pb)
            out.append((m_new, acc))
        return tuple(out)

    def steps(count, first):
        def trip(i, carry):
            s_max, stats, block, mask_row = carry
            for u in range(count):
                j = first + count * i + u
                ahead = far_scores(jnp.minimum(j, last_step), (u + 1) % 2)
                stats = absorb(stats, u % 2, s_max, block, mask_row)
                s_max, block, mask_row = ahead, j * per, j * per
            return s_max, stats, block, mask_row
        return trip

    def last_step_alone(_, carry):
        s_max, stats, block, mask_row = carry
        return s_max, absorb(stats, 0, s_max, block, mask_row), block, mask_row

    n_steps = n_far + 1
    carry = (tuple(tail_max), tuple(state), tb, jnp.int32(MAX_BLOCKS))
    done = 0
    for count in MOBA_STEPS_PER_TRIP:
        trips = (n_steps - done) // count
        carry = lax.fori_loop(0, trips, steps(count, done), carry)
        done = done + trips * count
    _, stats, _, _ = lax.fori_loop(0, n_steps - done, last_step_alone, carry)
    row = lax.broadcasted_iota(jnp.int32, (LANES, mb), 0)
    for p in range(MOBA_HEADS_PER_STEP // 2):
        (_, acc_a), (_, acc_b) = stats[2 * p], stats[2 * p + 1]
        o_t = jnp.where(row < hd, acc_a[:LANES] / acc_a[LANES:LANES + 1], acc_b[:LANES] / acc_b[LANES:LANES + 1])
        o_ref[0, :, p * LANES:(p + 1) * LANES] = o_t.T.astype(BF16)


def _moba(qm, km, vt, sel, tab):
    b, s, _ = qm.shape
    nh = MOBA_HEADS_PER_STEP
    width = nh * MOBA_HEAD_DIM
    nblk = s // MOBA_BLOCK
    grid = (b, MOBA_HEADS // nh, nblk)
    return pl.pallas_call(
        _moba_kernel,
        grid=grid,
        in_specs=[
            pl.BlockSpec((1, MOBA_BLOCK, width), lambda bi, hg, c: (bi, c, hg)),
            pl.BlockSpec((1, s, width), lambda bi, hg, c: (bi, 0, hg)),
            pl.BlockSpec((1, nblk, width, MOBA_BLOCK), lambda bi, hg, c: (bi, 0, hg, 0)),
            pl.BlockSpec((1, nh, MAX_BLOCKS, MOBA_BLOCK), lambda bi, hg, c: (bi, hg, 0, c)),
            pl.BlockSpec((nh, 3, MOBA_BLOCK, MOBA_BLOCK), lambda bi, hg, c: (hg, 0, 0, 0),
                         pipeline_mode=pl.Buffered(1 if nh == MOBA_HEADS else 2)),
        ],
        out_specs=pl.BlockSpec((1, MOBA_BLOCK, width), lambda bi, hg, c: (bi, c, hg)),
        out_shape=jax.ShapeDtypeStruct((b, s, MOBA_W), BF16),
        scratch_shapes=[pltpu.VMEM((nh, MAX_BLOCKS + ZERO_ROWS, MOBA_BLOCK), F32),
                        pltpu.VMEM((nh, MAX_BLOCKS, MOBA_BLOCK), F32),
                        pltpu.VMEM((2, nh, MOBA_KEYS_PER_STEP, MOBA_BLOCK), F32)],
        compiler_params=_cparams(3),
        name="moba",
    )(qm, km, vt, sel, tab)


def _local_kernel(qs_ref, ks_ref, kh_ref, vst_ref, vsh_ref, tab_ref, sink_ref, qe_ref, kme_ref, vmt_ref,
                  osw_ref, ome_ref):
    i = pl.program_id(1)
    qb = SWA_BLOCK
    pairs_per_kv = SWA_Q_HEADS // SWA_KV_HEADS // 2
    no_prev = jnp.where(i == 0, NEG, 0.0)
    row = lax.broadcasted_iota(jnp.int32, (LANES, qb), 0)
    pending = {}

    def swa_unit(blk, hk, j, par):
        rows = slice(blk * qb, (blk + 1) * qb)
        vrows = slice(hk * LANES, (hk + 1) * LANES)
        kc = slice((2 * hk + par) * LANES, (2 * hk + par + 1) * LANES)
        h = 2 * j + par

        def scores():
            if blk == 0:
                kk = jnp.concatenate([kh_ref[0, :, kc], ks_ref[0, 0:qb, kc]], axis=0)
            else:
                kk = ks_ref[0, (blk - 1) * qb:(blk + 1) * qb, kc]
            s = _nt(kk, qs_ref[0, rows, j * LANES:(j + 1) * LANES]) + tab_ref[h]
            if blk == 0:
                s = jnp.concatenate([s[:qb] + no_prev, s[qb:]], axis=0)
            return s

        def finish(s):
            v_prev = vsh_ref[0, 0, vrows] if blk == 0 else vst_ref[0, blk - 1, vrows]
            vv = jnp.concatenate([v_prev, vst_ref[0, blk, vrows]], axis=1)
            sink = sink_ref[h:h + 1, :]
            m = jnp.maximum(jnp.max(s, axis=0, keepdims=True), sink)
            p = jnp.exp2(s - m)
            den = jnp.sum(p, axis=0, keepdims=True) + jnp.exp2(sink - m)
            out = _mm(vv, p.astype(BF16)) / den
            if par == 0:
                pending[(blk, j)] = out
            else:
                o_t = jnp.where(row < SWA_HEAD_DIM, pending.pop((blk, j)), out)
                osw_ref[0, rows, j * LANES:(j + 1) * LANES] = o_t.T.astype(BF16)

        return scores, finish

    scale = LOG2E / math.sqrt(MEM_HEAD_DIM)
    mq = qb

    def mem_unit(blk, hd):
        rows = slice(blk * mq, (blk + 1) * mq)
        cols = slice(hd * MEM_HEAD_DIM, (hd + 1) * MEM_HEAD_DIM)

        def scores():
            return _nt(kme_ref[0, :, cols], qe_ref[0, rows, cols]) * scale

        def finish(s):
            m = jnp.max(s, axis=0, keepdims=True)
            p = jnp.exp2(s - m)
            den = jnp.sum(p, axis=0, keepdims=True)
            ome_ref[0, rows, cols] = (_mm(vmt_ref[0, cols, :], p.astype(BF16)) / den).T.astype(BF16)

        return scores, finish

    units = [swa_unit(blk, hk, hk * pairs_per_kv + jj, par) for blk in range(ROW_TILE // qb)
             for hk in range(SWA_KV_HEADS) for jj in range(pairs_per_kv) for par in range(2)]
    units += [mem_unit(blk, hd) for hd in range(MEM_HEADS) for blk in range(ROW_TILE // mq)]
    s_next = units[0][0]()
    for u, (_, finish) in enumerate(units):
        s = s_next
        if u + 1 < len(units):
            s_next = units[u + 1][0]()
        finish(s)


def _local(qs, ks, vst, tab, sinks, qe, kme, vmt):
    b, s, _ = qs.shape
    per_tile = ROW_TILE // SWA_BLOCK
    m_len = kme.shape[1]
    tile = lambda n: pl.BlockSpec((1, ROW_TILE, n), lambda bi, i: (bi, i, 0))
    prev_block = lambda bi, i: (bi, jnp.maximum(i * per_tile - 1, 0), 0)
    return pl.pallas_call(
        _local_kernel,
        grid=(b, s // ROW_TILE),
        in_specs=[
            tile(SWA_QW),
            tile(SWA_K_COLS), pl.BlockSpec((1, SWA_BLOCK, SWA_K_COLS), prev_block),
            pl.BlockSpec((1, per_tile, SWA_VT_ROWS, SWA_BLOCK), lambda bi, i: (bi, i, 0, 0)),
            pl.BlockSpec((1, 1, SWA_VT_ROWS, SWA_BLOCK), lambda bi, i: prev_block(bi, i) + (0,)),
            _resident((SWA_Q_HEADS, 2 * SWA_BLOCK, SWA_BLOCK)),
            _resident((SWA_Q_HEADS, LANES)),
            tile(MEM_W),
            pl.BlockSpec((1, m_len, MEM_W), lambda bi, i: (bi, 0, 0)),
            pl.BlockSpec((1, MEM_W, m_len), lambda bi, i: (bi, 0, 0)),
        ],
        out_specs=(tile(SWA_QW), tile(MEM_W)),
        out_shape=(jax.ShapeDtypeStruct((b, s, SWA_QW), BF16), jax.ShapeDtypeStruct((b, s, MEM_W), BF16)),
        compiler_params=_cparams(2),
        name="local",
    )(qs, ks, ks, vst, vst, tab, sinks, qe, kme, vmt)


def _merge_kernel(x_ref, omb_ref, osw_ref, ome_ref, gpre_ref, gpost_ref, wg_ref, wb_ref, wo_ref, out_ref):
    x = x_ref[...]
    h = _rms(x, gpre_ref[...]).astype(BF16)
    branch = (omb_ref[...], osw_ref[...], ome_ref[...])
    half = D_MODEL // 2
    parts = []
    for nc in range(2):
        acc = None
        for br in range(3):
            c0 = br * D_MODEL + nc * half
            gate = jax.nn.sigmoid(_mm(h, wg_ref[:, c0:c0 + half]))
            term = gate * _mm(branch[br], wb_ref[br, :, nc * half:(nc + 1) * half])
            acc = term if acc is None else acc + term
        parts.append(acc.astype(BF16))
    y = _mm(jnp.concatenate(parts, axis=1), wo_ref[...])
    out_ref[...] = x + _rms(y, gpost_ref[...])


def _merge(x2, omb, osw, ome, gpre, gpost, wg, wb, wo):
    t = x2.shape[0]
    row = lambda n: pl.BlockSpec((MERGE_TILE, n), lambda i: (i, 0))
    return pl.pallas_call(
        _merge_kernel,
        grid=(t // MERGE_TILE,),
        in_specs=[row(D_MODEL), row(MOBA_W), row(SWA_QW), row(MEM_W),
                  _resident((1, D_MODEL)), _resident((1, D_MODEL)),
                  _resident((D_MODEL, 3 * D_MODEL)), _resident((3, MOBA_W, D_MODEL)),
                  _resident((D_MODEL, D_MODEL))],
        out_specs=row(D_MODEL),
        out_shape=jax.ShapeDtypeStruct((t, D_MODEL), F32),
        compiler_params=_cparams(1),
        name="merge",
    )(x2, omb, osw, ome, gpre, gpost, wg, wb, wo)


def _ffn_kernel(x_ref, xh_ref, gpre_ref, gpost_ref, wup_ref, cw_ref, cb_ref, wd_ref, out_ref, acc_ref, act_ref,
                *, tiles_per_seq):
    i = pl.program_id(0)
    x = x_ref[...]
    xe = jnp.concatenate([xh_ref[...], x], axis=0)
    he = _rms(xe, gpre_ref[...])
    row = lax.broadcasted_iota(jnp.int32, (ROW_TILE + HALO, 1), 0)
    seq_start = (i % tiles_per_seq) == 0
    he = jnp.where((row < HALO) & seq_start, 0.0, he).astype(BF16)

    def up(fc):
        c0 = fc * FFN_CHUNK
        return (_mm(he, wup_ref[:, c0:c0 + FFN_CHUNK]),
                _mm(he, wup_ref[:, FFN_HIDDEN + c0:FFN_HIDDEN + c0 + FFN_CHUNK]))

    def conv(u, c0):
        w = cw_ref[:, c0:c0 + FFN_CHUNK]
        out = cb_ref[:, c0:c0 + FFN_CHUNK] + w[0:1] * pltpu.roll(u, 2, 0)
        out = out + w[1:2] * pltpu.roll(u, 1, 0)
        out = out + w[2:3] * u
        return out[HALO:]

    n_chunks = FFN_HIDDEN // FFN_CHUNK
    u_gate, u_val = up(0)
    for fc in range(n_chunks):
        c0 = fc * FFN_CHUNK
        ahead = up(fc + 1) if fc + 1 < n_chunks else None
        group, k = divmod(fc, FFN_GROUP)
        act = _gelu_times(conv(u_gate, c0), conv(u_val, FFN_HIDDEN + c0))
        act_ref[group % 2, :, k * FFN_CHUNK:(k + 1) * FFN_CHUNK] = act.astype(BF16)
        if k == FFN_GROUP - 1 or ahead is None:
            g0 = group * FFN_GROUP * FFN_CHUNK
            width = (k + 1) * FFN_CHUNK
            part = _mm(act_ref[group % 2, :, :width], wd_ref[g0:g0 + width, :])
            if group == 0:
                acc_ref[...] = part
            else:
                acc_ref[...] += part
        if ahead is not None:
            u_gate, u_val = ahead
    out_ref[...] = x + _rms(acc_ref[...], gpost_ref[...])


def _ffn(x2, gpre, gpost, wup, cw, cb, wd, seq_len):
    t = x2.shape[0]
    row = pl.BlockSpec((ROW_TILE, D_MODEL), lambda i: (i, 0))
    halo = pl.BlockSpec((HALO, D_MODEL), lambda i: (jnp.maximum(i * (ROW_TILE // HALO) - 1, 0), 0))
    return pl.pallas_call(
        functools.partial(_ffn_kernel, tiles_per_seq=seq_len // ROW_TILE),
        grid=(t // ROW_TILE,),
        in_specs=[row, halo, _resident((1, D_MODEL)), _resident((1, D_MODEL)),
                  _resident((D_MODEL, 2 * FFN_HIDDEN)), _resident((CONV_WIDTH, 2 * FFN_HIDDEN)),
                  _resident((1, 2 * FFN_HIDDEN)), _resident((FFN_HIDDEN, D_MODEL))],
        out_specs=row,
        out_shape=jax.ShapeDtypeStruct((t, D_MODEL), F32),
        scratch_shapes=[pltpu.VMEM((ROW_TILE, D_MODEL), F32),
                        pltpu.VMEM((2, ROW_TILE, FFN_GROUP * FFN_CHUNK), BF16)],
        compiler_params=_cparams(1),
        name="ffn",
    )(x2, x2, gpre, gpost, wup, cw, cb, wd)


def _t5_bucket(dist):
    n = jnp.maximum(dist, 0)
    nf = jnp.maximum(n, 1).astype(F32)
    large = MAX_EXACT + (jnp.log(nf / MAX_EXACT) / math.log(REL_MAX_DISTANCE / MAX_EXACT)
                         * (NUM_BUCKETS - MAX_EXACT)).astype(jnp.int32)
    large = jnp.minimum(large, NUM_BUCKETS - 1)
    return jnp.where(n < MAX_EXACT, n, large)


def _bias_tables(rel_bias):
    bm = rel_bias[:, :MOBA_HEADS].T.astype(F32)
    bs = rel_bias[:, MOBA_HEADS:].T.astype(F32)

    def lookup(table, dist):
        bucket = _t5_bucket(dist)[None]
        out = jnp.zeros((table.shape[0],) + dist.shape, F32)
        for bkt in range(NUM_BUCKETS):
            out = jnp.where(bucket == bkt, table[:, bkt][:, None], out)
        return out

    def deltas(rows, cols):
        p = rows + cols
        d = jnp.arange(p)
        return jnp.where(d < cols, d, d - p)

    def toeplitz(w, rows, cols):
        h, p = w.shape
        flat = jnp.broadcast_to(w[:, None, :], (h, rows, p)).reshape(h, rows * p)[:, :rows * (p - 1)]
        return flat.reshape(h, rows, p - 1)[:, :, :cols]

    mb = MOBA_BLOCK
    far = bm[:, NUM_BUCKETS - 1][:, None]
    d = deltas(mb, mb)
    own = toeplitz(jnp.where(d >= 0, (lookup(bm, d) - far) * LOG2E, NEG), mb, mb)
    prev = toeplitz((lookup(bm, mb + d) - far) * LOG2E, mb, mb)
    tab_moba = jnp.stack([prev, own, jnp.full_like(own, NEG)], axis=1)
    dist = SWA_BLOCK + deltas(2 * SWA_BLOCK, SWA_BLOCK)
    tab_swa = toeplitz(jnp.where((dist >= 0) & (dist < SWA_WINDOW), lookup(bs, dist) * LOG2E, NEG),
                       2 * SWA_BLOCK, SWA_BLOCK)
    return tab_moba, tab_swa


def _head_variants(w, heads):
    w3 = w.reshape(w.shape[0], heads, 1, -1)
    z = jnp.zeros_like(w3)
    lo = jnp.concatenate([w3, z], axis=3)
    hi = jnp.concatenate([z, w3], axis=3)
    return jnp.concatenate([lo, hi], axis=2).reshape(w.shape[0], heads * 2 * LANES)


def _proj_weight(w_in):
    o = 0
    parts = {}
    for name, width in (("qm", MOBA_W), ("km", MOBA_W), ("vm", MOBA_W), ("qs", SWA_QW), ("ks", SWA_KVW),
                        ("vs", SWA_KVW), ("qe", MEM_W)):
        parts[name] = w_in[:, o:o + width]
        o += width
    q_scale = LOG2E / math.sqrt(MOBA_HEAD_DIM)
    w = jnp.concatenate([parts["qm"] * q_scale, parts["km"], parts["qs"] * q_scale,
                         _head_variants(parts["ks"], SWA_KV_HEADS), parts["qe"]], axis=1)
    vs3 = parts["vs"].reshape(-1, SWA_KV_HEADS, SWA_HEAD_DIM)
    vs_dup = jnp.concatenate([vs3, vs3], axis=2).reshape(-1, SWA_VT_ROWS)
    w_t = jnp.concatenate([parts["vm"], vs_dup], axis=1).T
    return w.astype(BF16), w_t.astype(BF16), w_in[:, o:].astype(BF16)


def kernel(x, mem, norm_mix_pre, norm_mix_post, norm_ffn_pre, norm_ffn_post, norm_mem, w_in, rel_bias, swa_sinks,
           w_mem_kv, w_branch_moba, w_branch_swa, w_branch_mem, w_out, w_ffn_up, ffn_conv_w, ffn_conv_b,
           w_ffn_down):
    b, s, d = x.shape
    m_len = mem.shape[1]
    assert d == D_MODEL and s % ROW_TILE == 0 and s // MOBA_BLOCK <= 32 and m_len % LANES == 0
    depth = w_in.shape[0]
    tab_moba, tab_swa = _bias_tables(rel_bias)
    x2 = x.reshape(b * s, d).astype(F32)
    vec = lambda v: v.reshape(1, -1).astype(F32)
    nblk = s // MOBA_BLOCK
    r3 = lambda a: a.reshape(b, s, a.shape[-1])
    for l in range(depth):
        w_qkv, w_t, w_gate = _proj_weight(w_in[l])
        qm, km, vt, qs, ks, vst, qe, sel = _proj(x2, vec(norm_mix_pre[l]), w_qkv, w_t, s)
        kme, vmt = _memkv(mem.astype(F32), vec(norm_mem[l]), w_mem_kv[l][:, :MEM_W].astype(BF16),
                          w_mem_kv[l][:, MEM_W:].T.astype(BF16))
        o_mb = _moba(r3(qm), r3(km), vt.reshape(b, nblk, MOBA_W, MOBA_BLOCK), sel, tab_moba)
        sinks = jnp.broadcast_to(swa_sinks[l].astype(F32)[:, None] * LOG2E, (SWA_Q_HEADS, LANES))
        o_sw, o_me = _local(r3(qs), r3(ks), vst.reshape(b, s // SWA_BLOCK, SWA_VT_ROWS, SWA_BLOCK), tab_swa, sinks,
                            r3(qe), kme, vmt)
        w_branch = jnp.stack([w_branch_moba[l], w_branch_swa[l], w_branch_mem[l]]).astype(BF16)
        x2 = _merge(x2, o_mb.reshape(b * s, -1), o_sw.reshape(b * s, -1), o_me.reshape(b * s, -1),
                    vec(norm_mix_pre[l]), vec(norm_mix_post[l]), w_gate, w_branch, w_out[l].astype(BF16))
        x2 = _ffn(x2, vec(norm_ffn_pre[l]), vec(norm_ffn_post[l]), w_ffn_up[l].astype(BF16),
                  ffn_conv_w[l].astype(F32), vec(ffn_conv_b[l]), w_ffn_down[l].astype(BF16), s)
    return x2.reshape(b, s, d).astype(x.dtype)
```

```python
import functools
import math

import jax
import jax.numpy as jnp
from jax import lax
from jax.experimental import pallas as pl
from jax.experimental.pallas import tpu as pltpu

D_MODEL = 1024
MOBA_HEADS = 8
MOBA_HEAD_DIM = 64
MOBA_BLOCK = 256
MOBA_TOPK = 3
SWA_Q_HEADS = 8
SWA_KV_HEADS = 2
SWA_HEAD_DIM = 64
SWA_WINDOW = 128
SWA_BLOCK = 128
MEM_HEADS = 4
MEM_HEAD_DIM = 128
NUM_BUCKETS = 32
MAX_EXACT = NUM_BUCKETS // 2
REL_MAX_DISTANCE = 128
FFN_HIDDEN = 2816
CONV_WIDTH = 3
RMS_EPS = 1e-6

MOBA_W = MOBA_HEADS * MOBA_HEAD_DIM
SWA_QW = SWA_Q_HEADS * SWA_HEAD_DIM
SWA_KVW = SWA_KV_HEADS * SWA_HEAD_DIM
MEM_W = MEM_HEADS * MEM_HEAD_DIM

LANES = 128
NEG = -1e30
VMEM_LIMIT = 56 * 1024 * 1024
ROW_TILE = 1024
MERGE_TILE = 1024
MOBA_KEYS_PER_STEP = 512
MOBA_STEPS_PER_TRIP = (4, 2)
MOBA_HEADS_PER_STEP = 8
FFN_CHUNK = 256
FFN_GROUP = 4
HALO = 8
SUM_ROWS = 16
MAX_BLOCKS = 32
ZERO_ROWS = 8
STAT_ROWS = 8

BF16 = jnp.bfloat16
F32 = jnp.float32

SWA_K_COLS = 2 * SWA_KV_HEADS * LANES
SWA_VT_ROWS = SWA_KV_HEADS * LANES
OFF_QM = 0
OFF_KM = OFF_QM + MOBA_W
OFF_QS = OFF_KM + MOBA_W
OFF_KS = OFF_QS + SWA_QW
OFF_QE = OFF_KS + SWA_K_COLS
PROJ_W = OFF_QE + MEM_W
LOG2E = math.log2(math.e)


def _mm(a, b):
    return jnp.dot(a, b, preferred_element_type=F32)


def _nt(a, b):
    return lax.dot_general(a, b, (((1,), (1,)), ((), ())), preferred_element_type=F32)


def _rms(xf, g):
    r = lax.rsqrt(jnp.mean(xf * xf, axis=-1, keepdims=True) + RMS_EPS)
    return xf * r * g


def _gelu_times(g, v):
    a = math.sqrt(2.0 / math.pi)
    minus_2z_log2 = g * ((-2.0 * LOG2E * a) + (-2.0 * LOG2E * a * 0.044715) * (g * g))
    return (g * v) / (1.0 + jnp.exp2(minus_2z_log2))


def _cparams(n_axes):
    return pltpu.CompilerParams(dimension_semantics=("arbitrary",) * n_axes, vmem_limit_bytes=VMEM_LIMIT)


def _resident(shape):
    nd = len(shape)
    return pl.BlockSpec(shape, lambda *_: (0,) * nd, pipeline_mode=pl.Buffered(1))


def _proj_kernel(x_ref, g_ref, w_ref, wt_ref, qm_ref, km_ref, vt_ref, qs_ref, ks_ref, vst_ref, qe_ref, sel_ref,
                 kmean_ref, *, tiles_per_seq):
    i = pl.program_id(0)
    h = _rms(x_ref[...], g_ref[...]).astype(BF16)

    def seg(c0, n):
        return _mm(h, w_ref[:, c0:c0 + n])

    q = seg(OFF_QM, MOBA_W).astype(BF16)
    qm_ref[...] = q
    k = seg(OFF_KM, MOBA_W)
    km_ref[...] = k.astype(BF16)
    qs_ref[...] = seg(OFF_QS, SWA_QW).astype(BF16)
    ks_ref[...] = seg(OFF_KS, SWA_K_COLS).astype(BF16)
    qe_ref[...] = seg(OFF_QE, MEM_W).astype(BF16)
    vt = _nt(wt_ref[...], h).astype(BF16)
    for r in range(ROW_TILE // SWA_BLOCK):
        vst_ref[r] = vt[MOBA_W:, r * SWA_BLOCK:(r + 1) * SWA_BLOCK]

    @pl.when(i % tiles_per_seq == 0)
    def _():
        kmean_ref[...] = jnp.zeros_like(kmean_ref)

    blocks_per_tile = ROW_TILE // MOBA_BLOCK
    c0 = (i % tiles_per_seq) * blocks_per_tile
    lane = lax.broadcasted_iota(jnp.int32, (1, MOBA_W), 1)
    for r in range(blocks_per_tile):
        rows = slice(r * MOBA_BLOCK, (r + 1) * MOBA_BLOCK)
        vt_ref[r] = vt[:MOBA_W, rows]
        mean = jnp.sum(k[rows], axis=0, keepdims=True) * (1.0 / MOBA_BLOCK)
        for hd in range(MOBA_HEADS):
            in_head = (lane >= hd * MOBA_HEAD_DIM) & (lane < (hd + 1) * MOBA_HEAD_DIM)
            kmean_ref[pl.ds(hd * MAX_BLOCKS + c0 + r, 1), :] = jnp.where(in_head, mean, 0.0)

    km = kmean_ref[...]
    km_hi = km.astype(BF16)
    km_lo = (km - km_hi.astype(F32)).astype(BF16)
    gates = _nt(km_hi, q) + _nt(km_lo, q)
    blk = lax.broadcasted_iota(jnp.int32, (MAX_BLOCKS, ROW_TILE), 0)
    blk_f = blk.astype(F32)
    col = lax.broadcasted_iota(jnp.int32, (MAX_BLOCKS, ROW_TILE), 1)
    q_blk = c0 + sum((col >= r * MOBA_BLOCK).astype(jnp.int32) for r in range(1, blocks_per_tile))
    past = blk < q_blk
    for hd in range(MOBA_HEADS):
        g = jnp.where(past, gates[hd * MAX_BLOCKS:(hd + 1) * MAX_BLOCKS], -jnp.inf)
        sel = blk < 0
        for _ in range(MOBA_TOPK):
            top = jnp.max(g, axis=0, keepdims=True)
            first = jnp.min(jnp.where(g == top, blk_f, 1e9), axis=0, keepdims=True)
            hit = blk_f == first
            sel = sel | hit
            g = jnp.where(hit, -jnp.inf, g)
        sel_ref[0, hd] = jnp.where(sel & past, 1.0, 0.0)


def _proj(x2, g, w, wt, seq_len):
    t = x2.shape[0]
    n_tiles = t // ROW_TILE
    tiles_per_seq = seq_len // ROW_TILE
    bpt = ROW_TILE // MOBA_BLOCK
    spt = ROW_TILE // SWA_BLOCK
    row = lambda n: pl.BlockSpec((ROW_TILE, n), lambda i: (i, 0))
    out_shapes = (
        jax.ShapeDtypeStruct((t, MOBA_W), BF16),
        jax.ShapeDtypeStruct((t, MOBA_W), BF16),
        jax.ShapeDtypeStruct((t // MOBA_BLOCK, MOBA_W, MOBA_BLOCK), BF16),
        jax.ShapeDtypeStruct((t, SWA_QW), BF16),
        jax.ShapeDtypeStruct((t, SWA_K_COLS), BF16),
        jax.ShapeDtypeStruct((t // SWA_BLOCK, SWA_VT_ROWS, SWA_BLOCK), BF16),
        jax.ShapeDtypeStruct((t, MEM_W), BF16),
        jax.ShapeDtypeStruct((t // seq_len, MOBA_HEADS, MAX_BLOCKS, seq_len), F32),
    )
    out_specs = (row(MOBA_W), row(MOBA_W), pl.BlockSpec((bpt, MOBA_W, MOBA_BLOCK), lambda i: (i, 0, 0)),
                 row(SWA_QW), row(SWA_K_COLS), pl.BlockSpec((spt, SWA_VT_ROWS, SWA_BLOCK), lambda i: (i, 0, 0)),
                 row(MEM_W),
                 pl.BlockSpec((1, MOBA_HEADS, MAX_BLOCKS, ROW_TILE),
                              lambda i: (i // tiles_per_seq, 0, 0, i % tiles_per_seq)))
    return pl.pallas_call(
        functools.partial(_proj_kernel, tiles_per_seq=tiles_per_seq),
        grid=(n_tiles,),
        in_specs=[row(D_MODEL), _resident((1, D_MODEL)), _resident((D_MODEL, PROJ_W)),
                  _resident((MOBA_W + SWA_VT_ROWS, D_MODEL))],
        out_specs=out_specs,
        out_shape=out_shapes,
        scratch_shapes=[pltpu.VMEM((MOBA_HEADS * MAX_BLOCKS, MOBA_W), F32)],
        compiler_params=_cparams(1),
        name="proj",
    )(x2, g, w, wt)


def _memkv_kernel(m_ref, g_ref, wk_ref, wvt_ref, k_ref, vt_ref):
    h = _rms(m_ref[0], g_ref[...]).astype(BF16)
    k_ref[0] = _mm(h, wk_ref[...]).astype(BF16)
    vt_ref[0] = _nt(wvt_ref[...], h).astype(BF16)


def _memkv(mem, g, wk, wvt):
    b, m_len, _ = mem.shape
    return pl.pallas_call(
        _memkv_kernel,
        grid=(b,),
        in_specs=[pl.BlockSpec((1, m_len, D_MODEL), lambda i: (i, 0, 0)), _resident((1, D_MODEL)),
                  _resident((D_MODEL, MEM_W)), _resident((MEM_W, D_MODEL))],
        out_specs=(pl.BlockSpec((1, m_len, MEM_W), lambda i: (i, 0, 0)),
                   pl.BlockSpec((1, MEM_W, m_len), lambda i: (i, 0, 0))),
        out_shape=(jax.ShapeDtypeStruct((b, m_len, MEM_W), BF16), jax.ShapeDtypeStruct((b, MEM_W, m_len), BF16)),
        compiler_params=_cparams(1),
        name="memkv",
    )(mem, g, wk, wvt)


def _moba_kernel(q_ref, k_ref, vt_ref, sel_ref, tab_ref, o_ref, mfar_ref, mtail_ref, s_ref, smax_ref, m_ref,
                 acc_ref):
    c = pl.program_id(2)
    mb = MOBA_BLOCK
    kb = MOBA_KEYS_PER_STEP
    per = kb // mb
    hd = MOBA_HEAD_DIM
    heads = range(MOBA_HEADS_PER_STEP)
    pair_cols = lambda a: slice((a // 2) * LANES, (a // 2 + 1) * LANES)
    lane_q = lax.broadcasted_iota(jnp.int32, (mb, LANES), 1)
    blk = lax.broadcasted_iota(jnp.int32, (MAX_BLOCKS, mb), 0)
    tb = jnp.maximum(c - 1, 0)
    kt = pl.multiple_of(tb * mb, mb)
    left = jnp.where(c == 0, 1, 0)
    right = jnp.where(c == 0, 2, 1)

    ones = jnp.ones((SUM_ROWS, mb), BF16)

    def values(block, a):
        return jnp.concatenate([vt_ref[0, block, pair_cols(a)], ones], axis=0)

    q_pad = []
    for a in heads:
        half = a % 2
        qp = q_ref[0, :, pair_cols(a)]
        qa = jnp.where((lane_q >= half * hd) & (lane_q < (half + 1) * hd), qp, jnp.zeros_like(qp))
        q_pad.append(qa)
        sel = sel_ref[0, a] > 0.5
        mfar_ref[a, :MAX_BLOCKS] = jnp.where(sel & (blk <= c - 2), 0.0, NEG)
        mfar_ref[a, MAX_BLOCKS:] = jnp.zeros((ZERO_ROWS, mb), F32)
        mtail_ref[a] = jnp.where((blk == c) | (sel & (blk == c - 1)), 0.0, NEG)

        s = _nt(k_ref[0, pl.ds(kt, 2 * mb), pair_cols(a)], qa)
        s = jnp.concatenate([s[:mb] + tab_ref[a, left] + mtail_ref[a, pl.ds(tb, 1), :],
                             s[mb:] + tab_ref[a, right]], axis=0)
        s_ref[0, a] = s
        smax_ref[0, a] = jnp.broadcast_to(jnp.max(s, axis=0, keepdims=True), (STAT_ROWS, mb))
        m_ref[a] = jnp.full((STAT_ROWS, mb), -jnp.inf, F32)
        acc_ref[a] = jnp.zeros((LANES + SUM_ROWS, mb), F32)

    n_far = jnp.where(c >= 2, ((c - 1) * mb + kb - 1) // kb, 0)
    last_step = k_ref.shape[1] // kb - 1

    def far_scores(j, slot):
        rows = pl.ds(pl.multiple_of(j * kb, kb), kb)
        for a in heads:
            s = _nt(k_ref[0, rows, pair_cols(a)], q_pad[a])
            s_ref[slot, a] = s
            block_max = [jnp.max(s[t * mb:(t + 1) * mb], axis=0, keepdims=True)
                         + mfar_ref[a, pl.ds(j * per + t, 1), :] for t in range(per)]
            smax_ref[slot, a] = jnp.broadcast_to(functools.reduce(jnp.maximum, block_max), (STAT_ROWS, mb))

    def absorb(slot, block, mask_row):
        for a in heads:
            m = m_ref[a, 0:1]
            m_new = jnp.maximum(m, smax_ref[slot, a, 0:1])
            acc = jnp.exp2(m - m_new) * acc_ref[a]
            for t in range(per):
                shift = mfar_ref[a, pl.ds(mask_row + t, 1), :] - m_new
                pb = jnp.exp2(s_ref[slot, a, t * mb:(t + 1) * mb] + shift).astype(BF16)
                acc = acc + _mm(values(block + t, a), pb)
            acc_ref[a] = acc
            m_ref[a] = jnp.broadcast_to(m_new, (STAT_ROWS, mb))

    def steps(count, first):
        def trip(i, carry):
            block, mask_row = carry
            for u in range(count):
                j = first + count * i + u
                far_scores(jnp.minimum(j, last_step), (u + 1) % 2)
                absorb(u % 2, block, mask_row)
                block, mask_row = j * per, j * per
            return block, mask_row
        return trip

    def last_step_alone(_, carry):
        absorb(0, *carry)
        return carry

    n_steps = n_far + 1
    carry = (tb, jnp.int32(MAX_BLOCKS))
    done = 0
    for count in MOBA_STEPS_PER_TRIP:
        trips = (n_steps - done) // count
        carry = lax.fori_loop(0, trips, steps(count, done), carry)
        done = done + trips * count
    lax.fori_loop(0, n_steps - done, last_step_alone, carry)
    row = lax.broadcasted_iota(jnp.int32, (LANES, mb), 0)
    for p in range(MOBA_HEADS_PER_STEP // 2):
        acc_a, acc_b = acc_ref[2 * p], acc_ref[2 * p + 1]
        o_t = jnp.where(row < hd, acc_a[:LANES] / acc_a[LANES:LANES + 1], acc_b[:LANES] / acc_b[LANES:LANES + 1])
        o_ref[0, :, p * LANES:(p + 1) * LANES] = o_t.T.astype(BF16)


def _moba(qm, km, vt, sel, tab):
    b, s, _ = qm.shape
    nh = MOBA_HEADS_PER_STEP
    width = nh * MOBA_HEAD_DIM
    nblk = s // MOBA_BLOCK
    grid = (b, MOBA_HEADS // nh, nblk)
    return pl.pallas_call(
        _moba_kernel,
        grid=grid,
        in_specs=[
            pl.BlockSpec((1, MOBA_BLOCK, width), lambda bi, hg, c: (bi, c, hg)),
            pl.BlockSpec((1, s, width), lambda bi, hg, c: (bi, 0, hg)),
            pl.BlockSpec((1, nblk, width, MOBA_BLOCK), lambda bi, hg, c: (bi, 0, hg, 0)),
            pl.BlockSpec((1, nh, MAX_BLOCKS, MOBA_BLOCK), lambda bi, hg, c: (bi, hg, 0, c)),
            pl.BlockSpec((nh, 3, MOBA_BLOCK, MOBA_BLOCK), lambda bi, hg, c: (hg, 0, 0, 0),
                         pipeline_mode=pl.Buffered(1 if nh == MOBA_HEADS else 2)),
        ],
        out_specs=pl.BlockSpec((1, MOBA_BLOCK, width), lambda bi, hg, c: (bi, c, hg)),
        out_shape=jax.ShapeDtypeStruct((b, s, MOBA_W), BF16),
        scratch_shapes=[pltpu.VMEM((nh, MAX_BLOCKS + ZERO_ROWS, MOBA_BLOCK), F32),
                        pltpu.VMEM((nh, MAX_BLOCKS, MOBA_BLOCK), F32),
                        pltpu.VMEM((2, nh, MOBA_KEYS_PER_STEP, MOBA_BLOCK), F32),
                        pltpu.VMEM((2, nh, STAT_ROWS, MOBA_BLOCK), F32),
                        pltpu.VMEM((nh, STAT_ROWS, MOBA_BLOCK), F32),
                        pltpu.VMEM((nh, LANES + SUM_ROWS, MOBA_BLOCK), F32)],
        compiler_params=_cparams(3),
        name="moba",
    )(qm, km, vt, sel, tab)


def _local_kernel(qs_ref, ks_ref, kh_ref, vst_ref, vsh_ref, tab_ref, sink_ref, qe_ref, kme_ref, vmt_ref,
                  osw_ref, ome_ref):
    i = pl.program_id(1)
    qb = SWA_BLOCK
    pairs_per_kv = SWA_Q_HEADS // SWA_KV_HEADS // 2
    no_prev = jnp.where(i == 0, NEG, 0.0)
    row = lax.broadcasted_iota(jnp.int32, (LANES, qb), 0)
    pending = {}

    def swa_unit(blk, hk, j, par):
        rows = slice(blk * qb, (blk + 1) * qb)
        vrows = slice(hk * LANES, (hk + 1) * LANES)
        kc = slice((2 * hk + par) * LANES, (2 * hk + par + 1) * LANES)
        h = 2 * j + par

        def scores():
            if blk == 0:
                kk = jnp.concatenate([kh_ref[0, :, kc], ks_ref[0, 0:qb, kc]], axis=0)
            else:
                kk = ks_ref[0, (blk - 1) * qb:(blk + 1) * qb, kc]
            s = _nt(kk, qs_ref[0, rows, j * LANES:(j + 1) * LANES]) + tab_ref[h]
            if blk == 0:
                s = jnp.concatenate([s[:qb] + no_prev, s[qb:]], axis=0)
            return s

        def finish(s):
            v_prev = vsh_ref[0, 0, vrows] if blk == 0 else vst_ref[0, blk - 1, vrows]
            vv = jnp.concatenate([v_prev, vst_ref[0, blk, vrows]], axis=1)
            sink = sink_ref[h:h + 1, :]
            m = jnp.maximum(jnp.max(s, axis=0, keepdims=True), sink)
            p = jnp.exp2(s - m)
            den = jnp.sum(p, axis=0, keepdims=True) + jnp.exp2(sink - m)
            out = _mm(vv, p.astype(BF16)) / den
            if par == 0:
                pending[(blk, j)] = out
            else:
                o_t = jnp.where(row < SWA_HEAD_DIM, pending.pop((blk, j)), out)
                osw_ref[0, rows, j * LANES:(j + 1) * LANES] = o_t.T.astype(BF16)

        return scores, finish

    scale = LOG2E / math.sqrt(MEM_HEAD_DIM)
    mq = qb

    def mem_unit(blk, hd):
        rows = slice(blk * mq, (blk + 1) * mq)
        cols = slice(hd * MEM_HEAD_DIM, (hd + 1) * MEM_HEAD_DIM)

        def scores():
            return _nt(kme_ref[0, :, cols], qe_ref[0, rows, cols]) * scale

        def finish(s):
            m = jnp.max(s, axis=0, keepdims=True)
            p = jnp.exp2(s - m)
            den = jnp.sum(p, axis=0, keepdims=True)
            ome_ref[0, rows, cols] = (_mm(vmt_ref[0, cols, :], p.astype(BF16)) / den).T.astype(BF16)

        return scores, finish

    units = [swa_unit(blk, hk, hk * pairs_per_kv + jj, par) for blk in range(ROW_TILE // qb)
             for hk in range(SWA_KV_HEADS) for jj in range(pairs_per_kv) for par in range(2)]
    units += [mem_unit(blk, hd) for hd in range(MEM_HEADS) for blk in range(ROW_TILE // mq)]
    s_next = units[0][0]()
    for u, (_, finish) in enumerate(units):
        s = s_next
        if u + 1 < len(units):
            s_next = units[u + 1][0]()
        finish(s)


def _local(qs, ks, vst, tab, sinks, qe, kme, vmt):
    b, s, _ = qs.shape
    per_tile = ROW_TILE // SWA_BLOCK
    m_len = kme.shape[1]
    tile = lambda n: pl.BlockSpec((1, ROW_TILE, n), lambda bi, i: (bi, i, 0))
    prev_block = lambda bi, i: (bi, jnp.maximum(i * per_tile - 1, 0), 0)
    return pl.pallas_call(
        _local_kernel,
        grid=(b, s // ROW_TILE),
        in_specs=[
            tile(SWA_QW),
            tile(SWA_K_COLS), pl.BlockSpec((1, SWA_BLOCK, SWA_K_COLS), prev_block),
            pl.BlockSpec((1, per_tile, SWA_VT_ROWS, SWA_BLOCK), lambda bi, i: (bi, i, 0, 0)),
            pl.BlockSpec((1, 1, SWA_VT_ROWS, SWA_BLOCK), lambda bi, i: prev_block(bi, i) + (0,)),
            _resident((SWA_Q_HEADS, 2 * SWA_BLOCK, SWA_BLOCK)),
            _resident((SWA_Q_HEADS, LANES)),
            tile(MEM_W),
            pl.BlockSpec((1, m_len, MEM_W), lambda bi, i: (bi, 0, 0)),
            pl.BlockSpec((1, MEM_W, m_len), lambda bi, i: (bi, 0, 0)),
        ],
        out_specs=(tile(SWA_QW), tile(MEM_W)),
        out_shape=(jax.ShapeDtypeStruct((b, s, SWA_QW), BF16), jax.ShapeDtypeStruct((b, s, MEM_W), BF16)),
        compiler_params=_cparams(2),
        name="local",
    )(qs, ks, ks, vst, vst, tab, sinks, qe, kme, vmt)


def _merge_kernel(x_ref, omb_ref, osw_ref, ome_ref, gpre_ref, gpost_ref, wg_ref, wb_ref, wo_ref, out_ref):
    x = x_ref[...]
    h = _rms(x, gpre_ref[...]).astype(BF16)
    branch = (omb_ref[...], osw_ref[...], ome_ref[...])
    half = D_MODEL // 2
    parts = []
    for nc in range(2):
        acc = None
        for br in range(3):
            c0 = br * D_MODEL + nc * half
            gate = jax.nn.sigmoid(_mm(h, wg_ref[:, c0:c0 + half]))
            term = gate * _mm(branch[br], wb_ref[br, :, nc * half:(nc + 1) * half])
            acc = term if acc is None else acc + term
        parts.append(acc.astype(BF16))
    y = _mm(jnp.concatenate(parts, axis=1), wo_ref[...])
    out_ref[...] = x + _rms(y, gpost_ref[...])


def _merge(x2, omb, osw, ome, gpre, gpost, wg, wb, wo):
    t = x2.shape[0]
    row = lambda n: pl.BlockSpec((MERGE_TILE, n), lambda i: (i, 0))
    return pl.pallas_call(
        _merge_kernel,
        grid=(t // MERGE_TILE,),
        in_specs=[row(D_MODEL), row(MOBA_W), row(SWA_QW), row(MEM_W),
                  _resident((1, D_MODEL)), _resident((1, D_MODEL)),
                  _resident((D_MODEL, 3 * D_MODEL)), _resident((3, MOBA_W, D_MODEL)),
                  _resident((D_MODEL, D_MODEL))],
        out_specs=row(D_MODEL),
        out_shape=jax.ShapeDtypeStruct((t, D_MODEL), F32),
        compiler_params=_cparams(1),
        name="merge",
    )(x2, omb, osw, ome, gpre, gpost, wg, wb, wo)


def _ffn_kernel(x_ref, xh_ref, gpre_ref, gpost_ref, wup_ref, cw_ref, cb_ref, wd_ref, out_ref, acc_ref, act_ref,
                *, tiles_per_seq):
    i = pl.program_id(0)
    x = x_ref[...]
    xe = jnp.concatenate([xh_ref[...], x], axis=0)
    he = _rms(xe, gpre_ref[...])
    row = lax.broadcasted_iota(jnp.int32, (ROW_TILE + HALO, 1), 0)
    seq_start = (i % tiles_per_seq) == 0
    he = jnp.where((row < HALO) & seq_start, 0.0, he).astype(BF16)

    def up(fc):
        c0 = fc * FFN_CHUNK
        return (_mm(he, wup_ref[:, c0:c0 + FFN_CHUNK]),
                _mm(he, wup_ref[:, FFN_HIDDEN + c0:FFN_HIDDEN + c0 + FFN_CHUNK]))

    def conv(u, c0):
        w = cw_ref[:, c0:c0 + FFN_CHUNK]
        out = cb_ref[:, c0:c0 + FFN_CHUNK] + w[0:1] * pltpu.roll(u, 2, 0)
        out = out + w[1:2] * pltpu.roll(u, 1, 0)
        out = out + w[2:3] * u
        return out[HALO:]

    n_chunks = FFN_HIDDEN // FFN_CHUNK
    u_gate, u_val = up(0)
    for fc in range(n_chunks):
        c0 = fc * FFN_CHUNK
        ahead = up(fc + 1) if fc + 1 < n_chunks else None
        group, k = divmod(fc, FFN_GROUP)
        act = _gelu_times(conv(u_gate, c0), conv(u_val, FFN_HIDDEN + c0))
        act_ref[group % 2, :, k * FFN_CHUNK:(k + 1) * FFN_CHUNK] = act.astype(BF16)
        if k == FFN_GROUP - 1 or ahead is None:
            g0 = group * FFN_GROUP * FFN_CHUNK
            width = (k + 1) * FFN_CHUNK
            part = _mm(act_ref[group % 2, :, :width], wd_ref[g0:g0 + width, :])
            if group == 0:
                acc_ref[...] = part
            else:
                acc_ref[...] += part
        if ahead is not None:
            u_gate, u_val = ahead
    out_ref[...] = x + _rms(acc_ref[...], gpost_ref[...])


def _ffn(x2, gpre, gpost, wup, cw, cb, wd, seq_len):
    t = x2.shape[0]
    row = pl.BlockSpec((ROW_TILE, D_MODEL), lambda i: (i, 0))
    halo = pl.BlockSpec((HALO, D_MODEL), lambda i: (jnp.maximum(i * (ROW_TILE // HALO) - 1, 0), 0))
    return pl.pallas_call(
        functools.partial(_ffn_kernel, tiles_per_seq=seq_len // ROW_TILE),
        grid=(t // ROW_TILE,),
        in_specs=[row, halo, _resident((1, D_MODEL)), _resident((1, D_MODEL)),
                  _resident((D_MODEL, 2 * FFN_HIDDEN)), _resident((CONV_WIDTH, 2 * FFN_HIDDEN)),
                  _resident((1, 2 * FFN_HIDDEN)), _resident((FFN_HIDDEN, D_MODEL))],
        out_specs=row,
        out_shape=jax.ShapeDtypeStruct((t, D_MODEL), F32),
        scratch_shapes=[pltpu.VMEM((ROW_TILE, D_MODEL), F32),
                        pltpu.VMEM((2, ROW_TILE, FFN_GROUP * FFN_CHUNK), BF16)],
        compiler_params=_cparams(1),
        name="ffn",
    )(x2, x2, gpre, gpost, wup, cw, cb, wd)


def _t5_bucket(dist):
    n = jnp.maximum(dist, 0)
    nf = jnp.maximum(n, 1).astype(F32)
    large = MAX_EXACT + (jnp.log(nf / MAX_EXACT) / math.log(REL_MAX_DISTANCE / MAX_EXACT)
                         * (NUM_BUCKETS - MAX_EXACT)).astype(jnp.int32)
    large = jnp.minimum(large, NUM_BUCKETS - 1)
    return jnp.where(n < MAX_EXACT, n, large)


def _bias_tables(rel_bias):
    bm = rel_bias[:, :MOBA_HEADS].T.astype(F32)
    bs = rel_bias[:, MOBA_HEADS:].T.astype(F32)

    def lookup(table, dist):
        bucket = _t5_bucket(dist)[None]
        out = jnp.zeros((table.shape[0],) + dist.shape, F32)
        for bkt in range(NUM_BUCKETS):
            out = jnp.where(bucket == bkt, table[:, bkt][:, None, None], out)
        return out

    kj = jnp.arange(MOBA_BLOCK)[:, None]
    qi = jnp.arange(MOBA_BLOCK)[None, :]
    far = bm[:, NUM_BUCKETS - 1][:, None, None]
    own = jnp.where((qi - kj) >= 0, (lookup(bm, qi - kj) - far) * LOG2E, NEG)
    prev = (lookup(bm, MOBA_BLOCK + qi - kj) - far) * LOG2E
    tab_moba = jnp.stack([prev, own, jnp.full_like(own, NEG)], axis=1)
    kj = jnp.arange(2 * SWA_BLOCK)[:, None]
    qi = jnp.arange(SWA_BLOCK)[None, :]
    dist = SWA_BLOCK + qi - kj
    tab_swa = jnp.where((dist >= 0) & (dist < SWA_WINDOW), lookup(bs, dist) * LOG2E, NEG)
    return tab_moba, tab_swa


def _head_variants(w, heads):
    w3 = w.reshape(w.shape[0], heads, 1, -1)
    z = jnp.zeros_like(w3)
    lo = jnp.concatenate([w3, z], axis=3)
    hi = jnp.concatenate([z, w3], axis=3)
    return jnp.concatenate([lo, hi], axis=2).reshape(w.shape[0], heads * 2 * LANES)


def _proj_weight(w_in):
    o = 0
    parts = {}
    for name, width in (("qm", MOBA_W), ("km", MOBA_W), ("vm", MOBA_W), ("qs", SWA_QW), ("ks", SWA_KVW),
                        ("vs", SWA_KVW), ("qe", MEM_W)):
        parts[name] = w_in[:, o:o + width]
        o += width
    q_scale = LOG2E / math.sqrt(MOBA_HEAD_DIM)
    w = jnp.concatenate([parts["qm"] * q_scale, parts["km"], parts["qs"] * q_scale,
                         _head_variants(parts["ks"], SWA_KV_HEADS), parts["qe"]], axis=1)
    vs3 = parts["vs"].reshape(-1, SWA_KV_HEADS, SWA_HEAD_DIM)
    vs_dup = jnp.concatenate([vs3, vs3], axis=2).reshape(-1, SWA_VT_ROWS)
    w_t = jnp.concatenate([parts["vm"], vs_dup], axis=1).T
    return w.astype(BF16), w_t.astype(BF16), w_in[:, o:].astype(BF16)


def kernel(x, mem, norm_mix_pre, norm_mix_post, norm_ffn_pre, norm_ffn_post, norm_mem, w_in, rel_bias, swa_sinks,
           w_mem_kv, w_branch_moba, w_branch_swa, w_branch_mem, w_out, w_ffn_up, ffn_conv_w, ffn_conv_b,
           w_ffn_down):
    b, s, d = x.shape
    m_len = mem.shape[1]
    assert d == D_MODEL and s % ROW_TILE == 0 and s // MOBA_BLOCK <= 32 and m_len % LANES == 0
    depth = w_in.shape[0]
    tab_moba, tab_swa = _bias_tables(rel_bias)
    x2 = x.reshape(b * s, d).astype(F32)
    vec = lambda v: v.reshape(1, -1).astype(F32)
    nblk = s // MOBA_BLOCK
    r3 = lambda a: a.reshape(b, s, a.shape[-1])
    for l in range(depth):
        w_qkv, w_t, w_gate = _proj_weight(w_in[l])
        qm, km, vt, qs, ks, vst, qe, sel = _proj(x2, vec(norm_mix_pre[l]), w_qkv, w_t, s)
        kme, vmt = _memkv(mem.astype(F32), vec(norm_mem[l]), w_mem_kv[l][:, :MEM_W].astype(BF16),
                          w_mem_kv[l][:, MEM_W:].T.astype(BF16))
        o_mb = _moba(r3(qm), r3(km), vt.reshape(b, nblk, MOBA_W, MOBA_BLOCK), sel, tab_moba)
        sinks = jnp.broadcast_to(swa_sinks[l].astype(F32)[:, None] * LOG2E, (SWA_Q_HEADS, LANES))
        o_sw, o_me = _local(r3(qs), r3(ks), vst.reshape(b, s // SWA_BLOCK, SWA_VT_ROWS, SWA_BLOCK), tab_swa, sinks,
                            r3(qe), kme, vmt)
        w_branch = jnp.stack([w_branch_moba[l], w_branch_swa[l], w_branch_mem[l]]).astype(BF16)
        x2 = _merge(x2, o_mb.reshape(b * s, -1), o_sw.reshape(b * s, -1), o_me.reshape(b * s, -1),
                    vec(norm_mix_pre[l]), vec(norm_mix_post[l]), w_gate, w_branch, w_out[l].astype(BF16))
        x2 = _ffn(x2, vec(norm_ffn_pre[l]), vec(norm_ffn_post[l]), w_ffn_up[l].astype(BF16),
                  ffn_conv_w[l].astype(F32), vec(ffn_conv_b[l]), w_ffn_down[l].astype(BF16), s)
    return x2.reshape(b, s, d).astype(x.dtype)
```
